```python
import jax, jax.numpy as jnp
from jax import lax
import numpy as np

D_MODEL = 1024
BATCH = 16
SEQ = 2048
DEPTH = 4

CTX_LEN = 256
GRID_W = 64
BLOCK = 128
WINDOW = 128

RET_HEADS = 4
RET_DK = 128
RET_DV = 128
RET_W = RET_HEADS * RET_DV

HEAD_DIM = 64
WIN_Q_HEADS = 8
WIN_KV_HEADS = 2
AX_Q_HEADS = 8
AX_KV_HEADS = 2
WIN_W = WIN_Q_HEADS * HEAD_DIM
AX_W = AX_Q_HEADS * HEAD_DIM

ROPE_BASE = 10000.0
EPS = 1e-6
NEG = -1e30

IN_SIZES = (RET_HEADS * RET_DK, RET_HEADS * RET_DK, RET_W, RET_W,
            WIN_W, WIN_KV_HEADS * HEAD_DIM, WIN_KV_HEADS * HEAD_DIM, WIN_W,
            AX_W, AX_KV_HEADS * HEAD_DIM, AX_KV_HEADS * HEAD_DIM, AX_W,
            3 * D_MODEL)
IN_W = sum(IN_SIZES)

kernel_name = 'hybrid_retention_window_axial_dit_block'


def rms_norm(x, w):
    xf = x.astype(jnp.float32)
    y = xf * lax.rsqrt(jnp.mean(xf * xf, axis=-1, keepdims=True) + EPS)
    return (y * w.astype(jnp.float32)).astype(x.dtype)


def heads(t, h):
    return t.reshape(*t.shape[:-1], h, t.shape[-1] // h)


def split_last(t, sizes):
    return jnp.split(t, np.cumsum(sizes)[:-1].tolist(), axis=-1)


def apply_rope(x, ang):
    d = x.shape[-1]
    xf = x.astype(jnp.float32).reshape(*x.shape[:-1], d // 2, 2)
    x0, x1 = xf[..., 0], xf[..., 1]
    cos = jnp.cos(ang)[None, :, None, :]
    sin = jnp.sin(ang)[None, :, None, :]
    out = jnp.stack([x0 * cos - x1 * sin, x0 * sin + x1 * cos], axis=-1)
    return out.reshape(x.shape).astype(x.dtype)


def axial_angles(n_tokens, dim):
    rows = n_tokens // GRID_W
    r, col = jnp.meshgrid(jnp.arange(rows, dtype=jnp.float32),
                          jnp.arange(GRID_W, dtype=jnp.float32), indexing='ij')
    quarter = dim // 4
    freqs = ROPE_BASE ** (-jnp.arange(quarter, dtype=jnp.float32) / quarter)
    return jnp.concatenate([r.reshape(-1)[:, None] * freqs[None],
                            col.reshape(-1)[:, None] * freqs[None]], axis=-1)


def retnet_angles(pos):
    theta = ROPE_BASE ** (-jnp.linspace(0.0, 1.0, RET_DK // 2, dtype=jnp.float32))
    return pos[:, None] * theta[None]


def retention_final_state(k, v, log_gamma):
    n = k.shape[1]
    w = jnp.exp((n - 1 - jnp.arange(n, dtype=jnp.float32))[:, None] * log_gamma[None])
    return jnp.einsum('bnhk,nh,bnhv->bhkv', k.astype(jnp.float32), w, v.astype(jnp.float32))


def retention_chunkwise(q, k, v, log_gamma, state0, strict):
    B, n, H, dk = q.shape
    dv = v.shape[-1]
    nc = n // BLOCK
    qc = q.astype(jnp.float32).reshape(B, nc, BLOCK, H, dk)
    kc = k.astype(jnp.float32).reshape(B, nc, BLOCK, H, dk)
    vc = v.astype(jnp.float32).reshape(B, nc, BLOCK, H, dv)
    idx = jnp.arange(BLOCK, dtype=jnp.float32)
    diff = idx[:, None] - idx[None, :]
    keep = (diff > 0) if strict else (diff >= 0)
    intra = jnp.where(keep[None], jnp.exp(jnp.where(keep, diff, 0.0)[None] * log_gamma[:, None, None]), 0.0)
    q_decay = jnp.exp((idx + 1.0)[None] * log_gamma[:, None])
    k_decay = jnp.exp((BLOCK - 1.0 - idx)[None] * log_gamma[:, None])
    chunk_decay = jnp.exp(BLOCK * log_gamma)
    s = jnp.einsum('bnihk,bnjhk->bnhij', qc, kc) * intra[None, None]
    o_intra = jnp.einsum('bnhij,bnjhv->bnihv', s, vc)
    chunk_kv = jnp.einsum('bnjhk,hj,bnjhv->nbhkv', kc, k_decay, vc)

    def step(R, kv):
        return chunk_decay[None, :, None, None] * R + kv, R

    _, R_prev = lax.scan(step, state0, chunk_kv)
    o_cross = jnp.einsum('bnihk,nbhkv,hi->bnihv', qc, R_prev, q_decay)
    return (o_intra + o_cross).reshape(B, n, H, dv)


def retention_branch(q_c, k_c, v_c, q_l, k_l, v_l, a_fwd, a_bwd, need_ctx):
    B, L, H, _ = k_c.shape
    S = k_l.shape[1]
    ang_c = retnet_angles(jnp.arange(L, dtype=jnp.float32))
    ang_l = retnet_angles(L + jnp.arange(S, dtype=jnp.float32))
    k_scale = RET_DK ** -0.5
    out_l = 0.0
    out_c = 0.0
    for a, backward in ((a_fwd, False), (a_bwd, True)):
        lg = jax.nn.log_sigmoid(a.astype(jnp.float32))
        flip = (lambda t: jnp.flip(t, axis=1)) if backward else (lambda t: t)
        qc1 = apply_rope(flip(q_c), ang_c)
        kc1 = apply_rope(flip(k_c), ang_c) * k_scale
        vc1 = flip(v_c)
        ql1 = apply_rope(flip(q_l), ang_l)
        kl1 = apply_rope(flip(k_l), ang_l) * k_scale
        vl1 = flip(v_l)
        state = retention_final_state(kc1, vc1, lg)
        out_l = out_l + flip(retention_chunkwise(ql1, kl1, vl1, lg, state, backward))
        if need_ctx:
            zeros = jnp.zeros((B, H, RET_DK, RET_DV), jnp.float32)
            out_c = out_c + flip(retention_chunkwise(qc1, kc1, vc1, lg, zeros, backward))

    def head_norm(o, dtype):
        o = o * lax.rsqrt(jnp.mean(o * o, axis=-1, keepdims=True) + EPS)
        return o.reshape(*o.shape[:2], RET_W).astype(dtype)

    o_l = head_norm(out_l, q_l.dtype)
    o_c = head_norm(out_c, q_c.dtype) if need_ctx else None
    return o_c, o_l


def gqa_attend(q, k, v, scale):
    s = jnp.einsum('bigrd,bjgd->bgrij', q, k).astype(jnp.float32) * scale
    p = jax.nn.softmax(s, axis=-1).astype(v.dtype)
    return jnp.einsum('bgrij,bjgd->bigrd', p, v)


def sink_softmax(sink, scores):
    col = jnp.broadcast_to(sink[:, :, None, None], scores[0].shape[:-1] + (1,))
    p = jax.nn.softmax(jnp.concatenate([col] + list(scores), axis=-1), axis=-1)[..., 1:]
    return split_last(p, [s.shape[-1] for s in scores])


def window_branch(q_c, k_c, v_c, q_l, k_l, v_l, sink, need_ctx):
    B, S, Hq, dh = q_l.shape
    G = k_l.shape[2]
    R = Hq // G
    L = k_c.shape[1]
    nb = S // BLOCK
    scale = dh ** -0.5
    sink_g = sink.astype(jnp.float32).reshape(G, R)
    ang = axial_angles(S, dh)
    q_l = apply_rope(q_l, ang)
    k_l = apply_rope(k_l, ang)
    qb = q_l.reshape(B, nb, BLOCK, G, R, dh)
    pad = ((0, 0), (BLOCK, BLOCK), (0, 0), (0, 0))
    kp = jnp.pad(k_l, pad).reshape(B, nb + 2, BLOCK, G, dh)
    vp = jnp.pad(v_l, pad).reshape(B, nb + 2, BLOCK, G, dh)
    kw = jnp.concatenate([kp[:, :-2], kp[:, 1:-1], kp[:, 2:]], axis=2)
    vw = jnp.concatenate([vp[:, :-2], vp[:, 1:-1], vp[:, 2:]], axis=2)
    blk = jnp.arange(nb)[:, None] * BLOCK
    qpos = blk + jnp.arange(BLOCK)[None]
    kpos = blk - BLOCK + jnp.arange(3 * BLOCK)[None]
    valid = ((jnp.abs(qpos[:, :, None] - kpos[:, None, :]) <= WINDOW)
             & (kpos >= 0)[:, None, :] & (kpos < S)[:, None, :])
    s_win = jnp.einsum('bnigrd,bnjgd->bngrij', qb, kw).astype(jnp.float32) * scale
    s_win = jnp.where(valid[None, :, None, None], s_win, NEG)
    s_ctx = jnp.einsum('bnigrd,bjgd->bngrij', qb, k_c).astype(jnp.float32) * scale
    p_ctx, p_win = sink_softmax(sink_g, [s_ctx, s_win])
    o = (jnp.einsum('bngrij,bjgd->bnigrd', p_ctx.astype(v_c.dtype), v_c)
         + jnp.einsum('bngrij,bnjgd->bnigrd', p_win.astype(vw.dtype), vw))
    o_l = o.reshape(B, S, Hq * dh)
    o_c = None
    if need_ctx:
        qcg = q_c.reshape(B, L, G, R, dh)
        s_cc = jnp.einsum('bigrd,bjgd->bgrij', qcg, k_c).astype(jnp.float32) * scale
        (p_cc,) = sink_softmax(sink_g, [s_cc])
        o_c = jnp.einsum('bgrij,bjgd->bigrd', p_cc.astype(v_c.dtype), v_c).reshape(B, L, Hq * dh)
    return o_c, o_l


def axial_branch(q_c, k_c, v_c, q_l, k_l, v_l, q_gain, k_gain, need_ctx):
    B, S, Hq, dh = q_l.shape
    G = k_l.shape[2]
    R = Hq // G
    L = k_c.shape[1]
    nb = S // BLOCK
    scale = dh ** -0.5
    ang = axial_angles(S, dh)
    q_l = apply_rope(rms_norm(q_l, q_gain), ang)
    k_l = apply_rope(rms_norm(k_l, k_gain), ang)
    k_c = rms_norm(k_c, k_gain)
    k_all = jnp.concatenate([k_c, k_l], axis=1)
    v_all = jnp.concatenate([v_c, v_l], axis=1)
    qb = jnp.moveaxis(q_l.reshape(B, nb, BLOCK, G, R, dh), 1, 0)
    o = lax.map(lambda qblk: gqa_attend(qblk, k_all, v_all, scale), qb)
    o_l = jnp.moveaxis(o, 0, 1).reshape(B, S, Hq * dh)
    o_c = None
    if need_ctx:
        qcg = rms_norm(q_c, q_gain).reshape(B, L, G, R, dh)
        o_c = gqa_attend(qcg, k_c, v_c, scale).reshape(B, L, Hq * dh)
    return o_c, o_l


def merge_branches(o_ret, g_ret, o_win, g_win, o_ax, g_ax, merge_logits, w_pr, w_pw, w_pa, w_out):
    b_ret = (o_ret * jax.nn.silu(g_ret)) @ w_pr
    b_win = (o_win * jax.nn.silu(g_win)) @ w_pw
    b_ax = (o_ax * jax.nn.silu(g_ax)) @ w_pa
    m_ret, m_win, m_ax = jnp.split(jax.nn.sigmoid(merge_logits), 3, axis=-1)
    return (m_ret * b_ret + m_win * b_win + m_ax * b_ax) @ w_out


def layer(x, xc, c, c_ctx, norm_w, w_mod, b_mod, w_in, a_fwd, a_bwd, sink, q_gain, k_gain,
          w_pr, w_pw, w_pa, w_out, need_ctx):
    shift, scale, gate = jnp.split(jax.nn.silu(c) @ w_mod + b_mod, 3, axis=-1)
    shift_c, scale_c, gate_c = jnp.split(jax.nn.silu(c_ctx) @ w_mod + b_mod, 3, axis=-1)
    h = rms_norm(x, norm_w) * (1.0 + scale[:, None]) + shift[:, None]
    hc = rms_norm(xc, norm_w) * (1.0 + scale_c) + shift_c
    rq, rk, rv, rg, wq, wk, wv, wg, aq, ak, av, ag, mg = split_last(h @ w_in, IN_SIZES)
    crq, crk, crv, crg, cwq, cwk, cwv, cwg, caq, cak, cav, cag, cmg = split_last(hc @ w_in, IN_SIZES)
    o_ret_c, o_ret = retention_branch(heads(crq, RET_HEADS), heads(crk, RET_HEADS), heads(crv, RET_HEADS),
                                      heads(rq, RET_HEADS), heads(rk, RET_HEADS), heads(rv, RET_HEADS),
                                      a_fwd, a_bwd, need_ctx)
    o_win_c, o_win = window_branch(heads(cwq, WIN_Q_HEADS), heads(cwk, WIN_KV_HEADS), heads(cwv, WIN_KV_HEADS),
                                   heads(wq, WIN_Q_HEADS), heads(wk, WIN_KV_HEADS), heads(wv, WIN_KV_HEADS),
                                   sink, need_ctx)
    o_ax_c, o_ax = axial_branch(heads(caq, AX_Q_HEADS), heads(cak, AX_KV_HEADS), heads(cav, AX_KV_HEADS),
                                heads(aq, AX_Q_HEADS), heads(ak, AX_KV_HEADS), heads(av, AX_KV_HEADS),
                                q_gain, k_gain, need_ctx)
    x = x + gate[:, None] * merge_branches(o_ret, rg, o_win, wg, o_ax, ag, mg, w_pr, w_pw, w_pa, w_out)
    if need_ctx:
        xc = xc + gate_c * merge_branches(o_ret_c, crg, o_win_c, cwg, o_ax_c, cag, cmg, w_pr, w_pw, w_pa, w_out)
    return x, xc


def setup_inputs(seed: int = 0) -> dict:
    key = jax.random.key(seed)
    ks = jax.random.split(key, 20)
    f32 = jnp.float32
    nrm = lambda k, shp: jax.random.normal(k, shp, f32)
    base_decay = 1.0 - 2.0 ** (-5.0 - jnp.arange(RET_HEADS, dtype=f32))
    base_logit = jnp.log(base_decay / (1.0 - base_decay))
    return {
        'x': nrm(ks[0], (BATCH, SEQ, D_MODEL)),
        'c': nrm(ks[1], (BATCH, D_MODEL)),
        'ctx': nrm(ks[2], (BATCH, CTX_LEN, D_MODEL)),
        'c_ctx': nrm(ks[3], (D_MODEL,)),
        'norm_w': 1.0 + 0.02 * nrm(ks[4], (DEPTH, D_MODEL)),
        'w_mod': nrm(ks[5], (DEPTH, D_MODEL, 3 * D_MODEL)) * (0.5 * D_MODEL ** -0.5),
        'b_mod': 0.01 * nrm(ks[6], (DEPTH, 3 * D_MODEL)),
        'w_in': nrm(ks[7], (DEPTH, D_MODEL, IN_W)) * D_MODEL ** -0.5,
        'ret_decay_fwd': base_logit[None] + 0.05 * nrm(ks[8], (DEPTH, RET_HEADS)),
        'ret_decay_bwd': base_logit[None] + 0.05 * nrm(ks[9], (DEPTH, RET_HEADS)),
        'win_sink': 0.5 * nrm(ks[10], (DEPTH, WIN_Q_HEADS)),
        'ax_q_gain': 1.0 + 0.02 * nrm(ks[11], (DEPTH, HEAD_DIM)),
        'ax_k_gain': 1.0 + 0.02 * nrm(ks[12], (DEPTH, HEAD_DIM)),
        'w_proj_ret': nrm(ks[13], (DEPTH, RET_W, D_MODEL)) * RET_W ** -0.5,
        'w_proj_win': nrm(ks[14], (DEPTH, WIN_W, D_MODEL)) * WIN_W ** -0.5,
        'w_proj_ax': nrm(ks[15], (DEPTH, AX_W, D_MODEL)) * AX_W ** -0.5,
        'w_out': nrm(ks[16], (DEPTH, D_MODEL, D_MODEL)) * D_MODEL ** -0.5,
        'final_norm_w': 1.0 + 0.02 * nrm(ks[17], (D_MODEL,)),
    }


def reference(x, c, ctx, c_ctx, norm_w, w_mod, b_mod, w_in, ret_decay_fwd, ret_decay_bwd, win_sink,
              ax_q_gain, ax_k_gain, w_proj_ret, w_proj_win, w_proj_ax, w_out, final_norm_w):
    xc = ctx
    for l in range(DEPTH):
        x, xc = layer(x, xc, c, c_ctx, norm_w[l], w_mod[l], b_mod[l], w_in[l],
                      ret_decay_fwd[l], ret_decay_bwd[l], win_sink[l], ax_q_gain[l], ax_k_gain[l],
                      w_proj_ret[l], w_proj_win[l], w_proj_ax[l], w_out[l], l < DEPTH - 1)
    return rms_norm(x, final_norm_w)
```

```python
import functools

import numpy as np
import jax
import jax.numpy as jnp
from jax import lax
from jax.experimental import pallas as pl
from jax.experimental.pallas import tpu as pltpu

D_MODEL = 1024
BATCH = 16
SEQ = 2048
DEPTH = 4
CTX_LEN = 256
TOK = CTX_LEN + SEQ
GRID_W = 64
BLOCK = 128
RET_HEADS = 4
RET_DK = 128
HEAD_DIM = 64
ROPE_BASE = 10000.0
EPS = 1e-6
NEG = -1e30

LANES = 128
NBLK = TOK // BLOCK
CTX_BLKS = CTX_LEN // BLOCK
LAT_BLKS = SEQ // BLOCK

TM = 256
VMEM_LIMIT = 52 * 1024 * 1024

F32 = jnp.float32
BF16 = jnp.bfloat16

_IN_SIZES = (512, 512, 512, 512, 512, 128, 128, 512, 512, 128, 128, 512, 3 * D_MODEL)
_OFF = np.concatenate([[0], np.cumsum(_IN_SIZES)]).astype(np.int64)
(_RQ, _RK, _RV, _RG, _WQ, _WK, _WV, _WG, _AQ, _AK, _AV, _AG, _MG) = [int(o) for o in _OFF[:-1]]

_EV64 = np.arange(0, 128, 2)
_OD64 = np.arange(1, 128, 2)
_EV32 = np.arange(0, 64, 2)
_OD32 = np.arange(1, 64, 2)


def _ret_cols(base):
    return np.concatenate([base + h * 128 + np.concatenate([_EV64, _OD64]) for h in range(RET_HEADS)])


def _attq_cols(base):
    out = []
    for j in range(4):
        ha, hb = base + (2 * j) * 64, base + (2 * j + 1) * 64
        out.append(np.concatenate([ha + _EV32, hb + _EV32, ha + _OD32, hb + _OD32]))
    return np.concatenate(out)


def _attk_cols(base, first, second):
    ga, gb = base + first * 64, base + second * 64
    return np.concatenate([ga + _EV32, gb + _EV32, ga + _OD32, gb + _OD32])


def _attv_cols(base, first, second):
    return np.concatenate([base + first * 64 + np.arange(64), base + second * 64 + np.arange(64)])


_QKV_COLS = np.concatenate([
    _ret_cols(_RQ), _ret_cols(_RK), _RV + np.arange(512),
    _attq_cols(_WQ), _attk_cols(_WK, 0, 1), _attk_cols(_WK, 1, 0), _attv_cols(_WV, 0, 1), _attv_cols(_WV, 1, 0),
    _attq_cols(_AQ), _attk_cols(_AK, 0, 1), _attk_cols(_AK, 1, 0), _attv_cols(_AV, 0, 1), _attv_cols(_AV, 1, 0),
]).astype(np.int32)
_GATE_COLS = np.concatenate([_RG + np.arange(512), _WG + np.arange(512), _AG + np.arange(512),
                             _MG + np.arange(3 * D_MODEL)]).astype(np.int32)
NQKV = int(_QKV_COLS.shape[0])
NGATE = int(_GATE_COLS.shape[0])
_GAIN_LANES = np.concatenate([_EV32, _EV32, _OD32, _OD32]).astype(np.int32)

_ATT_GROUPS = (((0, 1), False, 0), ((0, 1), True, 1), ((2, 3), False, 1), ((2, 3), True, 0))

_NT = (((1,), (1,)), ((), ()))
_TN = (((0,), (0,)), ((), ()))


def _const_spec(shape):
    nd = len(shape)
    return pl.BlockSpec(shape, lambda *_: (0,) * nd, pipeline_mode=pl.Buffered(1))


def _params(*sem):
    return pltpu.CompilerParams(dimension_semantics=sem, vmem_limit_bytes=VMEM_LIMIT)


def _sigmoid(z):
    return 1.0 / (1.0 + jnp.exp(-z))


def _mod_kernel(c_ref, w_ref, b_ref, o_ref):
    cv = c_ref[...]
    s = cv * _sigmoid(cv)
    o_ref[0] = jnp.dot(s, w_ref[0], preferred_element_type=F32,
                       precision=lax.Precision.HIGHEST) + b_ref[0]


def _mod_call(cvec, w_mod, b_mod):
    rows = cvec.shape[0]
    return pl.pallas_call(
        _mod_kernel,
        grid=(DEPTH, 3),
        in_specs=[pl.BlockSpec((rows, D_MODEL), lambda l, j: (0, 0)),
                  pl.BlockSpec((1, D_MODEL, D_MODEL), lambda l, j: (l, 0, j)),
                  pl.BlockSpec((1, 1, D_MODEL), lambda l, j: (l, 0, j))],
        out_specs=pl.BlockSpec((1, rows, D_MODEL), lambda l, j: (l, 0, j)),
        out_shape=jax.ShapeDtypeStruct((DEPTH, rows, 3 * D_MODEL), F32),
        compiler_params=_params("arbitrary", "arbitrary"),
        name="adaln_mod",
    )(cvec, w_mod, b_mod.reshape(DEPTH, 1, 3 * D_MODEL))


def _modulated_norm(x, mod_ref, nw_ref):
    ms = jnp.mean(x * x, axis=-1, keepdims=True)
    y = x * lax.rsqrt(ms + EPS) * nw_ref[...]
    shift = mod_ref[0, :, 0:D_MODEL]
    scale = mod_ref[0, :, D_MODEL:2 * D_MODEL]
    return (y * (1.0 + scale) + shift).astype(BF16)


def _qkv_kernel(x_ref, mod_ref, nw_ref, w_ref, cf_ref, sf_ref, cb_ref, sb_ref, ca_ref, sa_ref,
                gq_ref, gk_ref, seg_ref,
                qf_ref, qb_ref, kf_ref, kb_ref, rv_ref, wq_ref, wkv_ref, aq_ref, akv_ref):
    h = _modulated_norm(x_ref[0], mod_ref, nw_ref)

    def proj(c0):
        return jnp.dot(h, w_ref[:, c0:c0 + 512], preferred_element_type=F32)

    def rope(v, c_ref, s_ref):
        return v * c_ref[...] + pltpu.roll(v, 64, 1) * s_ref[...]

    def lanes(a, j):
        return a[:, j * LANES:(j + 1) * LANES]

    def qk_norm(v, g_ref):
        ss = jnp.dot((v * v).astype(BF16), seg_ref[...], preferred_element_type=F32)
        return v * lax.rsqrt(ss * (1.0 / HEAD_DIM) + EPS) * g_ref[...]

    k_scale = RET_DK ** -0.5
    q_scale = HEAD_DIM ** -0.5

    acc = proj(0)
    for j in range(4):
        a = lanes(acc, j)
        qf_ref[0, :, j * LANES:(j + 1) * LANES] = rope(a, cf_ref, sf_ref).astype(BF16)
        qb_ref[0, :, j * LANES:(j + 1) * LANES] = rope(a, cb_ref, sb_ref).astype(BF16)
    acc = proj(512)
    for j in range(4):
        a = lanes(acc, j)
        kf_ref[0, :, j * LANES:(j + 1) * LANES] = (rope(a, cf_ref, sf_ref) * k_scale).astype(BF16)
        kb_ref[0, :, j * LANES:(j + 1) * LANES] = (rope(a, cb_ref, sb_ref) * k_scale).astype(BF16)
    rv_ref[0] = proj(1024).astype(BF16)

    acc = proj(1536)
    for j in range(4):
        wq_ref[0, :, j * LANES:(j + 1) * LANES] = (rope(lanes(acc, j), ca_ref, sa_ref) * q_scale).astype(BF16)
    acc = proj(2048)
    for j in range(2):
        wkv_ref[0, :, j * LANES:(j + 1) * LANES] = rope(lanes(acc, j), ca_ref, sa_ref).astype(BF16)
    wkv_ref[0, :, 2 * LANES:4 * LANES] = acc[:, 2 * LANES:4 * LANES].astype(BF16)

    acc = proj(2560)
    for j in range(4):
        a = qk_norm(lanes(acc, j), gq_ref)
        aq_ref[0, :, j * LANES:(j + 1) * LANES] = (rope(a, ca_ref, sa_ref) * q_scale).astype(BF16)
    acc = proj(3072)
    for j in range(2):
        a = qk_norm(lanes(acc, j), gk_ref)
        akv_ref[0, :, j * LANES:(j + 1) * LANES] = rope(a, ca_ref, sa_ref).astype(BF16)
    akv_ref[0, :, 2 * LANES:4 * LANES] = acc[:, 2 * LANES:4 * LANES].astype(BF16)


def _mod_spec():
    return pl.BlockSpec((1, 1, 3 * D_MODEL), lambda b, t: (jnp.where(t == 0, BATCH, b), 0, 0))


def _qkv_call(xx, mod_l, nw, wqkv, tabs, gq, gk, seg):
    tok_spec = pl.BlockSpec((1, TM, 512), lambda b, t: (b, t, 0))
    tab_spec = pl.BlockSpec((TM, LANES), lambda b, t: (t, 0))
    out_sds = jax.ShapeDtypeStruct((BATCH, TOK, 512), BF16)
    return pl.pallas_call(
        _qkv_kernel,
        grid=(BATCH, TOK // TM),
        in_specs=[pl.BlockSpec((1, TM, D_MODEL), lambda b, t: (b, t, 0)),
                  _mod_spec(),
                  _const_spec((1, D_MODEL)),
                  _const_spec((D_MODEL, NQKV)),
                  tab_spec, tab_spec, tab_spec, tab_spec, tab_spec, tab_spec,
                  _const_spec((1, LANES)), _const_spec((1, LANES)), _const_spec((LANES, LANES))],
        out_specs=[tok_spec] * 9,
        out_shape=[out_sds] * 9,
        compiler_params=_params("parallel", "arbitrary"),
        name="qkv_proj",
    )(xx, mod_l, nw, wqkv, *tabs, gq, gk, seg)


def _log_sigmoid(a):
    return jnp.minimum(a, 0.0) - jnp.log1p(jnp.exp(-jnp.abs(a)))


def _ret_kernel(af_ref, ab_ref, qf_ref, qb_ref, kf_ref, kb_ref, v_ref, o_ref, acc_ref, *, need_ctx):
    lgf = _log_sigmoid(af_ref[0])[0:1, :]
    lgb = _log_sigmoid(ab_ref[0])[0:1, :]
    ri = lax.broadcasted_iota(jnp.int32, (BLOCK, BLOCK), 0).astype(F32)
    ci = lax.broadcasted_iota(jnp.int32, (BLOCK, BLOCK), 1).astype(F32)
    d = ri - ci
    dec_f = (jnp.where(d >= 0, jnp.exp(jnp.maximum(d, 0.0) * lgf), 0.0),
             jnp.exp((ri + 1.0) * lgf), jnp.exp((BLOCK - 1.0 - ri) * lgf), jnp.exp(float(BLOCK) * lgf))
    dec_b = (jnp.where(d < 0, jnp.exp(jnp.maximum(-d, 0.0) * lgb), 0.0),
             jnp.exp((float(BLOCK) - ri) * lgb), jnp.exp(ri * lgb), jnp.exp(float(BLOCK) * lgb))

    def step(q_ref, k_ref, c, state, dec, want_o):
        intra, q_dec, k_dec, c_dec = dec
        rows = slice(c * BLOCK, (c + 1) * BLOCK)
        q = q_ref[0, rows, :]
        k = k_ref[0, rows, :]
        v = v_ref[0, rows, :]
        o = None
        if want_o:
            s = lax.dot_general(q, k, _NT, preferred_element_type=F32)
            o = jnp.dot((s * intra).astype(BF16), v, preferred_element_type=F32)
            o = o + jnp.dot(q, state.astype(BF16), preferred_element_type=F32) * q_dec
        kd = (k.astype(F32) * k_dec).astype(BF16)
        kv = lax.dot_general(kd, v, _TN, preferred_element_type=F32)
        return o, c_dec * state + kv

    def emit(c, o, seen):
        if c < CTX_BLKS and not need_ctx:
            return
        rows = slice(c * BLOCK, (c + 1) * BLOCK)
        if c not in seen:
            acc_ref[rows, :] = o
            seen.add(c)
            return
        tot = acc_ref[rows, :] + o
        tot = tot * lax.rsqrt(jnp.mean(tot * tot, axis=-1, keepdims=True) + EPS)
        first = 0 if need_ctx else CTX_BLKS
        o_ref[0, (c - first) * BLOCK:(c - first + 1) * BLOCK, :] = tot.astype(o_ref.dtype)

    bwd_order = list(range(CTX_BLKS - 1, -1, -1)) + list(range(NBLK - 1, CTX_BLKS - 1, -1))
    st_f = jnp.zeros((RET_DK, BLOCK), F32)
    st_b = jnp.zeros((RET_DK, BLOCK), F32)
    seen = set()
    for i in range(NBLK):
        cf, cb = i, bwd_order[i]
        of, st_f = step(qf_ref, kf_ref, cf, st_f, dec_f, need_ctx or cf >= CTX_BLKS)
        ob, st_b = step(qb_ref, kb_ref, cb, st_b, dec_b, need_ctx or cb >= CTX_BLKS)
        emit(cf, of, seen)
        emit(cb, ob, seen)


def _ret_call(a_f, a_b, qf, qb, kf, kb, rv, need_ctx):
    n_out = TOK if need_ctx else SEQ
    head_spec = pl.BlockSpec((1, TOK, LANES), lambda b, h: (b, 0, h))
    a_spec = pl.BlockSpec((1, 8, LANES), lambda b, h: (h, 0, 0))
    return pl.pallas_call(
        functools.partial(_ret_kernel, need_ctx=need_ctx),
        grid=(BATCH, RET_HEADS),
        in_specs=[a_spec, a_spec, head_spec, head_spec, head_spec, head_spec, head_spec],
        out_specs=pl.BlockSpec((1, n_out, LANES), lambda b, h: (b, 0, h)),
        out_shape=jax.ShapeDtypeStruct((BATCH, n_out, 512), BF16),
        scratch_shapes=[pltpu.VMEM((TOK, LANES), F32)],
        compiler_params=_params("parallel", "arbitrary"),
        name="retention",
    )(a_f, a_b, qf, qb, kf, kb, rv)


def _slot_masks():
    lane = lax.broadcasted_iota(jnp.int32, (BLOCK, LANES), 1)
    slot_a = (lane & 32) == 0
    return slot_a, lane < 64


def _stacked_queries(q_ref, cols, use_b, slot_a):
    parts = []
    for c in cols:
        qv = q_ref[0, :, c * LANES:(c + 1) * LANES].astype(F32)
        qv = jnp.where(slot_a, 0.0, qv) if use_b else jnp.where(slot_a, qv, 0.0)
        parts.append(qv.astype(BF16))
    return jnp.concatenate(parts, axis=0)


def _store_heads(o_ref, res, low_half):
    for c in range(4):
        g, idx = c // 2, c % 2
        ra = res[2 * g][idx * BLOCK:(idx + 1) * BLOCK]
        rb = res[2 * g + 1][idx * BLOCK:(idx + 1) * BLOCK]
        o_ref[0, :, c * LANES:(c + 1) * LANES] = jnp.where(low_half, ra, rb).astype(o_ref.dtype)


def _ax_kernel(q_ref, kv_ref, o_ref, *, blk_off):
    slot_a, low_half = _slot_masks()

    def attend(nkeys):
        res = []
        for cols, use_b, var in _ATT_GROUPS:
            qs = _stacked_queries(q_ref, cols, use_b, slot_a)
            k = kv_ref[0, 0:nkeys, var * LANES:(var + 1) * LANES]
            v = kv_ref[0, 0:nkeys, (2 + var) * LANES:(3 + var) * LANES]
            s = lax.dot_general(qs, k, _NT, preferred_element_type=F32)
            p = jnp.exp(s - jnp.max(s, axis=-1, keepdims=True))
            l = jnp.sum(p, axis=-1, keepdims=True)
            res.append(jnp.dot(p.astype(BF16), v, preferred_element_type=F32) * (1.0 / l))
        _store_heads(o_ref, res, low_half)

    if blk_off == 0:
        blk = pl.program_id(1)

        @pl.when(blk < CTX_BLKS)
        def _():
            attend(CTX_LEN)

        @pl.when(blk >= CTX_BLKS)
        def _():
            attend(TOK)
    else:
        attend(TOK)


def _ax_call(q, kv, need_ctx):
    off = 0 if need_ctx else CTX_BLKS
    nblk = NBLK - off
    return pl.pallas_call(
        functools.partial(_ax_kernel, blk_off=off),
        grid=(BATCH, nblk),
        in_specs=[pl.BlockSpec((1, BLOCK, 512), lambda b, i: (b, i + off, 0)),
                  pl.BlockSpec((1, TOK, 512), lambda b, i: (b, 0, 0))],
        out_specs=pl.BlockSpec((1, BLOCK, 512), lambda b, i: (b, i, 0)),
        out_shape=jax.ShapeDtypeStruct((BATCH, nblk * BLOCK, 512), BF16),
        compiler_params=_params("parallel", "arbitrary"),
        name="axial_attn",
    )(q, kv)


def _win_kernel(sink_ref, q_ref, kv_ref, o_ref, *, blk_off):
    slot_a, low_half = _slot_masks()
    row = lax.broadcasted_iota(jnp.int32, (2 * BLOCK, 1), 0)

    def sink_col(cols, use_b):
        h0, h1 = (2 * c + (1 if use_b else 0) for c in cols)
        return jnp.where(row < BLOCK, sink_ref[h0], sink_ref[h1])

    def attend(load_keys, valid):
        res = []
        for cols, use_b, var in _ATT_GROUPS:
            qs = _stacked_queries(q_ref, cols, use_b, slot_a)
            k = load_keys(var * LANES)
            v = load_keys((2 + var) * LANES)
            s = lax.dot_general(qs, k, _NT, preferred_element_type=F32)
            if valid is not None:
                s = jnp.where(valid, s, NEG)
            sk = sink_col(cols, use_b)
            mx = jnp.maximum(jnp.max(s, axis=-1, keepdims=True), sk)
            p = jnp.exp(s - mx)
            l = jnp.sum(p, axis=-1, keepdims=True) + jnp.exp(sk - mx)
            res.append(jnp.dot(p.astype(BF16), v, preferred_element_type=F32) * (1.0 / l))
        _store_heads(o_ref, res, low_half)

    def ctx_block():
        attend(lambda c0: kv_ref[0, 0:CTX_LEN, c0:c0 + LANES], None)

    def latent_block(n):
        left = jnp.maximum(n - 1, 0)
        right = jnp.minimum(n + 1, LAT_BLKS - 1)
        starts = [pl.multiple_of(CTX_LEN + b * BLOCK, BLOCK) for b in (left, n, right)]

        def load_keys(c0):
            parts = [kv_ref[0, 0:CTX_LEN, c0:c0 + LANES]]
            parts += [kv_ref[0, pl.ds(s0, BLOCK), c0:c0 + LANES] for s0 in starts]
            return jnp.concatenate(parts, axis=0)

        nk = CTX_LEN + 3 * BLOCK
        col = lax.broadcasted_iota(jnp.int32, (2 * BLOCK, nk), 1)
        qi = lax.broadcasted_iota(jnp.int32, (2 * BLOCK, nk), 0) & (BLOCK - 1)
        jj = col - CTX_LEN
        dlt = jj - qi
        lo = jnp.where(n >= 1, 0, BLOCK)
        hi = jnp.where(n <= LAT_BLKS - 2, 3 * BLOCK, 2 * BLOCK)
        in_win = (dlt >= 0) & (dlt <= 2 * BLOCK) & (jj >= lo) & (jj < hi)
        attend(load_keys, (col < CTX_LEN) | in_win)

    blk = pl.program_id(1) + blk_off
    if blk_off == 0:
        pl.when(blk < CTX_BLKS)(ctx_block)

        @pl.when(blk >= CTX_BLKS)
        def _():
            latent_block(blk - CTX_BLKS)
    else:
        latent_block(blk - CTX_BLKS)


def _win_call(sink, q, kv, need_ctx):
    off = 0 if need_ctx else CTX_BLKS
    nblk = NBLK - off
    return pl.pallas_call(
        functools.partial(_win_kernel, blk_off=off),
        grid=(BATCH, nblk),
        in_specs=[pl.BlockSpec(memory_space=pltpu.SMEM),
                  pl.BlockSpec((1, BLOCK, 512), lambda b, i: (b, i + off, 0)),
                  pl.BlockSpec((1, TOK, 512), lambda b, i: (b, 0, 0))],
        out_specs=pl.BlockSpec((1, BLOCK, 512), lambda b, i: (b, i, 0)),
        out_shape=jax.ShapeDtypeStruct((BATCH, nblk * BLOCK, 512), BF16),
        compiler_params=_params("parallel", "arbitrary"),
        name="window_attn",
    )(sink, q, kv)


def _merge_kernel(x_ref, mod_ref, nw_ref, or_ref, ow_ref, oa_ref, wg_ref, wpr_ref, wpw_ref, wpa_ref,
                  wo_ref, fw_ref, out_ref, *, final):
    x = x_ref[0]
    h = _modulated_norm(x, mod_ref, nw_ref)

    def branch(o_ref, wp_ref, j):
        g = jnp.dot(h, wg_ref[:, j * 512:(j + 1) * 512], preferred_element_type=F32)
        u = (o_ref[0].astype(F32) * (g * _sigmoid(g))).astype(BF16)
        b = jnp.dot(u, wp_ref[...], preferred_element_type=F32)
        m0 = 1536 + j * D_MODEL
        m = _sigmoid(jnp.dot(h, wg_ref[:, m0:m0 + D_MODEL], preferred_element_type=F32))
        return m * b

    mix = branch(or_ref, wpr_ref, 0) + branch(ow_ref, wpw_ref, 1) + branch(oa_ref, wpa_ref, 2)
    y = jnp.dot(mix.astype(BF16), wo_ref[...], preferred_element_type=F32)
    xn = x + mod_ref[0, :, 2 * D_MODEL:3 * D_MODEL] * y
    if final:
        xn = xn * lax.rsqrt(jnp.mean(xn * xn, axis=-1, keepdims=True) + EPS) * fw_ref[...]
    out_ref[0] = xn


def _merge_call(xx, mod_l, nw, o_ret, o_win, o_ax, wg, wpr, wpw, wpa, wo, fw, final):
    x_off = CTX_LEN // TM if final else 0
    n_tok = SEQ if final else TOK
    o_spec = pl.BlockSpec((1, TM, 512), lambda b, t: (b, t, 0))
    mod_spec = pl.BlockSpec((1, 1, 3 * D_MODEL), lambda b, t: (b, 0, 0)) if final else _mod_spec()
    return pl.pallas_call(
        functools.partial(_merge_kernel, final=final),
        grid=(BATCH, n_tok // TM),
        in_specs=[pl.BlockSpec((1, TM, D_MODEL), lambda b, t: (b, t + x_off, 0)),
                  mod_spec,
                  _const_spec((1, D_MODEL)),
                  o_spec, o_spec, o_spec,
                  _const_spec((D_MODEL, NGATE)),
                  _const_spec((512, D_MODEL)), _const_spec((512, D_MODEL)), _const_spec((512, D_MODEL)),
                  _const_spec((D_MODEL, D_MODEL)),
                  _const_spec((1, D_MODEL))],
        out_specs=pl.BlockSpec((1, TM, D_MODEL), lambda b, t: (b, t, 0)),
        out_shape=jax.ShapeDtypeStruct((BATCH, n_tok, D_MODEL), F32),
        compiler_params=_params("parallel", "arbitrary"),
        name="merge_final" if final else "merge",
    )(xx, mod_l, nw, o_ret, o_win, o_ax, wg, wpr, wpw, wpa, wo, fw)


def _rope_tables():
    t = jnp.arange(TOK, dtype=F32)
    theta = ROPE_BASE ** (-jnp.linspace(0.0, 1.0, RET_DK // 2, dtype=F32))
    pos_b = jnp.where(t < CTX_LEN, CTX_LEN - 1.0 - t, 2.0 * CTX_LEN + SEQ - 1.0 - t)

    def ret_pair(pos):
        ang = pos[:, None] * theta[None]
        c, s = jnp.cos(ang), jnp.sin(ang)
        return jnp.concatenate([c, c], axis=-1), jnp.concatenate([-s, s], axis=-1)

    cf, sf = ret_pair(t)
    cb, sb = ret_pair(pos_b)

    s_idx = jnp.arange(SEQ)
    quarter = HEAD_DIM // 4
    freqs = ROPE_BASE ** (-jnp.arange(quarter, dtype=F32) / quarter)
    r = (s_idx // GRID_W).astype(F32)
    col = (s_idx % GRID_W).astype(F32)
    ang = jnp.concatenate([r[:, None] * freqs[None], col[:, None] * freqs[None]], axis=-1)
    c, s = jnp.cos(ang), jnp.sin(ang)
    ca = jnp.concatenate([jnp.ones((CTX_LEN, LANES), F32), jnp.concatenate([c, c, c, c], axis=-1)], axis=0)
    sa = jnp.concatenate([jnp.zeros((CTX_LEN, LANES), F32), jnp.concatenate([-s, -s, s, s], axis=-1)], axis=0)
    return cf, sf, cb, sb, ca, sa


def _segment_matrix():
    slot = (np.arange(LANES) // 32) % 2
    return jnp.asarray((slot[:, None] == slot[None, :]).astype(np.float32), dtype=BF16)


def kernel(x, c, ctx, c_ctx, norm_w, w_mod, b_mod, w_in, ret_decay_fwd, ret_decay_bwd, win_sink,
           ax_q_gain, ax_k_gain, w_proj_ret, w_proj_win, w_proj_ax, w_out, final_norm_w):
    xx = jnp.concatenate([ctx, x], axis=1)
    cvec = jnp.zeros((24, D_MODEL), F32).at[:BATCH].set(c).at[BATCH].set(c_ctx)
    mod = _mod_call(cvec, w_mod, b_mod)
    tabs = _rope_tables()
    seg = _segment_matrix()
    wqkv = jnp.take(w_in, jnp.asarray(_QKV_COLS), axis=2).astype(BF16)
    wgate = jnp.take(w_in, jnp.asarray(_GATE_COLS), axis=2).astype(BF16)
    gain_lanes = jnp.asarray(_GAIN_LANES)
    fw = final_norm_w.reshape(1, D_MODEL)

    out = None
    for l in range(DEPTH):
        need_ctx = l < DEPTH - 1
        mod_l = mod[l].reshape(24, 1, 3 * D_MODEL)
        nw = norm_w[l].reshape(1, D_MODEL)
        gq = ax_q_gain[l][gain_lanes].reshape(1, LANES)
        gk = ax_k_gain[l][gain_lanes].reshape(1, LANES)
        qf, qb, kf, kb, rv, wq, wkv, aq, akv = _qkv_call(xx, mod_l, nw, wqkv[l], tabs, gq, gk, seg)
        a_f = jnp.broadcast_to(ret_decay_fwd[l][:, None, None], (RET_HEADS, 8, LANES))
        a_b = jnp.broadcast_to(ret_decay_bwd[l][:, None, None], (RET_HEADS, 8, LANES))
        o_ret = _ret_call(a_f, a_b, qf, qb, kf, kb, rv, need_ctx)
        o_win = _win_call(win_sink[l], wq, wkv, need_ctx)
        o_ax = _ax_call(aq, akv, need_ctx)
        res = _merge_call(xx, mod_l, nw, o_ret, o_win, o_ax, wgate[l],
                          w_proj_ret[l].astype(BF16), w_proj_win[l].astype(BF16), w_proj_ax[l].astype(BF16),
                          w_out[l].astype(BF16), fw, final=not need_ctx)
        if need_ctx:
            xx = res
        else:
            out = res
    return out
```

```python
import functools

import numpy as np
import jax
import jax.numpy as jnp
from jax import lax
from jax.experimental import pallas as pl
from jax.experimental.pallas import tpu as pltpu

D_MODEL = 1024
BATCH = 16
SEQ = 2048
DEPTH = 4
CTX_LEN = 256
TOK = CTX_LEN + SEQ
GRID_W = 64
BLOCK = 128
RET_HEADS = 4
RET_DK = 128
HEAD_DIM = 64
ROPE_BASE = 10000.0
EPS = 1e-6
NEG = -1e30

LANES = 128
NBLK = TOK // BLOCK
CTX_BLKS = CTX_LEN // BLOCK
LAT_BLKS = SEQ // BLOCK
LOG2E = 1.4426950408889634
KV_W = 6 * LANES
QB = CTX_LEN
NQB = TOK // QB
PV_KC = 256

TM = 256
VMEM_LIMIT = 52 * 1024 * 1024

F32 = jnp.float32
BF16 = jnp.bfloat16

_IN_SIZES = (512, 512, 512, 512, 512, 128, 128, 512, 512, 128, 128, 512, 3 * D_MODEL)
_OFF = np.concatenate([[0], np.cumsum(_IN_SIZES)]).astype(np.int64)
(_RQ, _RK, _RV, _RG, _WQ, _WK, _WV, _WG, _AQ, _AK, _AV, _AG, _MG) = [int(o) for o in _OFF[:-1]]

_EV64 = np.arange(0, 128, 2)
_OD64 = np.arange(1, 128, 2)
_EV32 = np.arange(0, 64, 2)
_OD32 = np.arange(1, 64, 2)


def _ret_cols(base):
    return np.concatenate([base + h * 128 + np.concatenate([_EV64, _OD64]) for h in range(RET_HEADS)])


def _attq_cols(base):
    out = []
    for j in range(4):
        ha, hb = base + (2 * j) * 64, base + (2 * j + 1) * 64
        out.append(np.concatenate([ha + _EV32, hb + _EV32, ha + _OD32, hb + _OD32]))
    return np.concatenate(out)


def _attk_cols(base, first, second):
    ga, gb = base + first * 64, base + second * 64
    return np.concatenate([ga + _EV32, gb + _EV32, ga + _OD32, gb + _OD32])


def _attv_cols(base, first, second):
    return np.concatenate([base + first * 64 + np.arange(64), base + second * 64 + np.arange(64)])


_QKV_COLS = np.concatenate([
    _ret_cols(_RQ), _ret_cols(_RK), _RV + np.arange(512),
    _attq_cols(_WQ), _attk_cols(_WK, 0, 1), _attk_cols(_WK, 1, 0), _attv_cols(_WV, 0, 1), _attv_cols(_WV, 1, 0),
    _attq_cols(_AQ), _attk_cols(_AK, 0, 1), _attk_cols(_AK, 1, 0), _attv_cols(_AV, 0, 1), _attv_cols(_AV, 1, 0),
]).astype(np.int32)
_GATE_COLS = np.concatenate([_RG + np.arange(512), _WG + np.arange(512), _AG + np.arange(512),
                             _MG + np.arange(3 * D_MODEL)]).astype(np.int32)
NQKV = int(_QKV_COLS.shape[0])
NGATE = int(_GATE_COLS.shape[0])
_GAIN_LANES = np.concatenate([_EV32, _EV32, _OD32, _OD32]).astype(np.int32)

_ATT_HEADS = tuple((c, b, (c // 2 + b) % 2, 2 + 2 * (c // 2) + b) for c in range(4) for b in (0, 1))

_NT = (((1,), (1,)), ((), ()))
_TN = (((0,), (0,)), ((), ()))


def _const_spec(shape):
    nd = len(shape)
    return pl.BlockSpec(shape, lambda *_: (0,) * nd, pipeline_mode=pl.Buffered(1))


def _params(*sem):
    return pltpu.CompilerParams(dimension_semantics=sem, vmem_limit_bytes=VMEM_LIMIT)


def _sigmoid(z):
    return 1.0 / (1.0 + jnp.exp(-z))


def _mod_kernel(c_ref, w_ref, b_ref, o_ref):
    cv = c_ref[...]
    s = cv * _sigmoid(cv)
    o_ref[0] = jnp.dot(s, w_ref[0], preferred_element_type=F32,
                       precision=lax.Precision.HIGHEST) + b_ref[0]


def _mod_call(cvec, w_mod, b_mod):
    rows = cvec.shape[0]
    return pl.pallas_call(
        _mod_kernel,
        grid=(DEPTH, 3),
        in_specs=[pl.BlockSpec((rows, D_MODEL), lambda l, j: (0, 0)),
                  pl.BlockSpec((1, D_MODEL, D_MODEL), lambda l, j: (l, 0, j)),
                  pl.BlockSpec((1, 1, D_MODEL), lambda l, j: (l, 0, j))],
        out_specs=pl.BlockSpec((1, rows, D_MODEL), lambda l, j: (l, 0, j)),
        out_shape=jax.ShapeDtypeStruct((DEPTH, rows, 3 * D_MODEL), F32),
        compiler_params=_params("arbitrary", "arbitrary"),
        name="adaln_mod",
    )(cvec, w_mod, b_mod.reshape(DEPTH, 1, 3 * D_MODEL))


def _modulated_norm(x, mod_ref, nw_ref):
    ms = jnp.mean(x * x, axis=-1, keepdims=True)
    y = x * lax.rsqrt(ms + EPS) * nw_ref[...]
    shift = mod_ref[0, :, 0:D_MODEL]
    scale = mod_ref[0, :, D_MODEL:2 * D_MODEL]
    return (y * (1.0 + scale) + shift).astype(BF16)


def _qkv_kernel(x_ref, mod_ref, nw_ref, w_ref, cf_ref, sf_ref, cb_ref, sb_ref, ca_ref, sa_ref,
                gq_ref, gk_ref, seg_ref,
                qf_ref, qb_ref, kf_ref, kb_ref, rv_ref, wq_ref, wkv_ref, aq_ref, akv_ref):
    h = _modulated_norm(x_ref[0], mod_ref, nw_ref)

    def proj(c0):
        return jnp.dot(h, w_ref[:, c0:c0 + 512], preferred_element_type=F32)

    def rope(v, c_ref, s_ref):
        return v * c_ref[...] + pltpu.roll(v, 64, 1) * s_ref[...]

    def lanes(a, j):
        return a[:, j * LANES:(j + 1) * LANES]

    def qk_norm(v, g_ref):
        ss = jnp.dot((v * v).astype(BF16), seg_ref[...], preferred_element_type=F32)
        return v * lax.rsqrt(ss * (1.0 / HEAD_DIM) + EPS) * g_ref[...]

    k_scale = RET_DK ** -0.5
    q_scale = HEAD_DIM ** -0.5 * LOG2E
    low_half = lax.broadcasted_iota(jnp.int32, (TM, LANES), 1) < 64

    def store_kv(kv_ref, acc):
        va, vb = lanes(acc, 2), lanes(acc, 3)
        kv_ref[0, :, 2 * LANES:3 * LANES] = jnp.where(low_half, va, 1.0).astype(BF16)
        kv_ref[0, :, 3 * LANES:4 * LANES] = jnp.where(low_half, 1.0, vb).astype(BF16)
        kv_ref[0, :, 4 * LANES:5 * LANES] = jnp.where(low_half, vb, 1.0).astype(BF16)
        kv_ref[0, :, 5 * LANES:6 * LANES] = jnp.where(low_half, 1.0, va).astype(BF16)

    acc = proj(0)
    for j in range(4):
        a = lanes(acc, j)
        qf_ref[0, :, j * LANES:(j + 1) * LANES] = rope(a, cf_ref, sf_ref).astype(BF16)
        qb_ref[0, :, j * LANES:(j + 1) * LANES] = rope(a, cb_ref, sb_ref).astype(BF16)
    acc = proj(512)
    for j in range(4):
        a = lanes(acc, j)
        kf_ref[0, :, j * LANES:(j + 1) * LANES] = (rope(a, cf_ref, sf_ref) * k_scale).astype(BF16)
        kb_ref[0, :, j * LANES:(j + 1) * LANES] = (rope(a, cb_ref, sb_ref) * k_scale).astype(BF16)
    rv_ref[0] = proj(1024).astype(BF16)

    acc = proj(1536)
    for j in range(4):
        wq_ref[0, :, j * LANES:(j + 1) * LANES] = (rope(lanes(acc, j), ca_ref, sa_ref) * q_scale).astype(BF16)
    acc = proj(2048)
    for j in range(2):
        wkv_ref[0, :, j * LANES:(j + 1) * LANES] = rope(lanes(acc, j), ca_ref, sa_ref).astype(BF16)
    store_kv(wkv_ref, acc)

    acc = proj(2560)
    for j in range(4):
        a = qk_norm(lanes(acc, j), gq_ref)
        aq_ref[0, :, j * LANES:(j + 1) * LANES] = (rope(a, ca_ref, sa_ref) * q_scale).astype(BF16)
    acc = proj(3072)
    for j in range(2):
        a = qk_norm(lanes(acc, j), gk_ref)
        akv_ref[0, :, j * LANES:(j + 1) * LANES] = rope(a, ca_ref, sa_ref).astype(BF16)
    store_kv(akv_ref, acc)


def _mod_spec():
    return pl.BlockSpec((1, 1, 3 * D_MODEL), lambda b, t: (jnp.where(t == 0, BATCH, b), 0, 0))


def _qkv_call(xx, mod_l, nw, wqkv, tabs, gq, gk, seg):
    tok_spec = pl.BlockSpec((1, TM, 512), lambda b, t: (b, t, 0))
    tab_spec = pl.BlockSpec((TM, LANES), lambda b, t: (t, 0))
    out_sds = jax.ShapeDtypeStruct((BATCH, TOK, 512), BF16)
    kv_spec = pl.BlockSpec((1, TM, KV_W), lambda b, t: (b, t, 0))
    kv_sds = jax.ShapeDtypeStruct((BATCH, TOK, KV_W), BF16)
    return pl.pallas_call(
        _qkv_kernel,
        grid=(BATCH, TOK // TM),
        in_specs=[pl.BlockSpec((1, TM, D_MODEL), lambda b, t: (b, t, 0)),
                  _mod_spec(),
                  _const_spec((1, D_MODEL)),
                  _const_spec((D_MODEL, NQKV)),
                  tab_spec, tab_spec, tab_spec, tab_spec, tab_spec, tab_spec,
                  _const_spec((1, LANES)), _const_spec((1, LANES)), _const_spec((LANES, LANES))],
        out_specs=[tok_spec] * 6 + [kv_spec, tok_spec, kv_spec],
        out_shape=[out_sds] * 6 + [kv_sds, out_sds, kv_sds],
        compiler_params=_params("parallel", "arbitrary"),
        name="qkv_proj",
    )(xx, mod_l, nw, wqkv, *tabs, gq, gk, seg)


def _log_sigmoid(a):
    return jnp.minimum(a, 0.0) - jnp.log1p(jnp.exp(-jnp.abs(a)))


def _ret_kernel(af_ref, ab_ref, qf_ref, qb_ref, kf_ref, kb_ref, v_ref, o_ref, acc_ref, *, need_ctx):
    lgf = _log_sigmoid(af_ref[0])[0:1, :]
    lgb = _log_sigmoid(ab_ref[0])[0:1, :]
    ri = lax.broadcasted_iota(jnp.int32, (BLOCK, BLOCK), 0).astype(F32)
    ci = lax.broadcasted_iota(jnp.int32, (BLOCK, BLOCK), 1).astype(F32)
    d = ri - ci
    dec_f = (jnp.where(d >= 0, jnp.exp(jnp.maximum(d, 0.0) * lgf), 0.0),
             jnp.exp((ri + 1.0) * lgf), jnp.exp((BLOCK - 1.0 - ri) * lgf), jnp.exp(float(BLOCK) * lgf))
    dec_b = (jnp.where(d < 0, jnp.exp(jnp.maximum(-d, 0.0) * lgb), 0.0),
             jnp.exp((float(BLOCK) - ri) * lgb), jnp.exp(ri * lgb), jnp.exp(float(BLOCK) * lgb))

    def step(q_ref, k_ref, c, state, dec, want_o):
        intra, q_dec, k_dec, c_dec = dec
        rows = slice(c * BLOCK, (c + 1) * BLOCK)
        q = q_ref[0, rows, :]
        k = k_ref[0, rows, :]
        v = v_ref[0, rows, :]
        o = None
        if want_o:
            s = lax.dot_general(q, k, _NT, preferred_element_type=F32)
            o = jnp.dot((s * intra).astype(BF16), v, preferred_element_type=F32)
            o = o + jnp.dot(q, state.astype(BF16), preferred_element_type=F32) * q_dec
        kd = (k.astype(F32) * k_dec).astype(BF16)
        kv = lax.dot_general(kd, v, _TN, preferred_element_type=F32)
        return o, c_dec * state + kv

    def emit(c, o, seen):
        if c < CTX_BLKS and not need_ctx:
            return
        rows = slice(c * BLOCK, (c + 1) * BLOCK)
        if c not in seen:
            acc_ref[rows, :] = o
            seen.add(c)
            return
        tot = acc_ref[rows, :] + o
        tot = tot * lax.rsqrt(jnp.mean(tot * tot, axis=-1, keepdims=True) + EPS)
        first = 0 if need_ctx else CTX_BLKS
        o_ref[0, (c - first) * BLOCK:(c - first + 1) * BLOCK, :] = tot.astype(o_ref.dtype)

    bwd_order = list(range(CTX_BLKS - 1, -1, -1)) + list(range(NBLK - 1, CTX_BLKS - 1, -1))
    st_f = jnp.zeros((RET_DK, BLOCK), F32)
    st_b = jnp.zeros((RET_DK, BLOCK), F32)
    seen = set()
    for i in range(NBLK):
        cf, cb = i, bwd_order[i]
        of, st_f = step(qf_ref, kf_ref, cf, st_f, dec_f, need_ctx or cf >= CTX_BLKS)
        ob, st_b = step(qb_ref, kb_ref, cb, st_b, dec_b, need_ctx or cb >= CTX_BLKS)
        emit(cf, of, seen)
        emit(cb, ob, seen)


def _ret_call(a_f, a_b, qf, qb, kf, kb, rv, need_ctx):
    n_out = TOK if need_ctx else SEQ
    head_spec = pl.BlockSpec((1, TOK, LANES), lambda b, h: (b, 0, h))
    a_spec = pl.BlockSpec((1, 8, LANES), lambda b, h: (h, 0, 0))
    return pl.pallas_call(
        functools.partial(_ret_kernel, need_ctx=need_ctx),
        grid=(BATCH, RET_HEADS),
        in_specs=[a_spec, a_spec, head_spec, head_spec, head_spec, head_spec, head_spec],
        out_specs=pl.BlockSpec((1, n_out, LANES), lambda b, h: (b, 0, h)),
        out_shape=jax.ShapeDtypeStruct((BATCH, n_out, 512), BF16),
        scratch_shapes=[pltpu.VMEM((TOK, LANES), F32)],
        compiler_params=_params("parallel", "arbitrary"),
        name="retention",
    )(a_f, a_b, qf, qb, kf, kb, rv)


def _slot_masks():
    lane = lax.broadcasted_iota(jnp.int32, (QB, LANES), 1)
    slot_a = (lane & 32) == 0
    return slot_a, lane < 64


def _head_queries(q_ref, c, use_b, slot_a):
    qv = q_ref[0, :, c * LANES:(c + 1) * LANES].astype(F32)
    qv = jnp.where(slot_a, 0.0, qv) if use_b else jnp.where(slot_a, qv, 0.0)
    return qv.astype(BF16)


def _pv(p, load_v, nkeys):
    acc = None
    for r0 in range(0, nkeys, PV_KC):
        d = jnp.dot(p[:, r0:r0 + PV_KC], load_v(r0, r0 + PV_KC), preferred_element_type=F32)
        acc = d if acc is None else acc + d
    return acc


def _store_heads(o_ref, res, low_half, extra=None):
    for c in range(4):
        ra, rb = res[2 * c], res[2 * c + 1]
        num = jnp.where(low_half, ra, rb)
        den = pltpu.roll(jnp.where(low_half, rb, ra), 64, 1)
        if extra is not None:
            den = den + jnp.where(low_half, extra[2 * c], extra[2 * c + 1])
        o_ref[0, :, c * LANES:(c + 1) * LANES] = (num * (1.0 / den)).astype(o_ref.dtype)


def _attn_specs(off):
    return dict(
        in_specs=[pl.BlockSpec((1, QB, 512), lambda b, i: (b, i + off, 0)),
                  pl.BlockSpec((1, TOK, KV_W), lambda b, i: (b, 0, 0))],
        out_specs=pl.BlockSpec((1, QB, 512), lambda b, i: (b, i, 0)),
        out_shape=jax.ShapeDtypeStruct((BATCH, (NQB - off) * QB, 512), BF16),
    )


def _ax_kernel(q_ref, kv_ref, o_ref, *, blk_off):
    slot_a, low_half = _slot_masks()

    def attend(nkeys):
        res = []
        for c, use_b, kb, vb in _ATT_HEADS:
            qs = _head_queries(q_ref, c, use_b, slot_a)
            k = kv_ref[0, 0:nkeys, kb * LANES:(kb + 1) * LANES]
            s = lax.dot_general(qs, k, _NT, preferred_element_type=F32)
            p = jnp.exp2(s - jnp.max(s, axis=-1, keepdims=True)).astype(BF16)
            v0 = (vb - use_b) * LANES
            wide = _pv(p, lambda r0, r1: kv_ref[0, r0:r1, v0:v0 + 2 * LANES], nkeys)
            res.append(wide[:, use_b * LANES:(use_b + 1) * LANES])
        _store_heads(o_ref, res, low_half)

    if blk_off == 0:
        blk = pl.program_id(1)

        @pl.when(blk == 0)
        def _():
            attend(CTX_LEN)

        @pl.when(blk > 0)
        def _():
            attend(TOK)
    else:
        attend(TOK)


def _ax_call(q, kv, need_ctx):
    off = 0 if need_ctx else 1
    return pl.pallas_call(
        functools.partial(_ax_kernel, blk_off=off),
        grid=(BATCH, NQB - off),
        compiler_params=_params("parallel", "arbitrary"),
        name="axial_attn",
        **_attn_specs(off),
    )(q, kv)


def _win_kernel(sink_ref, q_ref, kv_ref, o_ref, *, blk_off):
    slot_a, low_half = _slot_masks()

    def attend(load_keys, valid):
        res, sink_terms, keys, vals = [], [], {}, {}
        for c, use_b, kb, vb in _ATT_HEADS:
            qs = _head_queries(q_ref, c, use_b, slot_a)
            if kb not in keys:
                keys[kb] = load_keys(kb * LANES, LANES)
            if vb - use_b not in vals:
                vals[vb - use_b] = load_keys((vb - use_b) * LANES, 2 * LANES)
            v = vals[vb - use_b]
            s = lax.dot_general(qs, keys[kb], _NT, preferred_element_type=F32)
            if valid is not None:
                s = jnp.where(valid, s, NEG)
            sk = sink_ref[2 * c + use_b] * LOG2E
            mx = jnp.maximum(jnp.max(s, axis=-1, keepdims=True), sk)
            p = jnp.exp2(s - mx).astype(BF16)
            sink_terms.append(jnp.exp2(sk - mx))
            wide = _pv(p, lambda r0, r1: v[r0:r1], v.shape[0])
            res.append(wide[:, use_b * LANES:(use_b + 1) * LANES])
        _store_heads(o_ref, res, low_half, sink_terms)

    def ctx_block():
        attend(lambda c0, w: kv_ref[0, 0:CTX_LEN, c0:c0 + w], None)

    def latent_block(n):
        left = jnp.maximum(2 * n - 1, 0)
        right = jnp.minimum(2 * n + 2, LAT_BLKS - 1)
        s_left = pl.multiple_of(CTX_LEN + left * BLOCK, BLOCK)
        s_mid = pl.multiple_of(CTX_LEN + n * QB, QB)
        s_right = pl.multiple_of(CTX_LEN + right * BLOCK, BLOCK)

        def load_keys(c0, w):
            return jnp.concatenate([kv_ref[0, 0:CTX_LEN, c0:c0 + w],
                                    kv_ref[0, pl.ds(s_left, BLOCK), c0:c0 + w],
                                    kv_ref[0, pl.ds(s_mid, QB), c0:c0 + w],
                                    kv_ref[0, pl.ds(s_right, BLOCK), c0:c0 + w]], axis=0)

        nk = CTX_LEN + QB + 2 * BLOCK
        col = lax.broadcasted_iota(jnp.int32, (QB, nk), 1)
        qi = lax.broadcasted_iota(jnp.int32, (QB, nk), 0)
        jj = col - CTX_LEN
        dlt = jj - qi
        lo = jnp.where(n >= 1, 0, BLOCK)
        hi = jnp.where(n <= SEQ // QB - 2, QB + 2 * BLOCK, QB + BLOCK)
        in_win = (dlt >= 0) & (dlt <= 2 * BLOCK) & (jj >= lo) & (jj < hi)
        attend(load_keys, (col < CTX_LEN) | in_win)

    blk = pl.program_id(1) + blk_off
    if blk_off == 0:
        pl.when(blk == 0)(ctx_block)

        @pl.when(blk > 0)
        def _():
            latent_block(blk - 1)
    else:
        latent_block(blk - 1)


def _win_call(sink, q, kv, need_ctx):
    off = 0 if need_ctx else 1
    specs = _attn_specs(off)
    specs["in_specs"] = [pl.BlockSpec(memory_space=pltpu.SMEM)] + specs["in_specs"]
    return pl.pallas_call(
        functools.partial(_win_kernel, blk_off=off),
        grid=(BATCH, NQB - off),
        compiler_params=_params("parallel", "arbitrary"),
        name="window_attn",
        **specs,
    )(sink, q, kv)


def _merge_kernel(x_ref, mod_ref, nw_ref, or_ref, ow_ref, oa_ref, wg_ref, wpr_ref, wpw_ref, wpa_ref,
                  wo_ref, fw_ref, out_ref, *, final):
    x = x_ref[0]
    h = _modulated_norm(x, mod_ref, nw_ref)

    def branch(o_ref, wp_ref, j):
        g = jnp.dot(h, wg_ref[:, j * 512:(j + 1) * 512], preferred_element_type=F32)
        u = (o_ref[0].astype(F32) * (g * _sigmoid(g))).astype(BF16)
        b = jnp.dot(u, wp_ref[...], preferred_element_type=F32)
        m0 = 1536 + j * D_MODEL
        m = _sigmoid(jnp.dot(h, wg_ref[:, m0:m0 + D_MODEL], preferred_element_type=F32))
        return m * b

    mix = branch(or_ref, wpr_ref, 0) + branch(ow_ref, wpw_ref, 1) + branch(oa_ref, wpa_ref, 2)
    y = jnp.dot(mix.astype(BF16), wo_ref[...], preferred_element_type=F32)
    xn = x + mod_ref[0, :, 2 * D_MODEL:3 * D_MODEL] * y
    if final:
        xn = xn * lax.rsqrt(jnp.mean(xn * xn, axis=-1, keepdims=True) + EPS) * fw_ref[...]
    out_ref[0] = xn


def _merge_call(xx, mod_l, nw, o_ret, o_win, o_ax, wg, wpr, wpw, wpa, wo, fw, final):
    x_off = CTX_LEN // TM if final else 0
    n_tok = SEQ if final else TOK
    o_spec = pl.BlockSpec((1, TM, 512), lambda b, t: (b, t, 0))
    mod_spec = pl.BlockSpec((1, 1, 3 * D_MODEL), lambda b, t: (b, 0, 0)) if final else _mod_spec()
    return pl.pallas_call(
        functools.partial(_merge_kernel, final=final),
        grid=(BATCH, n_tok // TM),
        in_specs=[pl.BlockSpec((1, TM, D_MODEL), lambda b, t: (b, t + x_off, 0)),
                  mod_spec,
                  _const_spec((1, D_MODEL)),
                  o_spec, o_spec, o_spec,
                  _const_spec((D_MODEL, NGATE)),
                  _const_spec((512, D_MODEL)), _const_spec((512, D_MODEL)), _const_spec((512, D_MODEL)),
                  _const_spec((D_MODEL, D_MODEL)),
                  _const_spec((1, D_MODEL))],
        out_specs=pl.BlockSpec((1, TM, D_MODEL), lambda b, t: (b, t, 0)),
        out_shape=jax.ShapeDtypeStruct((BATCH, n_tok, D_MODEL), F32),
        compiler_params=_params("parallel", "arbitrary"),
        name="merge_final" if final else "merge",
    )(xx, mod_l, nw, o_ret, o_win, o_ax, wg, wpr, wpw, wpa, wo, fw)


def _rope_tables():
    t = jnp.arange(TOK, dtype=F32)
    theta = ROPE_BASE ** (-jnp.linspace(0.0, 1.0, RET_DK // 2, dtype=F32))
    pos_b = jnp.where(t < CTX_LEN, CTX_LEN - 1.0 - t, 2.0 * CTX_LEN + SEQ - 1.0 - t)

    def ret_pair(pos):
        ang = pos[:, None] * theta[None]
        c, s = jnp.cos(ang), jnp.sin(ang)
        return jnp.concatenate([c, c], axis=-1), jnp.concatenate([-s, s], axis=-1)

    cf, sf = ret_pair(t)
    cb, sb = ret_pair(pos_b)

    s_idx = jnp.arange(SEQ)
    quarter = HEAD_DIM // 4
    freqs = ROPE_BASE ** (-jnp.arange(quarter, dtype=F32) / quarter)
    r = (s_idx // GRID_W).astype(F32)
    col = (s_idx % GRID_W).astype(F32)
    ang = jnp.concatenate([r[:, None] * freqs[None], col[:, None] * freqs[None]], axis=-1)
    c, s = jnp.cos(ang), jnp.sin(ang)
    ca = jnp.concatenate([jnp.ones((CTX_LEN, LANES), F32), jnp.concatenate([c, c, c, c], axis=-1)], axis=0)
    sa = jnp.concatenate([jnp.zeros((CTX_LEN, LANES), F32), jnp.concatenate([-s, -s, s, s], axis=-1)], axis=0)
    return cf, sf, cb, sb, ca, sa


def _segment_matrix():
    slot = (np.arange(LANES) // 32) % 2
    return jnp.asarray((slot[:, None] == slot[None, :]).astype(np.float32), dtype=BF16)


def kernel(x, c, ctx, c_ctx, norm_w, w_mod, b_mod, w_in, ret_decay_fwd, ret_decay_bwd, win_sink,
           ax_q_gain, ax_k_gain, w_proj_ret, w_proj_win, w_proj_ax, w_out, final_norm_w):
    xx = jnp.concatenate([ctx, x], axis=1)
    cvec = jnp.zeros((24, D_MODEL), F32).at[:BATCH].set(c).at[BATCH].set(c_ctx)
    mod = _mod_call(cvec, w_mod, b_mod)
    tabs = _rope_tables()
    seg = _segment_matrix()
    wqkv = jnp.take(w_in, jnp.asarray(_QKV_COLS), axis=2).astype(BF16)
    wgate = jnp.take(w_in, jnp.asarray(_GATE_COLS), axis=2).astype(BF16)
    gain_lanes = jnp.asarray(_GAIN_LANES)
    fw = final_norm_w.reshape(1, D_MODEL)

    out = None
    for l in range(DEPTH):
        need_ctx = l < DEPTH - 1
        mod_l = mod[l].reshape(24, 1, 3 * D_MODEL)
        nw = norm_w[l].reshape(1, D_MODEL)
        gq = ax_q_gain[l][gain_lanes].reshape(1, LANES)
        gk = ax_k_gain[l][gain_lanes].reshape(1, LANES)
        qf, qb, kf, kb, rv, wq, wkv, aq, akv = _qkv_call(xx, mod_l, nw, wqkv[l], tabs, gq, gk, seg)
        a_f = jnp.broadcast_to(ret_decay_fwd[l][:, None, None], (RET_HEADS, 8, LANES))
        a_b = jnp.broadcast_to(ret_decay_bwd[l][:, None, None], (RET_HEADS, 8, LANES))
        o_ret = _ret_call(a_f, a_b, qf, qb, kf, kb, rv, need_ctx)
        o_win = _win_call(win_sink[l], wq, wkv, need_ctx)
        o_ax = _ax_call(aq, akv, need_ctx)
        res = _merge_call(xx, mod_l, nw, o_ret, o_win, o_ax, wgate[l],
                          w_proj_ret[l].astype(BF16), w_proj_win[l].astype(BF16), w_proj_ax[l].astype(BF16),
                          w_out[l].astype(BF16), fw, final=not need_ctx)
        if need_ctx:
            xx = res
        else:
            out = res
    return out
```

```python
import functools

import numpy as np
import jax
import jax.numpy as jnp
from jax import lax
from jax.experimental import pallas as pl
from jax.experimental.pallas import tpu as pltpu

D_MODEL = 1024
BATCH = 16
SEQ = 2048
DEPTH = 4
CTX_LEN = 256
TOK = CTX_LEN + SEQ
GRID_W = 64
BLOCK = 128
RET_HEADS = 4
RET_DK = 128
HEAD_DIM = 64
ROPE_BASE = 10000.0
EPS = 1e-6
NEG = -1e30

LANES = 128
NBLK = TOK // BLOCK
CTX_BLKS = CTX_LEN // BLOCK
LAT_BLKS = SEQ // BLOCK
LOG2E = 1.4426950408889634
KV_W = 6 * LANES
QB = CTX_LEN
NQB = TOK // QB
PV_KC = 256

TM = 256
VMEM_LIMIT = 52 * 1024 * 1024

F32 = jnp.float32
BF16 = jnp.bfloat16

_IN_SIZES = (512, 512, 512, 512, 512, 128, 128, 512, 512, 128, 128, 512, 3 * D_MODEL)
_OFF = np.concatenate([[0], np.cumsum(_IN_SIZES)]).astype(np.int64)
(_RQ, _RK, _RV, _RG, _WQ, _WK, _WV, _WG, _AQ, _AK, _AV, _AG, _MG) = [int(o) for o in _OFF[:-1]]

_EV64 = np.arange(0, 128, 2)
_OD64 = np.arange(1, 128, 2)
_EV32 = np.arange(0, 64, 2)
_OD32 = np.arange(1, 64, 2)


def _ret_cols(base):
    return np.concatenate([base + h * 128 + np.concatenate([_EV64, _OD64]) for h in range(RET_HEADS)])


def _attq_cols(base):
    out = []
    for j in range(4):
        ha, hb = base + (2 * j) * 64, base + (2 * j + 1) * 64
        out.append(np.concatenate([ha + _EV32, hb + _EV32, ha + _OD32, hb + _OD32]))
    return np.concatenate(out)


def _attk_cols(base, first, second):
    ga, gb = base + first * 64, base + second * 64
    return np.concatenate([ga + _EV32, gb + _EV32, ga + _OD32, gb + _OD32])


def _attv_cols(base, first, second):
    return np.concatenate([base + first * 64 + np.arange(64), base + second * 64 + np.arange(64)])


_QKV_COLS = np.concatenate([
    _ret_cols(_RQ), _ret_cols(_RK), _RV + np.arange(512),
    _attq_cols(_WQ), _attk_cols(_WK, 0, 1), _attk_cols(_WK, 1, 0), _attv_cols(_WV, 0, 1), _attv_cols(_WV, 1, 0),
    _attq_cols(_AQ), _attk_cols(_AK, 0, 1), _attk_cols(_AK, 1, 0), _attv_cols(_AV, 0, 1), _attv_cols(_AV, 1, 0),
]).astype(np.int32)
_GATE_COLS = np.concatenate([_RG + np.arange(512), _WG + np.arange(512), _AG + np.arange(512),
                             _MG + np.arange(3 * D_MODEL)]).astype(np.int32)
NQKV = int(_QKV_COLS.shape[0])
NGATE = int(_GATE_COLS.shape[0])
_GAIN_LANES = np.concatenate([_EV32, _EV32, _OD32, _OD32]).astype(np.int32)

_ATT_HEADS = tuple((c, b, (c // 2 + b) % 2, 2 + 2 * (c // 2) + b) for c in range(4) for b in (0, 1))

_NT = (((1,), (1,)), ((), ()))
_TN = (((0,), (0,)), ((), ()))


def _const_spec(shape):
    nd = len(shape)
    return pl.BlockSpec(shape, lambda *_: (0,) * nd, pipeline_mode=pl.Buffered(1))


def _params(*sem):
    return pltpu.CompilerParams(dimension_semantics=sem, vmem_limit_bytes=VMEM_LIMIT)


def _sigmoid(z):
    return 1.0 / (1.0 + jnp.exp(-z))


def _mod_kernel(c_ref, w_ref, b_ref, o_ref):
    cv = c_ref[...]
    s = cv * _sigmoid(cv)
    o_ref[0] = jnp.dot(s, w_ref[0], preferred_element_type=F32,
                       precision=lax.Precision.HIGHEST) + b_ref[0]


def _mod_call(cvec, w_mod, b_mod):
    rows = cvec.shape[0]
    return pl.pallas_call(
        _mod_kernel,
        grid=(DEPTH, 3),
        in_specs=[pl.BlockSpec((rows, D_MODEL), lambda l, j: (0, 0)),
                  pl.BlockSpec((1, D_MODEL, D_MODEL), lambda l, j: (l, 0, j)),
                  pl.BlockSpec((1, 1, D_MODEL), lambda l, j: (l, 0, j))],
        out_specs=pl.BlockSpec((1, rows, D_MODEL), lambda l, j: (l, 0, j)),
        out_shape=jax.ShapeDtypeStruct((DEPTH, rows, 3 * D_MODEL), F32),
        compiler_params=_params("arbitrary", "arbitrary"),
        name="adaln_mod",
    )(cvec, w_mod, b_mod.reshape(DEPTH, 1, 3 * D_MODEL))


def _modulated_norm(x, mod_ref, nw_ref):
    ms = jnp.mean(x * x, axis=-1, keepdims=True)
    y = x * lax.rsqrt(ms + EPS) * nw_ref[...]
    shift = mod_ref[0, :, 0:D_MODEL]
    scale = mod_ref[0, :, D_MODEL:2 * D_MODEL]
    return (y * (1.0 + scale) + shift).astype(BF16)


def _qkv_kernel(x_ref, mod_ref, nw_ref, w_ref, cf_ref, sf_ref, cb_ref, sb_ref, ca_ref, sa_ref,
                gq_ref, gk_ref, seg_ref,
                qf_ref, qb_ref, kf_ref, kb_ref, rv_ref, wq_ref, wkv_ref, aq_ref, akv_ref):
    h = _modulated_norm(x_ref[0], mod_ref, nw_ref)

    def proj(c0):
        return jnp.dot(h, w_ref[:, c0:c0 + 512], preferred_element_type=F32)

    def rope(v, c_ref, s_ref):
        return v * c_ref[...] + pltpu.roll(v, 64, 1) * s_ref[...]

    def lanes(a, j):
        return a[:, j * LANES:(j + 1) * LANES]

    def qk_norm(v, g_ref):
        ss = jnp.dot((v * v).astype(BF16), seg_ref[...], preferred_element_type=F32)
        return v * lax.rsqrt(ss * (1.0 / HEAD_DIM) + EPS) * g_ref[...]

    k_scale = RET_DK ** -0.5
    q_scale = HEAD_DIM ** -0.5 * LOG2E
    low_half = lax.broadcasted_iota(jnp.int32, (TM, LANES), 1) < 64

    def store_kv(kv_ref, acc):
        va, vb = lanes(acc, 2), lanes(acc, 3)
        kv_ref[0, :, 2 * LANES:3 * LANES] = jnp.where(low_half, va, 1.0).astype(BF16)
        kv_ref[0, :, 3 * LANES:4 * LANES] = jnp.where(low_half, 1.0, vb).astype(BF16)
        kv_ref[0, :, 4 * LANES:5 * LANES] = jnp.where(low_half, vb, 1.0).astype(BF16)
        kv_ref[0, :, 5 * LANES:6 * LANES] = jnp.where(low_half, 1.0, va).astype(BF16)

    acc = proj(0)
    for j in range(4):
        a = lanes(acc, j)
        qf_ref[0, :, j * LANES:(j + 1) * LANES] = rope(a, cf_ref, sf_ref).astype(BF16)
        qb_ref[0, :, j * LANES:(j + 1) * LANES] = rope(a, cb_ref, sb_ref).astype(BF16)
    acc = proj(512)
    for j in range(4):
        a = lanes(acc, j)
        kf_ref[0, :, j * LANES:(j + 1) * LANES] = (rope(a, cf_ref, sf_ref) * k_scale).astype(BF16)
        kb_ref[0, :, j * LANES:(j + 1) * LANES] = (rope(a, cb_ref, sb_ref) * k_scale).astype(BF16)
    rv_ref[0] = proj(1024).astype(BF16)

    acc = proj(1536)
    for j in range(4):
        wq_ref[0, :, j * LANES:(j + 1) * LANES] = (rope(lanes(acc, j), ca_ref, sa_ref) * q_scale).astype(BF16)
    acc = proj(2048)
    for j in range(2):
        wkv_ref[0, :, j * LANES:(j + 1) * LANES] = rope(lanes(acc, j), ca_ref, sa_ref).astype(BF16)
    store_kv(wkv_ref, acc)

    acc = proj(2560)
    for j in range(4):
        a = qk_norm(lanes(acc, j), gq_ref)
        aq_ref[0, :, j * LANES:(j + 1) * LANES] = (rope(a, ca_ref, sa_ref) * q_scale).astype(BF16)
    acc = proj(3072)
    for j in range(2):
        a = qk_norm(lanes(acc, j), gk_ref)
        akv_ref[0, :, j * LANES:(j + 1) * LANES] = rope(a, ca_ref, sa_ref).astype(BF16)
    store_kv(akv_ref, acc)


def _mod_spec():
    return pl.BlockSpec((1, 1, 3 * D_MODEL), lambda b, t: (jnp.where(t == 0, BATCH, b), 0, 0))


def _qkv_call(xx, mod_l, nw, wqkv, tabs, gq, gk, seg):
    tok_spec = pl.BlockSpec((1, TM, 512), lambda b, t: (b, t, 0))
    tab_spec = pl.BlockSpec((TM, LANES), lambda b, t: (t, 0))
    out_sds = jax.ShapeDtypeStruct((BATCH, TOK, 512), BF16)
    kv_spec = pl.BlockSpec((1, TM, KV_W), lambda b, t: (b, t, 0))
    kv_sds = jax.ShapeDtypeStruct((BATCH, TOK, KV_W), BF16)
    return pl.pallas_call(
        _qkv_kernel,
        grid=(BATCH, TOK // TM),
        in_specs=[pl.BlockSpec((1, TM, D_MODEL), lambda b, t: (b, t, 0)),
                  _mod_spec(),
                  _const_spec((1, D_MODEL)),
                  _const_spec((D_MODEL, NQKV)),
                  tab_spec, tab_spec, tab_spec, tab_spec, tab_spec, tab_spec,
                  _const_spec((1, LANES)), _const_spec((1, LANES)), _const_spec((LANES, LANES))],
        out_specs=[tok_spec] * 6 + [kv_spec, tok_spec, kv_spec],
        out_shape=[out_sds] * 6 + [kv_sds, out_sds, kv_sds],
        compiler_params=_params("parallel", "arbitrary"),
        name="qkv_proj",
    )(xx, mod_l, nw, wqkv, *tabs, gq, gk, seg)


def _log_sigmoid(a):
    return jnp.minimum(a, 0.0) - jnp.log1p(jnp.exp(-jnp.abs(a)))


def _ret_kernel(af_ref, ab_ref, qf_ref, qb_ref, kf_ref, kb_ref, v_ref, o_ref, acc_ref, st_ref, dec_ref,
                *, need_ctx):
    ri = lax.broadcasted_iota(jnp.int32, (BLOCK, BLOCK), 0).astype(F32)
    ci = lax.broadcasted_iota(jnp.int32, (BLOCK, BLOCK), 1).astype(F32)
    d = ri - ci
    for h in range(RET_HEADS):
        lgf = _log_sigmoid(af_ref[h])[0:1, :]
        lgb = _log_sigmoid(ab_ref[h])[0:1, :]
        dec_ref[2 * h, 0] = jnp.where(d >= 0, jnp.exp(jnp.maximum(d, 0.0) * lgf), 0.0)
        dec_ref[2 * h, 1] = jnp.exp((ri + 1.0) * lgf)
        dec_ref[2 * h, 2] = jnp.exp((BLOCK - 1.0 - ri) * lgf)
        dec_ref[2 * h, 3] = jnp.exp(0.0 * ri + float(BLOCK) * lgf)
        dec_ref[2 * h + 1, 0] = jnp.where(d < 0, jnp.exp(jnp.maximum(-d, 0.0) * lgb), 0.0)
        dec_ref[2 * h + 1, 1] = jnp.exp((float(BLOCK) - ri) * lgb)
        dec_ref[2 * h + 1, 2] = jnp.exp(ri * lgb)
        dec_ref[2 * h + 1, 3] = jnp.exp(0.0 * ri + float(BLOCK) * lgb)
    st_ref[...] = jnp.zeros_like(st_ref)
    first_row = 0 if need_ctx else CTX_LEN

    def visit(i, in_ctx, first, want_o):
        cf = i
        cb = (CTX_BLKS - 1 - i) if in_ctx else (NBLK + CTX_BLKS - 1 - i)
        pending = []
        for h in range(RET_HEADS):
            lanes_h = slice(h * LANES, (h + 1) * LANES)
            for dr, (q_ref, k_ref, c) in enumerate(((qf_ref, kf_ref, cf), (qb_ref, kb_ref, cb))):
                r0 = c * BLOCK if in_ctx else pl.multiple_of(c * BLOCK, BLOCK)
                j = 2 * h + dr
                k = k_ref[0, pl.ds(r0, BLOCK), lanes_h]
                v = v_ref[0, pl.ds(r0, BLOCK), lanes_h]
                state = st_ref[j]
                kd = (k.astype(F32) * dec_ref[j, 2]).astype(BF16)
                st_ref[j] = dec_ref[j, 3] * state + lax.dot_general(kd, v, _TN, preferred_element_type=F32)
                if want_o:
                    q = q_ref[0, pl.ds(r0, BLOCK), lanes_h]
                    s = lax.dot_general(q, k, _NT, preferred_element_type=F32)
                    cross = jnp.dot(q, state.astype(BF16), preferred_element_type=F32)
                    pending.append((j, r0, lanes_h, v, s, cross))
        for j, r0, lanes_h, v, s, cross in pending:
            o = jnp.dot((s * dec_ref[j, 0]).astype(BF16), v, preferred_element_type=F32) + cross * dec_ref[j, 1]
            if first:
                acc_ref[pl.ds(r0, BLOCK), lanes_h] = o
            else:
                tot = acc_ref[pl.ds(r0, BLOCK), lanes_h] + o
                tot = tot * lax.rsqrt(jnp.mean(tot * tot, axis=-1, keepdims=True) + EPS)
                o_ref[0, pl.ds(r0 - first_row, BLOCK), lanes_h] = tot.astype(o_ref.dtype)

    for i in range(CTX_BLKS):
        visit(i, True, i < CTX_BLKS // 2, need_ctx)
    half = (NBLK + CTX_BLKS) // 2

    def first_pass(i, carry):
        visit(i, False, True, True)
        return carry

    def second_pass(i, carry):
        visit(i, False, False, True)
        return carry

    lax.fori_loop(CTX_BLKS, half, first_pass, 0)
    lax.fori_loop(half, NBLK, second_pass, 0)


def _ret_call(a_f, a_b, qf, qb, kf, kb, rv, need_ctx):
    n_out = TOK if need_ctx else SEQ
    tok_spec = pl.BlockSpec((1, TOK, 512), lambda b: (b, 0, 0))
    a_spec = _const_spec((RET_HEADS, 8, LANES))
    return pl.pallas_call(
        functools.partial(_ret_kernel, need_ctx=need_ctx),
        grid=(BATCH,),
        in_specs=[a_spec, a_spec, tok_spec, tok_spec, tok_spec, tok_spec, tok_spec],
        out_specs=pl.BlockSpec((1, n_out, 512), lambda b: (b, 0, 0)),
        out_shape=jax.ShapeDtypeStruct((BATCH, n_out, 512), BF16),
        scratch_shapes=[pltpu.VMEM((TOK, 512), F32),
                        pltpu.VMEM((2 * RET_HEADS, RET_DK, LANES), F32),
                        pltpu.VMEM((2 * RET_HEADS, 4, BLOCK, BLOCK), F32)],
        compiler_params=_params("parallel"),
        name="retention",
    )(a_f, a_b, qf, qb, kf, kb, rv)


def _slot_masks():
    lane = lax.broadcasted_iota(jnp.int32, (QB, LANES), 1)
    slot_a = (lane & 32) == 0
    return slot_a, lane < 64


def _head_queries(q_ref, c, use_b, slot_a):
    qv = q_ref[0, :, c * LANES:(c + 1) * LANES].astype(F32)
    qv = jnp.where(slot_a, 0.0, qv) if use_b else jnp.where(slot_a, qv, 0.0)
    return qv.astype(BF16)


def _pv(p, load_v, nkeys):
    acc = None
    for r0 in range(0, nkeys, PV_KC):
        d = jnp.dot(p[:, r0:r0 + PV_KC], load_v(r0, r0 + PV_KC), preferred_element_type=F32)
        acc = d if acc is None else acc + d
    return acc


def _store_heads(o_ref, res, low_half, extra=None):
    for c in range(4):
        ra, rb = res[2 * c], res[2 * c + 1]
        num = jnp.where(low_half, ra, rb)
        den = pltpu.roll(jnp.where(low_half, rb, ra), 64, 1)
        if extra is not None:
            den = den + jnp.where(low_half, extra[2 * c], extra[2 * c + 1])
        o_ref[0, :, c * LANES:(c + 1) * LANES] = (num * (1.0 / den)).astype(o_ref.dtype)


def _attn_specs(off):
    return dict(
        in_specs=[pl.BlockSpec((1, QB, 512), lambda b, i: (b, i + off, 0)),
                  pl.BlockSpec((1, TOK, KV_W), lambda b, i: (b, 0, 0))],
        out_specs=pl.BlockSpec((1, QB, 512), lambda b, i: (b, i, 0)),
        out_shape=jax.ShapeDtypeStruct((BATCH, (NQB - off) * QB, 512), BF16),
    )


def _ax_kernel(q_ref, kv_ref, o_ref, *, blk_off):
    slot_a, low_half = _slot_masks()

    def attend(nkeys):
        res = []
        for c, use_b, kb, vb in _ATT_HEADS:
            qs = _head_queries(q_ref, c, use_b, slot_a)
            k = kv_ref[0, 0:nkeys, kb * LANES:(kb + 1) * LANES]
            s = lax.dot_general(qs, k, _NT, preferred_element_type=F32)
            p = jnp.exp2(s - jnp.max(s, axis=-1, keepdims=True)).astype(BF16)
            v0 = (vb - use_b) * LANES
            wide = _pv(p, lambda r0, r1: kv_ref[0, r0:r1, v0:v0 + 2 * LANES], nkeys)
            res.append(wide[:, use_b * LANES:(use_b + 1) * LANES])
        _store_heads(o_ref, res, low_half)

    if blk_off == 0:
        blk = pl.program_id(1)

        @pl.when(blk == 0)
        def _():
            attend(CTX_LEN)

        @pl.when(blk > 0)
        def _():
            attend(TOK)
    else:
        attend(TOK)


def _ax_call(q, kv, need_ctx):
    off = 0 if need_ctx else 1
    return pl.pallas_call(
        functools.partial(_ax_kernel, blk_off=off),
        grid=(BATCH, NQB - off),
        compiler_params=_params("parallel", "arbitrary"),
        name="axial_attn",
        **_attn_specs(off),
    )(q, kv)


def _win_kernel(sink_ref, q_ref, kv_ref, o_ref, *, blk_off):
    slot_a, low_half = _slot_masks()

    def attend(load_keys, valid):
        res, sink_terms, keys, vals = [], [], {}, {}
        for c, use_b, kb, vb in _ATT_HEADS:
            qs = _head_queries(q_ref, c, use_b, slot_a)
            if kb not in keys:
                keys[kb] = load_keys(kb * LANES, LANES)
            if vb - use_b not in vals:
                vals[vb - use_b] = load_keys((vb - use_b) * LANES, 2 * LANES)
            v = vals[vb - use_b]
            s = lax.dot_general(qs, keys[kb], _NT, preferred_element_type=F32)
            if valid is not None:
                s = jnp.where(valid, s, NEG)
            sk = sink_ref[2 * c + use_b] * LOG2E
            mx = jnp.maximum(jnp.max(s, axis=-1, keepdims=True), sk)
            p = jnp.exp2(s - mx).astype(BF16)
            sink_terms.append(jnp.exp2(sk - mx))
            wide = _pv(p, lambda r0, r1: v[r0:r1], v.shape[0])
            res.append(wide[:, use_b * LANES:(use_b + 1) * LANES])
        _store_heads(o_ref, res, low_half, sink_terms)

    def ctx_block():
        attend(lambda c0, w: kv_ref[0, 0:CTX_LEN, c0:c0 + w], None)

    def latent_block(n):
        left = jnp.maximum(2 * n - 1, 0)
        right = jnp.minimum(2 * n + 2, LAT_BLKS - 1)
        s_left = pl.multiple_of(CTX_LEN + left * BLOCK, BLOCK)
        s_mid = pl.multiple_of(CTX_LEN + n * QB, QB)
        s_right = pl.multiple_of(CTX_LEN + right * BLOCK, BLOCK)

        def load_keys(c0, w):
            return jnp.concatenate([kv_ref[0, 0:CTX_LEN, c0:c0 + w],
                                    kv_ref[0, pl.ds(s_left, BLOCK), c0:c0 + w],
                                    kv_ref[0, pl.ds(s_mid, QB), c0:c0 + w],
                                    kv_ref[0, pl.ds(s_right, BLOCK), c0:c0 + w]], axis=0)

        nk = CTX_LEN + QB + 2 * BLOCK
        col = lax.broadcasted_iota(jnp.int32, (QB, nk), 1)
        qi = lax.broadcasted_iota(jnp.int32, (QB, nk), 0)
        jj = col - CTX_LEN
        dlt = jj - qi
        lo = jnp.where(n >= 1, 0, BLOCK)
        hi = jnp.where(n <= SEQ // QB - 2, QB + 2 * BLOCK, QB + BLOCK)
        in_win = (dlt >= 0) & (dlt <= 2 * BLOCK) & (jj >= lo) & (jj < hi)
        attend(load_keys, (col < CTX_LEN) | in_win)

    blk = pl.program_id(1) + blk_off
    if blk_off == 0:
        pl.when(blk == 0)(ctx_block)

        @pl.when(blk > 0)
        def _():
            latent_block(blk - 1)
    else:
        latent_block(blk - 1)


def _win_call(sink, q, kv, need_ctx):
    off = 0 if need_ctx else 1
    specs = _attn_specs(off)
    specs["in_specs"] = [pl.BlockSpec(memory_space=pltpu.SMEM)] + specs["in_specs"]
    return pl.pallas_call(
        functools.partial(_win_kernel, blk_off=off),
        grid=(BATCH, NQB - off),
        compiler_params=_params("parallel", "arbitrary"),
        name="window_attn",
        **specs,
    )(sink, q, kv)


def _merge_kernel(x_ref, mod_ref, nw_ref, or_ref, ow_ref, oa_ref, wg_ref, wpr_ref, wpw_ref, wpa_ref,
                  wo_ref, fw_ref, out_ref, *, final):
    x = x_ref[0]
    h = _modulated_norm(x, mod_ref, nw_ref)

    def branch(o_ref, wp_ref, j):
        g = jnp.dot(h, wg_ref[:, j * 512:(j + 1) * 512], preferred_element_type=F32)
        u = (o_ref[0].astype(F32) * (g * _sigmoid(g))).astype(BF16)
        b = jnp.dot(u, wp_ref[...], preferred_element_type=F32)
        m0 = 1536 + j * D_MODEL
        m = _sigmoid(jnp.dot(h, wg_ref[:, m0:m0 + D_MODEL], preferred_element_type=F32))
        return m * b

    mix = branch(or_ref, wpr_ref, 0) + branch(ow_ref, wpw_ref, 1) + branch(oa_ref, wpa_ref, 2)
    y = jnp.dot(mix.astype(BF16), wo_ref[...], preferred_element_type=F32)
    xn = x + mod_ref[0, :, 2 * D_MODEL:3 * D_MODEL] * y
    if final:
        xn = xn * lax.rsqrt(jnp.mean(xn * xn, axis=-1, keepdims=True) + EPS) * fw_ref[...]
    out_ref[0] = xn


def _merge_call(xx, mod_l, nw, o_ret, o_win, o_ax, wg, wpr, wpw, wpa, wo, fw, final):
    x_off = CTX_LEN // TM if final else 0
    n_tok = SEQ if final else TOK
    o_spec = pl.BlockSpec((1, TM, 512), lambda b, t: (b, t, 0))
    mod_spec = pl.BlockSpec((1, 1, 3 * D_MODEL), lambda b, t: (b, 0, 0)) if final else _mod_spec()
    return pl.pallas_call(
        functools.partial(_merge_kernel, final=final),
        grid=(BATCH, n_tok // TM),
        in_specs=[pl.BlockSpec((1, TM, D_MODEL), lambda b, t: (b, t + x_off, 0)),
                  mod_spec,
                  _const_spec((1, D_MODEL)),
                  o_spec, o_spec, o_spec,
                  _const_spec((D_MODEL, NGATE)),
                  _const_spec((512, D_MODEL)), _const_spec((512, D_MODEL)), _const_spec((512, D_MODEL)),
                  _const_spec((D_MODEL, D_MODEL)),
                  _const_spec((1, D_MODEL))],
        out_specs=pl.BlockSpec((1, TM, D_MODEL), lambda b, t: (b, t, 0)),
        out_shape=jax.ShapeDtypeStruct((BATCH, n_tok, D_MODEL), F32),
        compiler_params=_params("parallel", "arbitrary"),
        name="merge_final" if final else "merge",
    )(xx, mod_l, nw, o_ret, o_win, o_ax, wg, wpr, wpw, wpa, wo, fw)


def _rope_tables():
    t = jnp.arange(TOK, dtype=F32)
    theta = ROPE_BASE ** (-jnp.linspace(0.0, 1.0, RET_DK // 2, dtype=F32))
    pos_b = jnp.where(t < CTX_LEN, CTX_LEN - 1.0 - t, 2.0 * CTX_LEN + SEQ - 1.0 - t)

    def ret_pair(pos):
        ang = pos[:, None] * theta[None]
        c, s = jnp.cos(ang), jnp.sin(ang)
        return jnp.concatenate([c, c], axis=-1), jnp.concatenate([-s, s], axis=-1)

    cf, sf = ret_pair(t)
    cb, sb = ret_pair(pos_b)

    s_idx = jnp.arange(SEQ)
    quarter = HEAD_DIM // 4
    freqs = ROPE_BASE ** (-jnp.arange(quarter, dtype=F32) / quarter)
    r = (s_idx // GRID_W).astype(F32)
    col = (s_idx % GRID_W).astype(F32)
    ang = jnp.concatenate([r[:, None] * freqs[None], col[:, None] * freqs[None]], axis=-1)
    c, s = jnp.cos(ang), jnp.sin(ang)
    ca = jnp.concatenate([jnp.ones((CTX_LEN, LANES), F32), jnp.concatenate([c, c, c, c], axis=-1)], axis=0)
    sa = jnp.concatenate([jnp.zeros((CTX_LEN, LANES), F32), jnp.concatenate([-s, -s, s, s], axis=-1)], axis=0)
    return cf, sf, cb, sb, ca, sa


def _segment_matrix():
    slot = (np.arange(LANES) // 32) % 2
    return jnp.asarray((slot[:, None] == slot[None, :]).astype(np.float32), dtype=BF16)


def kernel(x, c, ctx, c_ctx, norm_w, w_mod, b_mod, w_in, ret_decay_fwd, ret_decay_bwd, win_sink,
           ax_q_gain, ax_k_gain, w_proj_ret, w_proj_win, w_proj_ax, w_out, final_norm_w):
    xx = jnp.concatenate([ctx, x], axis=1)
    cvec = jnp.zeros((24, D_MODEL), F32).at[:BATCH].set(c).at[BATCH].set(c_ctx)
    mod = _mod_call(cvec, w_mod, b_mod)
    tabs = _rope_tables()
    seg = _segment_matrix()
    wqkv = jnp.take(w_in, jnp.asarray(_QKV_COLS), axis=2).astype(BF16)
    wgate = jnp.take(w_in, jnp.asarray(_GATE_COLS), axis=2).astype(BF16)
    gain_lanes = jnp.asarray(_GAIN_LANES)
    fw = final_norm_w.reshape(1, D_MODEL)

    out = None
    for l in range(DEPTH):
        need_ctx = l < DEPTH - 1
        mod_l = mod[l].reshape(24, 1, 3 * D_MODEL)
        nw = norm_w[l].reshape(1, D_MODEL)
        gq = ax_q_gain[l][gain_lanes].reshape(1, LANES)
        gk = ax_k_gain[l][gain_lanes].reshape(1, LANES)
        qf, qb, kf, kb, rv, wq, wkv, aq, akv = _qkv_call(xx, mod_l, nw, wqkv[l], tabs, gq, gk, seg)
        a_f = jnp.broadcast_to(ret_decay_fwd[l][:, None, None], (RET_HEADS, 8, LANES))
        a_b = jnp.broadcast_to(ret_decay_bwd[l][:, None, None], (RET_HEADS, 8, LANES))
        o_ret = _ret_call(a_f, a_b, qf, qb, kf, kb, rv, need_ctx)
        o_win = _win_call(win_sink[l], wq, wkv, need_ctx)
        o_ax = _ax_call(aq, akv, need_ctx)
        res = _merge_call(xx, mod_l, nw, o_ret, o_win, o_ax, wgate[l],
                          w_proj_ret[l].astype(BF16), w_proj_win[l].astype(BF16), w_proj_ax[l].astype(BF16),
                          w_out[l].astype(BF16), fw, final=not need_ctx)
        if need_ctx:
            xx = res
        else:
            out = res
    return out
```

```python
import functools

import numpy as np
import jax
import jax.numpy as jnp
from jax import lax
from jax.experimental import pallas as pl
from jax.experimental.pallas import tpu as pltpu

D_MODEL = 1024
BATCH = 16
SEQ = 2048
DEPTH = 4
CTX_LEN = 256
TOK = CTX_LEN + SEQ
GRID_W = 64
BLOCK = 128
RET_HEADS = 4
RET_DK = 128
HEAD_DIM = 64
ROPE_BASE = 10000.0
EPS = 1e-6
NEG = -1e30

LANES = 128
NBLK = TOK // BLOCK
CTX_BLKS = CTX_LEN // BLOCK
LAT_BLKS = SEQ // BLOCK
LOG2E = 1.4426950408889634
KV_W = 6 * LANES
QB = CTX_LEN
NQB = TOK // QB
PV_KC = 256

SUB = CTX_LEN
TM = 3 * SUB
VMEM_LIMIT = 52 * 1024 * 1024

F32 = jnp.float32
BF16 = jnp.bfloat16

_IN_SIZES = (512, 512, 512, 512, 512, 128, 128, 512, 512, 128, 128, 512, 3 * D_MODEL)
_OFF = np.concatenate([[0], np.cumsum(_IN_SIZES)]).astype(np.int64)
(_RQ, _RK, _RV, _RG, _WQ, _WK, _WV, _WG, _AQ, _AK, _AV, _AG, _MG) = [int(o) for o in _OFF[:-1]]

_EV64 = np.arange(0, 128, 2)
_OD64 = np.arange(1, 128, 2)
_EV32 = np.arange(0, 64, 2)
_OD32 = np.arange(1, 64, 2)


def _ret_cols(base):
    return np.concatenate([base + h * 128 + np.concatenate([_EV64, _OD64]) for h in range(RET_HEADS)])


def _attq_cols(base):
    out = []
    for j in range(4):
        ha, hb = base + (2 * j) * 64, base + (2 * j + 1) * 64
        out.append(np.concatenate([ha + _EV32, hb + _EV32, ha + _OD32, hb + _OD32]))
    return np.concatenate(out)


def _attk_cols(base, first, second):
    ga, gb = base + first * 64, base + second * 64
    return np.concatenate([ga + _EV32, gb + _EV32, ga + _OD32, gb + _OD32])


def _attv_cols(base, first, second):
    return np.concatenate([base + first * 64 + np.arange(64), base + second * 64 + np.arange(64)])


_QKV_COLS = np.concatenate([
    _ret_cols(_RQ), _ret_cols(_RK), _RV + np.arange(512),
    _attq_cols(_WQ), _attk_cols(_WK, 0, 1), _attk_cols(_WK, 1, 0), _attv_cols(_WV, 0, 1), _attv_cols(_WV, 1, 0),
    _attq_cols(_AQ), _attk_cols(_AK, 0, 1), _attk_cols(_AK, 1, 0), _attv_cols(_AV, 0, 1), _attv_cols(_AV, 1, 0),
]).astype(np.int32)
_GATE_COLS = np.concatenate([_RG + np.arange(512), _WG + np.arange(512), _AG + np.arange(512),
                             _MG + np.arange(3 * D_MODEL)]).astype(np.int32)
NQKV = int(_QKV_COLS.shape[0])
NGATE = int(_GATE_COLS.shape[0])
_GAIN_LANES = np.concatenate([_EV32, _EV32, _OD32, _OD32]).astype(np.int32)

_ATT_HEADS = tuple((c, b, (c // 2 + b) % 2, 2 + 2 * (c // 2) + b) for c in range(4) for b in (0, 1))

_NT = (((1,), (1,)), ((), ()))
_TN = (((0,), (0,)), ((), ()))


def _const_spec(shape):
    nd = len(shape)
    return pl.BlockSpec(shape, lambda *_: (0,) * nd, pipeline_mode=pl.Buffered(1))


def _params(*sem):
    return pltpu.CompilerParams(dimension_semantics=sem, vmem_limit_bytes=VMEM_LIMIT)


def _sigmoid(z):
    return 1.0 / (1.0 + jnp.exp(-z))


def _mod_kernel(c_ref, w_ref, b_ref, o_ref):
    cv = c_ref[...]
    s = cv * _sigmoid(cv)
    o_ref[0] = jnp.dot(s, w_ref[0], preferred_element_type=F32,
                       precision=lax.Precision.HIGHEST) + b_ref[0]


def _mod_call(cvec, w_mod, b_mod):
    rows = cvec.shape[0]
    return pl.pallas_call(
        _mod_kernel,
        grid=(DEPTH, 3),
        in_specs=[pl.BlockSpec((rows, D_MODEL), lambda l, j: (0, 0)),
                  pl.BlockSpec((1, D_MODEL, D_MODEL), lambda l, j: (l, 0, j)),
                  pl.BlockSpec((1, 1, D_MODEL), lambda l, j: (l, 0, j))],
        out_specs=pl.BlockSpec((1, rows, D_MODEL), lambda l, j: (l, 0, j)),
        out_shape=jax.ShapeDtypeStruct((DEPTH, rows, 3 * D_MODEL), F32),
        compiler_params=_params("arbitrary", "arbitrary"),
        name="adaln_mod",
    )(cvec, w_mod, b_mod.reshape(DEPTH, 1, 3 * D_MODEL))


def _sub_mod(mod_ref, modc_ref, j):
    if modc_ref is None or j > 0:
        return mod_ref[0]
    return jnp.where(pl.program_id(1) == 0, modc_ref[0], mod_ref[0])


def _modulated_norm(x, mod, nw_ref):
    ms = jnp.mean(x * x, axis=-1, keepdims=True)
    y = x * lax.rsqrt(ms + EPS) * nw_ref[...]
    return (y * (1.0 + mod[:, D_MODEL:2 * D_MODEL]) + mod[:, 0:D_MODEL]).astype(BF16)


def _qkv_kernel(x_ref, mod_ref, modc_ref, nw_ref, w_ref, cf_ref, sf_ref, cb_ref, sb_ref, ca_ref, sa_ref,
                gq_ref, gk_ref, seg_ref,
                qf_ref, qb_ref, kf_ref, kb_ref, rv_ref, wq_ref, wkv_ref, aq_ref, akv_ref):
    k_scale = RET_DK ** -0.5
    q_scale = HEAD_DIM ** -0.5 * LOG2E
    low_half = lax.broadcasted_iota(jnp.int32, (SUB, LANES), 1) < 64

    def lanes(a, j):
        return a[:, j * LANES:(j + 1) * LANES]

    def qk_norm(v, g_ref):
        ss = jnp.dot((v * v).astype(BF16), seg_ref[...], preferred_element_type=F32)
        return v * lax.rsqrt(ss * (1.0 / HEAD_DIM) + EPS) * g_ref[...]

    for sb in range(x_ref.shape[1] // SUB):
        rows = slice(sb * SUB, (sb + 1) * SUB)
        h = _modulated_norm(x_ref[0, rows, :], _sub_mod(mod_ref, modc_ref, sb), nw_ref)

        def proj(c0):
            return jnp.dot(h, w_ref[:, c0:c0 + 512], preferred_element_type=F32)

        def rope(v, c_ref, s_ref):
            return v * c_ref[rows, :] + pltpu.roll(v, 64, 1) * s_ref[rows, :]

        def put(ref, j, val):
            ref[0, rows, j * LANES:(j + 1) * LANES] = val.astype(BF16)

        def store_kv(kv_ref, acc):
            va, vb = lanes(acc, 2), lanes(acc, 3)
            put(kv_ref, 2, jnp.where(low_half, va, 1.0))
            put(kv_ref, 3, jnp.where(low_half, 1.0, vb))
            put(kv_ref, 4, jnp.where(low_half, vb, 1.0))
            put(kv_ref, 5, jnp.where(low_half, 1.0, va))

        acc = proj(0)
        for j in range(4):
            put(qf_ref, j, rope(lanes(acc, j), cf_ref, sf_ref))
            put(qb_ref, j, rope(lanes(acc, j), cb_ref, sb_ref))
        acc = proj(512)
        for j in range(4):
            put(kf_ref, j, rope(lanes(acc, j), cf_ref, sf_ref) * k_scale)
            put(kb_ref, j, rope(lanes(acc, j), cb_ref, sb_ref) * k_scale)
        rv_ref[0, rows, :] = proj(1024).astype(BF16)

        acc = proj(1536)
        for j in range(4):
            put(wq_ref, j, rope(lanes(acc, j), ca_ref, sa_ref) * q_scale)
        acc = proj(2048)
        for j in range(2):
            put(wkv_ref, j, rope(lanes(acc, j), ca_ref, sa_ref))
        store_kv(wkv_ref, acc)

        acc = proj(2560)
        for j in range(4):
            put(aq_ref, j, rope(qk_norm(lanes(acc, j), gq_ref), ca_ref, sa_ref) * q_scale)
        acc = proj(3072)
        for j in range(2):
            put(akv_ref, j, rope(qk_norm(lanes(acc, j), gk_ref), ca_ref, sa_ref))
        store_kv(akv_ref, acc)


def _mod_specs():
    return [pl.BlockSpec((1, 1, 3 * D_MODEL), lambda b, t: (b, 0, 0)),
            pl.BlockSpec((1, 1, 3 * D_MODEL), lambda b, t: (BATCH, 0, 0))]


def _qkv_call(xx, mod_l, nw, wqkv, tabs, gq, gk, seg):
    tok_spec = pl.BlockSpec((1, TM, 512), lambda b, t: (b, t, 0))
    tab_spec = pl.BlockSpec((TM, LANES), lambda b, t: (t, 0))
    out_sds = jax.ShapeDtypeStruct((BATCH, TOK, 512), BF16)
    kv_spec = pl.BlockSpec((1, TM, KV_W), lambda b, t: (b, t, 0))
    kv_sds = jax.ShapeDtypeStruct((BATCH, TOK, KV_W), BF16)
    return pl.pallas_call(
        _qkv_kernel,
        grid=(BATCH, TOK // TM),
        in_specs=[pl.BlockSpec((1, TM, D_MODEL), lambda b, t: (b, t, 0)),
                  *_mod_specs(),
                  _const_spec((1, D_MODEL)),
                  _const_spec((D_MODEL, NQKV)),
                  tab_spec, tab_spec, tab_spec, tab_spec, tab_spec, tab_spec,
                  _const_spec((1, LANES)), _const_spec((1, LANES)), _const_spec((LANES, LANES))],
        out_specs=[tok_spec] * 6 + [kv_spec, tok_spec, kv_spec],
        out_shape=[out_sds] * 6 + [kv_sds, out_sds, kv_sds],
        compiler_params=_params("parallel", "arbitrary"),
        name="qkv_proj",
    )(xx, mod_l, mod_l, nw, wqkv, *tabs, gq, gk, seg)


def _log_sigmoid(a):
    return jnp.minimum(a, 0.0) - jnp.log1p(jnp.exp(-jnp.abs(a)))


def _ret_kernel(af_ref, ab_ref, qf_ref, qb_ref, kf_ref, kb_ref, v_ref, o_ref, acc_ref, st_ref, dec_ref,
                *, need_ctx):
    ri = lax.broadcasted_iota(jnp.int32, (BLOCK, BLOCK), 0).astype(F32)
    ci = lax.broadcasted_iota(jnp.int32, (BLOCK, BLOCK), 1).astype(F32)
    d = ri - ci
    for h in range(RET_HEADS):
        lgf = _log_sigmoid(af_ref[h])[0:1, :]
        lgb = _log_sigmoid(ab_ref[h])[0:1, :]
        dec_ref[2 * h, 0] = jnp.where(d >= 0, jnp.exp(jnp.maximum(d, 0.0) * lgf), 0.0)
        dec_ref[2 * h, 1] = jnp.exp((ri + 1.0) * lgf)
        dec_ref[2 * h, 2] = jnp.exp((BLOCK - 1.0 - ri) * lgf)
        dec_ref[2 * h, 3] = jnp.exp(0.0 * ri + float(BLOCK) * lgf)
        dec_ref[2 * h + 1, 0] = jnp.where(d < 0, jnp.exp(jnp.maximum(-d, 0.0) * lgb), 0.0)
        dec_ref[2 * h + 1, 1] = jnp.exp((float(BLOCK) - ri) * lgb)
        dec_ref[2 * h + 1, 2] = jnp.exp(ri * lgb)
        dec_ref[2 * h + 1, 3] = jnp.exp(0.0 * ri + float(BLOCK) * lgb)
    st_ref[...] = jnp.zeros_like(st_ref)
    first_row = 0 if need_ctx else CTX_LEN

    def visit(i, in_ctx, first, want_o):
        cf = i
        cb = (CTX_BLKS - 1 - i) if in_ctx else (NBLK + CTX_BLKS - 1 - i)
        pending = []
        for h in range(RET_HEADS):
            lanes_h = slice(h * LANES, (h + 1) * LANES)
            for dr, (q_ref, k_ref, c) in enumerate(((qf_ref, kf_ref, cf), (qb_ref, kb_ref, cb))):
                r0 = c * BLOCK if in_ctx else pl.multiple_of(c * BLOCK, BLOCK)
                j = 2 * h + dr
                k = k_ref[0, pl.ds(r0, BLOCK), lanes_h]
                v = v_ref[0, pl.ds(r0, BLOCK), lanes_h]
                state = st_ref[j]
                kd = (k.astype(F32) * dec_ref[j, 2]).astype(BF16)
                st_ref[j] = dec_ref[j, 3] * state + lax.dot_general(kd, v, _TN, preferred_element_type=F32)
                if want_o:
                    q = q_ref[0, pl.ds(r0, BLOCK), lanes_h]
                    s = lax.dot_general(q, k, _NT, preferred_element_type=F32)
                    cross = jnp.dot(q, state.astype(BF16), preferred_element_type=F32)
                    pending.append((j, r0, lanes_h, v, s, cross))
        for j, r0, lanes_h, v, s, cross in pending:
            o = jnp.dot((s * dec_ref[j, 0]).astype(BF16), v, preferred_element_type=F32) + cross * dec_ref[j, 1]
            if first:
                acc_ref[pl.ds(r0, BLOCK), lanes_h] = o
            else:
                tot = acc_ref[pl.ds(r0, BLOCK), lanes_h] + o
                tot = tot * lax.rsqrt(jnp.mean(tot * tot, axis=-1, keepdims=True) + EPS)
                o_ref[0, pl.ds(r0 - first_row, BLOCK), lanes_h] = tot.astype(o_ref.dtype)

    for i in range(CTX_BLKS):
        visit(i, True, i < CTX_BLKS // 2, need_ctx)
    half = (NBLK + CTX_BLKS) // 2

    def first_pass(i, carry):
        visit(i, False, True, True)
        return carry

    def second_pass(i, carry):
        visit(i, False, False, True)
        return carry

    lax.fori_loop(CTX_BLKS, half, first_pass, 0)
    lax.fori_loop(half, NBLK, second_pass, 0)


def _ret_call(a_f, a_b, qf, qb, kf, kb, rv, need_ctx):
    n_out = TOK if need_ctx else SEQ
    tok_spec = pl.BlockSpec((1, TOK, 512), lambda b: (b, 0, 0))
    a_spec = _const_spec((RET_HEADS, 8, LANES))
    return pl.pallas_call(
        functools.partial(_ret_kernel, need_ctx=need_ctx),
        grid=(BATCH,),
        in_specs=[a_spec, a_spec, tok_spec, tok_spec, tok_spec, tok_spec, tok_spec],
        out_specs=pl.BlockSpec((1, n_out, 512), lambda b: (b, 0, 0)),
        out_shape=jax.ShapeDtypeStruct((BATCH, n_out, 512), BF16),
        scratch_shapes=[pltpu.VMEM((TOK, 512), F32),
                        pltpu.VMEM((2 * RET_HEADS, RET_DK, LANES), F32),
                        pltpu.VMEM((2 * RET_HEADS, 4, BLOCK, BLOCK), F32)],
        compiler_params=_params("parallel"),
        name="retention",
    )(a_f, a_b, qf, qb, kf, kb, rv)


def _slot_masks():
    lane = lax.broadcasted_iota(jnp.int32, (QB, LANES), 1)
    slot_a = (lane & 32) == 0
    return slot_a, lane < 64


def _head_queries(q_ref, c, use_b, slot_a):
    qv = q_ref[0, :, c * LANES:(c + 1) * LANES].astype(F32)
    qv = jnp.where(slot_a, 0.0, qv) if use_b else jnp.where(slot_a, qv, 0.0)
    return qv.astype(BF16)


def _pv(p, load_v, nkeys):
    acc = None
    for r0 in range(0, nkeys, PV_KC):
        d = jnp.dot(p[:, r0:r0 + PV_KC], load_v(r0, r0 + PV_KC), preferred_element_type=F32)
        acc = d if acc is None else acc + d
    return acc


def _store_heads(o_ref, res, low_half, extra=None):
    for c in range(4):
        ra, rb = res[2 * c], res[2 * c + 1]
        num = jnp.where(low_half, ra, rb)
        den = pltpu.roll(jnp.where(low_half, rb, ra), 64, 1)
        if extra is not None:
            den = den + jnp.where(low_half, extra[2 * c], extra[2 * c + 1])
        o_ref[0, :, c * LANES:(c + 1) * LANES] = (num * (1.0 / den)).astype(o_ref.dtype)


def _attn_specs(off):
    return dict(
        in_specs=[pl.BlockSpec((1, QB, 512), lambda b, i: (b, i + off, 0)),
                  pl.BlockSpec((1, TOK, KV_W), lambda b, i: (b, 0, 0))],
        out_specs=pl.BlockSpec((1, QB, 512), lambda b, i: (b, i, 0)),
        out_shape=jax.ShapeDtypeStruct((BATCH, (NQB - off) * QB, 512), BF16),
    )


def _ax_kernel(q_ref, kv_ref, o_ref, *, blk_off):
    slot_a, low_half = _slot_masks()

    def attend(nkeys):
        res = []
        for c, use_b, kb, vb in _ATT_HEADS:
            qs = _head_queries(q_ref, c, use_b, slot_a)
            k = kv_ref[0, 0:nkeys, kb * LANES:(kb + 1) * LANES]
            s = lax.dot_general(qs, k, _NT, preferred_element_type=F32)
            p = jnp.exp2(s - jnp.max(s, axis=-1, keepdims=True)).astype(BF16)
            v0 = (vb - use_b) * LANES
            wide = _pv(p, lambda r0, r1: kv_ref[0, r0:r1, v0:v0 + 2 * LANES], nkeys)
            res.append(wide[:, use_b * LANES:(use_b + 1) * LANES])
        _store_heads(o_ref, res, low_half)

    if blk_off == 0:
        blk = pl.program_id(1)

        @pl.when(blk == 0)
        def _():
            attend(CTX_LEN)

        @pl.when(blk > 0)
        def _():
            attend(TOK)
    else:
        attend(TOK)


def _ax_call(q, kv, need_ctx):
    off = 0 if need_ctx else 1
    return pl.pallas_call(
        functools.partial(_ax_kernel, blk_off=off),
        grid=(BATCH, NQB - off),
        compiler_params=_params("parallel", "arbitrary"),
        name="axial_attn",
        **_attn_specs(off),
    )(q, kv)


def _win_kernel(sink_ref, q_ref, kv_ref, o_ref, *, blk_off):
    slot_a, low_half = _slot_masks()

    def attend(load_keys, valid):
        res, sink_terms, keys, vals = [], [], {}, {}
        for c, use_b, kb, vb in _ATT_HEADS:
            qs = _head_queries(q_ref, c, use_b, slot_a)
            if kb not in keys:
                keys[kb] = load_keys(kb * LANES, LANES)
            if vb - use_b not in vals:
                vals[vb - use_b] = load_keys((vb - use_b) * LANES, 2 * LANES)
            v = vals[vb - use_b]
            s = lax.dot_general(qs, keys[kb], _NT, preferred_element_type=F32)
            if valid is not None:
                s = jnp.where(valid, s, NEG)
            sk = sink_ref[2 * c + use_b] * LOG2E
            mx = jnp.maximum(jnp.max(s, axis=-1, keepdims=True), sk)
            p = jnp.exp2(s - mx).astype(BF16)
            sink_terms.append(jnp.exp2(sk - mx))
            wide = _pv(p, lambda r0, r1: v[r0:r1], v.shape[0])
            res.append(wide[:, use_b * LANES:(use_b + 1) * LANES])
        _store_heads(o_ref, res, low_half, sink_terms)

    def ctx_block():
        attend(lambda c0, w: kv_ref[0, 0:CTX_LEN, c0:c0 + w], None)

    def latent_block(n):
        left = jnp.maximum(2 * n - 1, 0)
        right = jnp.minimum(2 * n + 2, LAT_BLKS - 1)
        s_left = pl.multiple_of(CTX_LEN + left * BLOCK, BLOCK)
        s_mid = pl.multiple_of(CTX_LEN + n * QB, QB)
        s_right = pl.multiple_of(CTX_LEN + right * BLOCK, BLOCK)

        def load_keys(c0, w):
            return jnp.concatenate([kv_ref[0, 0:CTX_LEN, c0:c0 + w],
                                    kv_ref[0, pl.ds(s_left, BLOCK), c0:c0 + w],
                                    kv_ref[0, pl.ds(s_mid, QB), c0:c0 + w],
                                    kv_ref[0, pl.ds(s_right, BLOCK), c0:c0 + w]], axis=0)

        nk = CTX_LEN + QB + 2 * BLOCK
        col = lax.broadcasted_iota(jnp.int32, (QB, nk), 1)
        qi = lax.broadcasted_iota(jnp.int32, (QB, nk), 0)
        jj = col - CTX_LEN
        dlt = jj - qi
        lo = jnp.where(n >= 1, 0, BLOCK)
        hi = jnp.where(n <= SEQ // QB - 2, QB + 2 * BLOCK, QB + BLOCK)
        in_win = (dlt >= 0) & (dlt <= 2 * BLOCK) & (jj >= lo) & (jj < hi)
        attend(load_keys, (col < CTX_LEN) | in_win)

    blk = pl.program_id(1) + blk_off
    if blk_off == 0:
        pl.when(blk == 0)(ctx_block)

        @pl.when(blk > 0)
        def _():
            latent_block(blk - 1)
    else:
        latent_block(blk - 1)


def _win_call(sink, q, kv, need_ctx):
    off = 0 if need_ctx else 1
    specs = _attn_specs(off)
    specs["in_specs"] = [pl.BlockSpec(memory_space=pltpu.SMEM)] + specs["in_specs"]
    return pl.pallas_call(
        functools.partial(_win_kernel, blk_off=off),
        grid=(BATCH, NQB - off),
        compiler_params=_params("parallel", "arbitrary"),
        name="window_attn",
        **specs,
    )(sink, q, kv)


def _merge_kernel(x_ref, mod_ref, modc_ref, nw_ref, or_ref, ow_ref, oa_ref, wg_ref, wpr_ref, wpw_ref, wpa_ref,
                  wo_ref, fw_ref, out_ref, *, final):
    for sb in range(x_ref.shape[1] // SUB):
        rows = slice(sb * SUB, (sb + 1) * SUB)
        mod = _sub_mod(mod_ref, None if final else modc_ref, sb)
        x = x_ref[0, rows, :]
        h = _modulated_norm(x, mod, nw_ref)

        def branch(o_ref, wp_ref, j):
            g = jnp.dot(h, wg_ref[:, j * 512:(j + 1) * 512], preferred_element_type=F32)
            u = (o_ref[0, rows, :].astype(F32) * (g * _sigmoid(g))).astype(BF16)
            b = jnp.dot(u, wp_ref[...], preferred_element_type=F32)
            m0 = 1536 + j * D_MODEL
            m = _sigmoid(jnp.dot(h, wg_ref[:, m0:m0 + D_MODEL], preferred_element_type=F32))
            return m * b

        mix = branch(or_ref, wpr_ref, 0) + branch(ow_ref, wpw_ref, 1) + branch(oa_ref, wpa_ref, 2)
        y = jnp.dot(mix.astype(BF16), wo_ref[...], preferred_element_type=F32)
        xn = x + mod[:, 2 * D_MODEL:3 * D_MODEL] * y
        if final:
            xn = xn * lax.rsqrt(jnp.mean(xn * xn, axis=-1, keepdims=True) + EPS) * fw_ref[...]
        out_ref[0, rows, :] = xn


def _merge_call(xx, mod_l, nw, o_ret, o_win, o_ax, wg, wpr, wpw, wpa, wo, fw, final):
    tm = CTX_LEN if final else TM
    x_off = CTX_LEN // tm if final else 0
    n_tok = SEQ if final else TOK
    o_spec = pl.BlockSpec((1, tm, 512), lambda b, t: (b, t, 0))
    return pl.pallas_call(
        functools.partial(_merge_kernel, final=final),
        grid=(BATCH, n_tok // tm),
        in_specs=[pl.BlockSpec((1, tm, D_MODEL), lambda b, t: (b, t + x_off, 0)),
                  *_mod_specs(),
                  _const_spec((1, D_MODEL)),
                  o_spec, o_spec, o_spec,
                  _const_spec((D_MODEL, NGATE)),
                  _const_spec((512, D_MODEL)), _const_spec((512, D_MODEL)), _const_spec((512, D_MODEL)),
                  _const_spec((D_MODEL, D_MODEL)),
                  _const_spec((1, D_MODEL))],
        out_specs=pl.BlockSpec((1, tm, D_MODEL), lambda b, t: (b, t, 0)),
        out_shape=jax.ShapeDtypeStruct((BATCH, n_tok, D_MODEL), F32),
        compiler_params=_params("parallel", "arbitrary"),
        name="merge_final" if final else "merge",
    )(xx, mod_l, mod_l, nw, o_ret, o_win, o_ax, wg, wpr, wpw, wpa, wo, fw)


def _rope_tables():
    t = jnp.arange(TOK, dtype=F32)
    theta = ROPE_BASE ** (-jnp.linspace(0.0, 1.0, RET_DK // 2, dtype=F32))
    pos_b = jnp.where(t < CTX_LEN, CTX_LEN - 1.0 - t, 2.0 * CTX_LEN + SEQ - 1.0 - t)

    def ret_pair(pos):
        ang = pos[:, None] * theta[None]
        c, s = jnp.cos(ang), jnp.sin(ang)
        return jnp.concatenate([c, c], axis=-1), jnp.concatenate([-s, s], axis=-1)

    cf, sf = ret_pair(t)
    cb, sb = ret_pair(pos_b)

    s_idx = jnp.arange(SEQ)
    quarter = HEAD_DIM // 4
    freqs = ROPE_BASE ** (-jnp.arange(quarter, dtype=F32) / quarter)
    r = (s_idx // GRID_W).astype(F32)
    col = (s_idx % GRID_W).astype(F32)
    ang = jnp.concatenate([r[:, None] * freqs[None], col[:, None] * freqs[None]], axis=-1)
    c, s = jnp.cos(ang), jnp.sin(ang)
    ca = jnp.concatenate([jnp.ones((CTX_LEN, LANES), F32), jnp.concatenate([c, c, c, c], axis=-1)], axis=0)
    sa = jnp.concatenate([jnp.zeros((CTX_LEN, LANES), F32), jnp.concatenate([-s, -s, s, s], axis=-1)], axis=0)
    return cf, sf, cb, sb, ca, sa


def _segment_matrix():
    slot = (np.arange(LANES) // 32) % 2
    return jnp.asarray((slot[:, None] == slot[None, :]).astype(np.float32), dtype=BF16)


def kernel(x, c, ctx, c_ctx, norm_w, w_mod, b_mod, w_in, ret_decay_fwd, ret_decay_bwd, win_sink,
           ax_q_gain, ax_k_gain, w_proj_ret, w_proj_win, w_proj_ax, w_out, final_norm_w):
    xx = jnp.concatenate([ctx, x], axis=1)
    cvec = jnp.zeros((24, D_MODEL), F32).at[:BATCH].set(c).at[BATCH].set(c_ctx)
    mod = _mod_call(cvec, w_mod, b_mod)
    tabs = _rope_tables()
    seg = _segment_matrix()
    wqkv = jnp.take(w_in, jnp.asarray(_QKV_COLS), axis=2).astype(BF16)
    wgate = jnp.take(w_in, jnp.asarray(_GATE_COLS), axis=2).astype(BF16)
    gain_lanes = jnp.asarray(_GAIN_LANES)
    fw = final_norm_w.reshape(1, D_MODEL)

    out = None
    for l in range(DEPTH):
        need_ctx = l < DEPTH - 1
        mod_l = mod[l].reshape(24, 1, 3 * D_MODEL)
        nw = norm_w[l].reshape(1, D_MODEL)
        gq = ax_q_gain[l][gain_lanes].reshape(1, LANES)
        gk = ax_k_gain[l][gain_lanes].reshape(1, LANES)
        qf, qb, kf, kb, rv, wq, wkv, aq, akv = _qkv_call(xx, mod_l, nw, wqkv[l], tabs, gq, gk, seg)
        a_f = jnp.broadcast_to(ret_decay_fwd[l][:, None, None], (RET_HEADS, 8, LANES))
        a_b = jnp.broadcast_to(ret_decay_bwd[l][:, None, None], (RET_HEADS, 8, LANES))
        o_ret = _ret_call(a_f, a_b, qf, qb, kf, kb, rv, need_ctx)
        o_win = _win_call(win_sink[l], wq, wkv, need_ctx)
        o_ax = _ax_call(aq, akv, need_ctx)
        res = _merge_call(xx, mod_l, nw, o_ret, o_win, o_ax, wgate[l],
                          w_proj_ret[l].astype(BF16), w_proj_win[l].astype(BF16), w_proj_ax[l].astype(BF16),
                          w_out[l].astype(BF16), fw, final=not need_ctx)
        if need_ctx:
            xx = res
        else:
            out = res
    return out
```

```python
import functools

import numpy as np
import jax
import jax.numpy as jnp
from jax import lax
from jax.experimental import pallas as pl
from jax.experimental.pallas import tpu as pltpu

D_MODEL = 1024
BATCH = 16
SEQ = 2048
DEPTH = 4
CTX_LEN = 256
TOK = CTX_LEN + SEQ
GRID_W = 64
BLOCK = 128
RET_HEADS = 4
RET_DK = 128
HEAD_DIM = 64
ROPE_BASE = 10000.0
EPS = 1e-6
NEG = -1e30

LANES = 128
NBLK = TOK // BLOCK
CTX_BLKS = CTX_LEN // BLOCK
LAT_BLKS = SEQ // BLOCK
LOG2E = 1.4426950408889634
KV_W = 6 * LANES
QB = CTX_LEN
NQB = TOK // QB
AX_AHEAD = 2
PV_KC = 256

SUB = CTX_LEN
TM = 3 * SUB
VMEM_LIMIT = 52 * 1024 * 1024

F32 = jnp.float32
BF16 = jnp.bfloat16

_IN_SIZES = (512, 512, 512, 512, 512, 128, 128, 512, 512, 128, 128, 512, 3 * D_MODEL)
_OFF = np.concatenate([[0], np.cumsum(_IN_SIZES)]).astype(np.int64)
(_RQ, _RK, _RV, _RG, _WQ, _WK, _WV, _WG, _AQ, _AK, _AV, _AG, _MG) = [int(o) for o in _OFF[:-1]]

_EV64 = np.arange(0, 128, 2)
_OD64 = np.arange(1, 128, 2)
_EV32 = np.arange(0, 64, 2)
_OD32 = np.arange(1, 64, 2)


def _ret_cols(base):
    return np.concatenate([base + h * 128 + np.concatenate([_EV64, _OD64]) for h in range(RET_HEADS)])


def _attq_cols(base):
    out = []
    for j in range(4):
        ha, hb = base + (2 * j) * 64, base + (2 * j + 1) * 64
        out.append(np.concatenate([ha + _EV32, hb + _EV32, ha + _OD32, hb + _OD32]))
    return np.concatenate(out)


def _attk_cols(base, first, second):
    ga, gb = base + first * 64, base + second * 64
    return np.concatenate([ga + _EV32, gb + _EV32, ga + _OD32, gb + _OD32])


def _attv_cols(base, first, second):
    return np.concatenate([base + first * 64 + np.arange(64), base + second * 64 + np.arange(64)])


_QKV_COLS = np.concatenate([
    _ret_cols(_RQ), _ret_cols(_RK), _RV + np.arange(512),
    _attq_cols(_WQ), _attk_cols(_WK, 0, 1), _attk_cols(_WK, 1, 0), _attv_cols(_WV, 0, 1), _attv_cols(_WV, 1, 0),
    _attq_cols(_AQ), _attk_cols(_AK, 0, 1), _attk_cols(_AK, 1, 0), _attv_cols(_AV, 0, 1), _attv_cols(_AV, 1, 0),
]).astype(np.int32)
_GATE_COLS = np.concatenate([_RG + np.arange(512), _WG + np.arange(512), _AG + np.arange(512),
                             _MG + np.arange(3 * D_MODEL)]).astype(np.int32)
NQKV = int(_QKV_COLS.shape[0])
NGATE = int(_GATE_COLS.shape[0])
_GAIN_LANES = np.concatenate([_EV32, _EV32, _OD32, _OD32]).astype(np.int32)

_ATT_HEADS = tuple((c, b, (c // 2 + b) % 2, 2 + 2 * (c // 2) + b) for c in range(4) for b in (0, 1))

_NT = (((1,), (1,)), ((), ()))
_TN = (((0,), (0,)), ((), ()))


def _const_spec(shape):
    nd = len(shape)
    return pl.BlockSpec(shape, lambda *_: (0,) * nd, pipeline_mode=pl.Buffered(1))


def _params(*sem):
    return pltpu.CompilerParams(dimension_semantics=sem, vmem_limit_bytes=VMEM_LIMIT)


def _sigmoid(z):
    return 1.0 / (1.0 + jnp.exp(-z))


def _mod_kernel(c_ref, w_ref, b_ref, o_ref):
    cv = c_ref[...]
    s = cv * _sigmoid(cv)
    o_ref[0] = jnp.dot(s, w_ref[0], preferred_element_type=F32,
                       precision=lax.Precision.HIGHEST) + b_ref[0]


def _mod_call(cvec, w_mod, b_mod):
    rows = cvec.shape[0]
    return pl.pallas_call(
        _mod_kernel,
        grid=(DEPTH, 3),
        in_specs=[pl.BlockSpec((rows, D_MODEL), lambda l, j: (0, 0)),
                  pl.BlockSpec((1, D_MODEL, D_MODEL), lambda l, j: (l, 0, j)),
                  pl.BlockSpec((1, 1, D_MODEL), lambda l, j: (l, 0, j))],
        out_specs=pl.BlockSpec((1, rows, D_MODEL), lambda l, j: (l, 0, j)),
        out_shape=jax.ShapeDtypeStruct((DEPTH, rows, 3 * D_MODEL), F32),
        compiler_params=_params("arbitrary", "arbitrary"),
        name="adaln_mod",
    )(cvec, w_mod, b_mod.reshape(DEPTH, 1, 3 * D_MODEL))


def _sub_mod(mod_ref, modc_ref, j):
    if modc_ref is None or j > 0:
        return mod_ref[0]
    return jnp.where(pl.program_id(1) == 0, modc_ref[0], mod_ref[0])


def _modulated_norm(x, mod, nw_ref):
    ms = jnp.mean(x * x, axis=-1, keepdims=True)
    y = x * lax.rsqrt(ms + EPS) * nw_ref[...]
    return (y * (1.0 + mod[:, D_MODEL:2 * D_MODEL]) + mod[:, 0:D_MODEL]).astype(BF16)


def _qkv_kernel(x_ref, mod_ref, modc_ref, nw_ref, w_ref, cf_ref, sf_ref, cb_ref, sb_ref, ca_ref, sa_ref,
                gq_ref, gk_ref, seg_ref,
                qf_ref, qb_ref, kf_ref, kb_ref, rv_ref, wq_ref, wkv_ref, aq_ref, akv_ref):
    k_scale = RET_DK ** -0.5
    q_scale = HEAD_DIM ** -0.5 * LOG2E
    low_half = lax.broadcasted_iota(jnp.int32, (SUB, LANES), 1) < 64

    def lanes(a, j):
        return a[:, j * LANES:(j + 1) * LANES]

    def qk_norm(v, g_ref):
        ss = jnp.dot((v * v).astype(BF16), seg_ref[...], preferred_element_type=F32)
        return v * lax.rsqrt(ss * (1.0 / HEAD_DIM) + EPS) * g_ref[...]

    for sb in range(x_ref.shape[1] // SUB):
        rows = slice(sb * SUB, (sb + 1) * SUB)
        h = _modulated_norm(x_ref[0, rows, :], _sub_mod(mod_ref, modc_ref, sb), nw_ref)

        def proj(c0):
            return jnp.dot(h, w_ref[:, c0:c0 + 512], preferred_element_type=F32)

        def rope(v, c_ref, s_ref):
            return v * c_ref[rows, :] + pltpu.roll(v, 64, 1) * s_ref[rows, :]

        def put(ref, j, val):
            ref[0, rows, j * LANES:(j + 1) * LANES] = val.astype(BF16)

        def store_kv(kv_ref, acc):
            va, vb = lanes(acc, 2), lanes(acc, 3)
            put(kv_ref, 2, jnp.where(low_half, va, 1.0))
            put(kv_ref, 3, jnp.where(low_half, 1.0, vb))
            put(kv_ref, 4, jnp.where(low_half, vb, 1.0))
            put(kv_ref, 5, jnp.where(low_half, 1.0, va))

        acc = proj(0)
        for j in range(4):
            put(qf_ref, j, rope(lanes(acc, j), cf_ref, sf_ref))
            put(qb_ref, j, rope(lanes(acc, j), cb_ref, sb_ref))
        acc = proj(512)
        for j in range(4):
            put(kf_ref, j, rope(lanes(acc, j), cf_ref, sf_ref) * k_scale)
            put(kb_ref, j, rope(lanes(acc, j), cb_ref, sb_ref) * k_scale)
        rv_ref[0, rows, :] = proj(1024).astype(BF16)

        acc = proj(1536)
        for j in range(4):
            put(wq_ref, j, rope(lanes(acc, j), ca_ref, sa_ref) * q_scale)
        acc = proj(2048)
        for j in range(2):
            put(wkv_ref, j, rope(lanes(acc, j), ca_ref, sa_ref))
        store_kv(wkv_ref, acc)

        acc = proj(2560)
        for j in range(4):
            put(aq_ref, j, rope(qk_norm(lanes(acc, j), gq_ref), ca_ref, sa_ref) * q_scale)
        acc = proj(3072)
        for j in range(2):
            put(akv_ref, j, rope(qk_norm(lanes(acc, j), gk_ref), ca_ref, sa_ref))
        store_kv(akv_ref, acc)


def _mod_specs():
    return [pl.BlockSpec((1, 1, 3 * D_MODEL), lambda b, t: (b, 0, 0)),
            pl.BlockSpec((1, 1, 3 * D_MODEL), lambda b, t: (BATCH, 0, 0))]


def _qkv_call(xx, mod_l, nw, wqkv, tabs, gq, gk, seg):
    tok_spec = pl.BlockSpec((1, TM, 512), lambda b, t: (b, t, 0))
    tab_spec = pl.BlockSpec((TM, LANES), lambda b, t: (t, 0))
    out_sds = jax.ShapeDtypeStruct((BATCH, TOK, 512), BF16)
    kv_spec = pl.BlockSpec((1, TM, KV_W), lambda b, t: (b, t, 0))
    kv_sds = jax.ShapeDtypeStruct((BATCH, TOK, KV_W), BF16)
    return pl.pallas_call(
        _qkv_kernel,
        grid=(BATCH, TOK // TM),
        in_specs=[pl.BlockSpec((1, TM, D_MODEL), lambda b, t: (b, t, 0)),
                  *_mod_specs(),
                  _const_spec((1, D_MODEL)),
                  _const_spec((D_MODEL, NQKV)),
                  tab_spec, tab_spec, tab_spec, tab_spec, tab_spec, tab_spec,
                  _const_spec((1, LANES)), _const_spec((1, LANES)), _const_spec((LANES, LANES))],
        out_specs=[tok_spec] * 6 + [kv_spec, tok_spec, kv_spec],
        out_shape=[out_sds] * 6 + [kv_sds, out_sds, kv_sds],
        compiler_params=_params("parallel", "arbitrary"),
        name="qkv_proj",
    )(xx, mod_l, mod_l, nw, wqkv, *tabs, gq, gk, seg)


def _log_sigmoid(a):
    return jnp.minimum(a, 0.0) - jnp.log1p(jnp.exp(-jnp.abs(a)))


def _ret_kernel(af_ref, ab_ref, qf_ref, qb_ref, kf_ref, kb_ref, v_ref, o_ref, acc_ref, st_ref, dec_ref,
                *, need_ctx):
    ri = lax.broadcasted_iota(jnp.int32, (BLOCK, BLOCK), 0).astype(F32)
    ci = lax.broadcasted_iota(jnp.int32, (BLOCK, BLOCK), 1).astype(F32)
    d = ri - ci
    for h in range(RET_HEADS):
        lgf = _log_sigmoid(af_ref[h])[0:1, :]
        lgb = _log_sigmoid(ab_ref[h])[0:1, :]
        dec_ref[2 * h, 0] = jnp.where(d >= 0, jnp.exp(jnp.maximum(d, 0.0) * lgf), 0.0)
        dec_ref[2 * h, 1] = jnp.exp((ri + 1.0) * lgf)
        dec_ref[2 * h, 2] = jnp.exp((BLOCK - 1.0 - ri) * lgf)
        dec_ref[2 * h, 3] = jnp.exp(0.0 * ri + float(BLOCK) * lgf)
        dec_ref[2 * h + 1, 0] = jnp.where(d < 0, jnp.exp(jnp.maximum(-d, 0.0) * lgb), 0.0)
        dec_ref[2 * h + 1, 1] = jnp.exp((float(BLOCK) - ri) * lgb)
        dec_ref[2 * h + 1, 2] = jnp.exp(ri * lgb)
        dec_ref[2 * h + 1, 3] = jnp.exp(0.0 * ri + float(BLOCK) * lgb)
    st_ref[...] = jnp.zeros_like(st_ref)
    first_row = 0 if need_ctx else CTX_LEN

    def visit(i, in_ctx, first, want_o):
        cf = i
        cb = (CTX_BLKS - 1 - i) if in_ctx else (NBLK + CTX_BLKS - 1 - i)
        pending = []
        for h in range(RET_HEADS):
            lanes_h = slice(h * LANES, (h + 1) * LANES)
            for dr, (q_ref, k_ref, c) in enumerate(((qf_ref, kf_ref, cf), (qb_ref, kb_ref, cb))):
                r0 = c * BLOCK if in_ctx else pl.multiple_of(c * BLOCK, BLOCK)
                j = 2 * h + dr
                k = k_ref[0, pl.ds(r0, BLOCK), lanes_h]
                v = v_ref[0, pl.ds(r0, BLOCK), lanes_h]
                state = st_ref[j]
                kd = (k.astype(F32) * dec_ref[j, 2]).astype(BF16)
                st_ref[j] = dec_ref[j, 3] * state + lax.dot_general(kd, v, _TN, preferred_element_type=F32)
                if want_o:
                    q = q_ref[0, pl.ds(r0, BLOCK), lanes_h]
                    s = lax.dot_general(q, k, _NT, preferred_element_type=F32)
                    cross = jnp.dot(q, state.astype(BF16), preferred_element_type=F32)
                    pending.append((j, r0, lanes_h, v, s, cross))
        for j, r0, lanes_h, v, s, cross in pending:
            o = jnp.dot((s * dec_ref[j, 0]).astype(BF16), v, preferred_element_type=F32) + cross * dec_ref[j, 1]
            if first:
                acc_ref[pl.ds(r0, BLOCK), lanes_h] = o
            else:
                tot = acc_ref[pl.ds(r0, BLOCK), lanes_h] + o
                tot = tot * lax.rsqrt(jnp.mean(tot * tot, axis=-1, keepdims=True) + EPS)
                o_ref[0, pl.ds(r0 - first_row, BLOCK), lanes_h] = tot.astype(o_ref.dtype)

    for i in range(CTX_BLKS):
        visit(i, True, i < CTX_BLKS // 2, need_ctx)
    half = (NBLK + CTX_BLKS) // 2

    def first_pass(i, carry):
        visit(i, False, True, True)
        return carry

    def second_pass(i, carry):
        visit(i, False, False, True)
        return carry

    lax.fori_loop(CTX_BLKS, half, first_pass, 0)
    lax.fori_loop(half, NBLK, second_pass, 0)


def _ret_call(a_f, a_b, qf, qb, kf, kb, rv, need_ctx):
    n_out = TOK if need_ctx else SEQ
    tok_spec = pl.BlockSpec((1, TOK, 512), lambda b: (b, 0, 0))
    a_spec = _const_spec((RET_HEADS, 8, LANES))
    return pl.pallas_call(
        functools.partial(_ret_kernel, need_ctx=need_ctx),
        grid=(BATCH,),
        in_specs=[a_spec, a_spec, tok_spec, tok_spec, tok_spec, tok_spec, tok_spec],
        out_specs=pl.BlockSpec((1, n_out, 512), lambda b: (b, 0, 0)),
        out_shape=jax.ShapeDtypeStruct((BATCH, n_out, 512), BF16),
        scratch_shapes=[pltpu.VMEM((TOK, 512), F32),
                        pltpu.VMEM((2 * RET_HEADS, RET_DK, LANES), F32),
                        pltpu.VMEM((2 * RET_HEADS, 4, BLOCK, BLOCK), F32)],
        compiler_params=_params("parallel"),
        name="retention",
    )(a_f, a_b, qf, qb, kf, kb, rv)


def _slot_masks():
    lane = lax.broadcasted_iota(jnp.int32, (QB, LANES), 1)
    slot_a = (lane & 32) == 0
    return slot_a, lane < 64


def _head_queries(q_ref, c, use_b, slot_a):
    qv = q_ref[0, :, c * LANES:(c + 1) * LANES].astype(F32)
    qv = jnp.where(slot_a, 0.0, qv) if use_b else jnp.where(slot_a, qv, 0.0)
    return qv.astype(BF16)


def _pv(p, load_v, nkeys):
    acc = None
    for r0 in range(0, nkeys, PV_KC):
        d = jnp.dot(p[:, r0:r0 + PV_KC], load_v(r0, r0 + PV_KC), preferred_element_type=F32)
        acc = d if acc is None else acc + d
    return acc


def _store_heads(o_ref, res, low_half, extra=None):
    for c in range(4):
        ra, rb = res[2 * c], res[2 * c + 1]
        num = jnp.where(low_half, ra, rb)
        den = pltpu.roll(jnp.where(low_half, rb, ra), 64, 1)
        if extra is not None:
            den = den + jnp.where(low_half, extra[2 * c], extra[2 * c + 1])
        o_ref[0, :, c * LANES:(c + 1) * LANES] = (num * (1.0 / den)).astype(o_ref.dtype)


def _attn_specs(off):
    return dict(
        in_specs=[pl.BlockSpec((1, QB, 512), lambda b, i: (b, i + off, 0)),
                  pl.BlockSpec((1, TOK, KV_W), lambda b, i: (b, 0, 0))],
        out_specs=pl.BlockSpec((1, QB, 512), lambda b, i: (b, i, 0)),
        out_shape=jax.ShapeDtypeStruct((BATCH, (NQB - off) * QB, 512), BF16),
    )


def _ax_kernel(q_ref, qn_ref, kv_ref, o_ref, s_ref, m_ref, *, blk_off):
    slot_a, low_half = _slot_masks()

    def scores(qr, head, nkeys):
        c, use_b, kb, _ = head
        qs = _head_queries(qr, c, use_b, slot_a)
        k = kv_ref[0, 0:nkeys, kb * LANES:(kb + 1) * LANES]
        return lax.dot_general(qs, k, _NT, preferred_element_type=F32)

    def carry(slot, s):
        s_ref[slot] = s
        m_ref[slot] = jnp.broadcast_to(jnp.max(s, axis=-1, keepdims=True), m_ref.shape[1:])

    def finish(s, head, nkeys, mx=None):
        _, use_b, _, vb = head
        if mx is None:
            mx = jnp.max(s, axis=-1, keepdims=True)
        p = jnp.exp2(s - mx).astype(BF16)
        v0 = (vb - use_b) * LANES
        wide = _pv(p, lambda r0, r1: kv_ref[0, r0:r1, v0:v0 + 2 * LANES], nkeys)
        return wide[:, use_b * LANES:(use_b + 1) * LANES]

    nh = len(_ATT_HEADS)

    def prologue(qr):
        for a in range(AX_AHEAD):
            carry(a, scores(qr, _ATT_HEADS[a], TOK))

    def ctx_block():
        res = [finish(scores(q_ref, head, CTX_LEN), head, CTX_LEN) for head in _ATT_HEADS]
        _store_heads(o_ref, res, low_half)
        prologue(qn_ref)

    def latent_block():
        res = []
        ahead = [(s_ref[a], m_ref[a, :, 0:1]) for a in range(AX_AHEAD)]
        for n, head in enumerate(_ATT_HEADS):
            m = n + AX_AHEAD
            ahead.append((scores(q_ref, _ATT_HEADS[m], TOK) if m < nh else scores(qn_ref, _ATT_HEADS[m - nh], TOK), None))
            s, mx = ahead.pop(0)
            res.append(finish(s, head, TOK, mx))
        for a in range(AX_AHEAD):
            carry(a, ahead[a][0])
        _store_heads(o_ref, res, low_half)

    blk = pl.program_id(1)
    if blk_off == 0:
        pl.when(blk == 0)(ctx_block)
        pl.when(blk > 0)(latent_block)
    else:
        pl.when(blk == 0)(lambda: prologue(q_ref))
        latent_block()


def _ax_call(q, kv, need_ctx):
    off = 0 if need_ctx else 1
    specs = _attn_specs(off)
    q_next = pl.BlockSpec((1, QB, 512), lambda b, i: (b, jnp.minimum(i + off + 1, NQB - 1), 0))
    specs["in_specs"] = [specs["in_specs"][0], q_next, specs["in_specs"][1]]
    return pl.pallas_call(
        functools.partial(_ax_kernel, blk_off=off),
        grid=(BATCH, NQB - off),
        scratch_shapes=[pltpu.VMEM((AX_AHEAD, QB, TOK), F32), pltpu.VMEM((AX_AHEAD, QB, LANES), F32)],
        compiler_params=_params("parallel", "arbitrary"),
        name="axial_attn",
        **specs,
    )(q, q, kv)


def _win_kernel(sink_ref, q_ref, kv_ref, o_ref, *, blk_off):
    slot_a, low_half = _slot_masks()

    def attend(load_keys, valid):
        res, sink_terms, keys, vals = [], [], {}, {}
        for c, use_b, kb, vb in _ATT_HEADS:
            qs = _head_queries(q_ref, c, use_b, slot_a)
            if kb not in keys:
                keys[kb] = load_keys(kb * LANES, LANES)
            if vb - use_b not in vals:
                vals[vb - use_b] = load_keys((vb - use_b) * LANES, 2 * LANES)
            v = vals[vb - use_b]
            s = lax.dot_general(qs, keys[kb], _NT, preferred_element_type=F32)
            if valid is not None:
                s = jnp.where(valid, s, NEG)
            sk = sink_ref[2 * c + use_b] * LOG2E
            mx = jnp.maximum(jnp.max(s, axis=-1, keepdims=True), sk)
            p = jnp.exp2(s - mx).astype(BF16)
            sink_terms.append(jnp.exp2(sk - mx))
            wide = _pv(p, lambda r0, r1: v[r0:r1], v.shape[0])
            res.append(wide[:, use_b * LANES:(use_b + 1) * LANES])
        _store_heads(o_ref, res, low_half, sink_terms)

    def ctx_block():
        attend(lambda c0, w: kv_ref[0, 0:CTX_LEN, c0:c0 + w], None)

    def latent_block(n):
        left = jnp.maximum(2 * n - 1, 0)
        right = jnp.minimum(2 * n + 2, LAT_BLKS - 1)
        s_left = pl.multiple_of(CTX_LEN + left * BLOCK, BLOCK)
        s_mid = pl.multiple_of(CTX_LEN + n * QB, QB)
        s_right = pl.multiple_of(CTX_LEN + right * BLOCK, BLOCK)

        def load_keys(c0, w):
            return jnp.concatenate([kv_ref[0, 0:CTX_LEN, c0:c0 + w],
                                    kv_ref[0, pl.ds(s_left, BLOCK), c0:c0 + w],
                                    kv_ref[0, pl.ds(s_mid, QB), c0:c0 + w],
                                    kv_ref[0, pl.ds(s_right, BLOCK), c0:c0 + w]], axis=0)

        nk = CTX_LEN + QB + 2 * BLOCK
        col = lax.broadcasted_iota(jnp.int32, (QB, nk), 1)
        qi = lax.broadcasted_iota(jnp.int32, (QB, nk), 0)
        jj = col - CTX_LEN
        dlt = jj - qi
        lo = jnp.where(n >= 1, 0, BLOCK)
        hi = jnp.where(n <= SEQ // QB - 2, QB + 2 * BLOCK, QB + BLOCK)
        in_win = (dlt >= 0) & (dlt <= 2 * BLOCK) & (jj >= lo) & (jj < hi)
        attend(load_keys, (col < CTX_LEN) | in_win)

    blk = pl.program_id(1) + blk_off
    if blk_off == 0:
        pl.when(blk == 0)(ctx_block)

        @pl.when(blk > 0)
        def _():
            latent_block(blk - 1)
    else:
        latent_block(blk - 1)


def _win_call(sink, q, kv, need_ctx):
    off = 0 if need_ctx else 1
    specs = _attn_specs(off)
    specs["in_specs"] = [pl.BlockSpec(memory_space=pltpu.SMEM)] + specs["in_specs"]
    return pl.pallas_call(
        functools.partial(_win_kernel, blk_off=off),
        grid=(BATCH, NQB - off),
        compiler_params=_params("parallel", "arbitrary"),
        name="window_attn",
        **specs,
    )(sink, q, kv)


def _merge_kernel(x_ref, mod_ref, modc_ref, nw_ref, or_ref, ow_ref, oa_ref, wg_ref, wpr_ref, wpw_ref, wpa_ref,
                  wo_ref, fw_ref, out_ref, *, final):
    for sb in range(x_ref.shape[1] // SUB):
        rows = slice(sb * SUB, (sb + 1) * SUB)
        mod = _sub_mod(mod_ref, None if final else modc_ref, sb)
        x = x_ref[0, rows, :]
        h = _modulated_norm(x, mod, nw_ref)

        def branch(o_ref, wp_ref, j):
            g = jnp.dot(h, wg_ref[:, j * 512:(j + 1) * 512], preferred_element_type=F32)
            u = (o_ref[0, rows, :].astype(F32) * (g * _sigmoid(g))).astype(BF16)
            b = jnp.dot(u, wp_ref[...], preferred_element_type=F32)
            m0 = 1536 + j * D_MODEL
            m = _sigmoid(jnp.dot(h, wg_ref[:, m0:m0 + D_MODEL], preferred_element_type=F32))
            return m * b

        mix = branch(or_ref, wpr_ref, 0) + branch(ow_ref, wpw_ref, 1) + branch(oa_ref, wpa_ref, 2)
        y = jnp.dot(mix.astype(BF16), wo_ref[...], preferred_element_type=F32)
        xn = x + mod[:, 2 * D_MODEL:3 * D_MODEL] * y
        if final:
            xn = xn * lax.rsqrt(jnp.mean(xn * xn, axis=-1, keepdims=True) + EPS) * fw_ref[...]
        out_ref[0, rows, :] = xn


def _merge_call(xx, mod_l, nw, o_ret, o_win, o_ax, wg, wpr, wpw, wpa, wo, fw, final):
    tm = CTX_LEN if final else TM
    x_off = CTX_LEN // tm if final else 0
    n_tok = SEQ if final else TOK
    o_spec = pl.BlockSpec((1, tm, 512), lambda b, t: (b, t, 0))
    return pl.pallas_call(
        functools.partial(_merge_kernel, final=final),
        grid=(BATCH, n_tok // tm),
        in_specs=[pl.BlockSpec((1, tm, D_MODEL), lambda b, t: (b, t + x_off, 0)),
                  *_mod_specs(),
                  _const_spec((1, D_MODEL)),
                  o_spec, o_spec, o_spec,
                  _const_spec((D_MODEL, NGATE)),
                  _const_spec((512, D_MODEL)), _const_spec((512, D_MODEL)), _const_spec((512, D_MODEL)),
                  _const_spec((D_MODEL, D_MODEL)),
                  _const_spec((1, D_MODEL))],
        out_specs=pl.BlockSpec((1, tm, D_MODEL), lambda b, t: (b, t, 0)),
        out_shape=jax.ShapeDtypeStruct((BATCH, n_tok, D_MODEL), F32),
        compiler_params=_params("parallel", "arbitrary"),
        name="merge_final" if final else "merge",
    )(xx, mod_l, mod_l, nw, o_ret, o_win, o_ax, wg, wpr, wpw, wpa, wo, fw)


def _rope_tables():
    t = jnp.arange(TOK, dtype=F32)
    theta = ROPE_BASE ** (-jnp.linspace(0.0, 1.0, RET_DK // 2, dtype=F32))
    pos_b = jnp.where(t < CTX_LEN, CTX_LEN - 1.0 - t, 2.0 * CTX_LEN + SEQ - 1.0 - t)

    def ret_pair(pos):
        ang = pos[:, None] * theta[None]
        c, s = jnp.cos(ang), jnp.sin(ang)
        return jnp.concatenate([c, c], axis=-1), jnp.concatenate([-s, s], axis=-1)

    cf, sf = ret_pair(t)
    cb, sb = ret_pair(pos_b)

    s_idx = jnp.arange(SEQ)
    quarter = HEAD_DIM // 4
    freqs = ROPE_BASE ** (-jnp.arange(quarter, dtype=F32) / quarter)
    r = (s_idx // GRID_W).astype(F32)
    col = (s_idx % GRID_W).astype(F32)
    ang = jnp.concatenate([r[:, None] * freqs[None], col[:, None] * freqs[None]], axis=-1)
    c, s = jnp.cos(ang), jnp.sin(ang)
    ca = jnp.concatenate([jnp.ones((CTX_LEN, LANES), F32), jnp.concatenate([c, c, c, c], axis=-1)], axis=0)
    sa = jnp.concatenate([jnp.zeros((CTX_LEN, LANES), F32), jnp.concatenate([-s, -s, s, s], axis=-1)], axis=0)
    return cf, sf, cb, sb, ca, sa


def _segment_matrix():
    slot = (np.arange(LANES) // 32) % 2
    return jnp.asarray((slot[:, None] == slot[None, :]).astype(np.float32), dtype=BF16)


def kernel(x, c, ctx, c_ctx, norm_w, w_mod, b_mod, w_in, ret_decay_fwd, ret_decay_bwd, win_sink,
           ax_q_gain, ax_k_gain, w_proj_ret, w_proj_win, w_proj_ax, w_out, final_norm_w):
    xx = jnp.concatenate([ctx, x], axis=1)
    cvec = jnp.zeros((24, D_MODEL), F32).at[:BATCH].set(c).at[BATCH].set(c_ctx)
    mod = _mod_call(cvec, w_mod, b_mod)
    tabs = _rope_tables()
    seg = _segment_matrix()
    wqkv = jnp.take(w_in, jnp.asarray(_QKV_COLS), axis=2).astype(BF16)
    wgate = jnp.take(w_in, jnp.asarray(_GATE_COLS), axis=2).astype(BF16)
    gain_lanes = jnp.asarray(_GAIN_LANES)
    fw = final_norm_w.reshape(1, D_MODEL)

    out = None
    for l in range(DEPTH):
        need_ctx = l < DEPTH - 1
        mod_l = mod[l].reshape(24, 1, 3 * D_MODEL)
        nw = norm_w[l].reshape(1, D_MODEL)
        gq = ax_q_gain[l][gain_lanes].reshape(1, LANES)
        gk = ax_k_gain[l][gain_lanes].reshape(1, LANES)
        qf, qb, kf, kb, rv, wq, wkv, aq, akv = _qkv_call(xx, mod_l, nw, wqkv[l], tabs, gq, gk, seg)
        a_f = jnp.broadcast_to(ret_decay_fwd[l][:, None, None], (RET_HEADS, 8, LANES))
        a_b = jnp.broadcast_to(ret_decay_bwd[l][:, None, None], (RET_HEADS, 8, LANES))
        o_ret = _ret_call(a_f, a_b, qf, qb, kf, kb, rv, need_ctx)
        o_win = _win_call(win_sink[l], wq, wkv, need_ctx)
        o_ax = _ax_call(aq, akv, need_ctx)
        res = _merge_call(xx, mod_l, nw, o_ret, o_win, o_ax, wgate[l],
                          w_proj_ret[l].astype(BF16), w_proj_win[l].astype(BF16), w_proj_ax[l].astype(BF16),
                          w_out[l].astype(BF16), fw, final=not need_ctx)
        if need_ctx:
            xx = res
        else:
            out = res
    return out
```

```python
import functools

import numpy as np
import jax
import jax.numpy as jnp
from jax import lax
from jax.experimental import pallas as pl
from jax.experimental.pallas import tpu as pltpu

D_MODEL = 1024
BATCH = 16
SEQ = 2048
DEPTH = 4
CTX_LEN = 256
TOK = CTX_LEN + SEQ
GRID_W = 64
BLOCK = 128
RET_HEADS = 4
RET_DK = 128
HEAD_DIM = 64
ROPE_BASE = 10000.0
EPS = 1e-6
NEG = -1e30

LANES = 128
NBLK = TOK // BLOCK
CTX_BLKS = CTX_LEN // BLOCK
LAT_BLKS = SEQ // BLOCK
LOG2E = 1.4426950408889634
KV_W = 6 * LANES
QB = CTX_LEN
NQB = TOK // QB
AX_AHEAD = 2
WIN_AHEAD = 2
PV_KC = 256

SUB = CTX_LEN
TM = 3 * SUB
VMEM_LIMIT = 52 * 1024 * 1024

F32 = jnp.float32
BF16 = jnp.bfloat16

_IN_SIZES = (512, 512, 512, 512, 512, 128, 128, 512, 512, 128, 128, 512, 3 * D_MODEL)
_OFF = np.concatenate([[0], np.cumsum(_IN_SIZES)]).astype(np.int64)
(_RQ, _RK, _RV, _RG, _WQ, _WK, _WV, _WG, _AQ, _AK, _AV, _AG, _MG) = [int(o) for o in _OFF[:-1]]

_EV64 = np.arange(0, 128, 2)
_OD64 = np.arange(1, 128, 2)
_EV32 = np.arange(0, 64, 2)
_OD32 = np.arange(1, 64, 2)


def _ret_cols(base):
    return np.concatenate([base + h * 128 + np.concatenate([_EV64, _OD64]) for h in range(RET_HEADS)])


def _attq_cols(base):
    out = []
    for j in range(4):
        ha, hb = base + (2 * j) * 64, base + (2 * j + 1) * 64
        out.append(np.concatenate([ha + _EV32, hb + _EV32, ha + _OD32, hb + _OD32]))
    return np.concatenate(out)


def _attk_cols(base, first, second):
    ga, gb = base + first * 64, base + second * 64
    return np.concatenate([ga + _EV32, gb + _EV32, ga + _OD32, gb + _OD32])


def _attv_cols(base, first, second):
    return np.concatenate([base + first * 64 + np.arange(64), base + second * 64 + np.arange(64)])


_QKV_COLS = np.concatenate([
    _ret_cols(_RQ), _ret_cols(_RK), _RV + np.arange(512),
    _attq_cols(_WQ), _attk_cols(_WK, 0, 1), _attk_cols(_WK, 1, 0), _attv_cols(_WV, 0, 1), _attv_cols(_WV, 1, 0),
    _attq_cols(_AQ), _attk_cols(_AK, 0, 1), _attk_cols(_AK, 1, 0), _attv_cols(_AV, 0, 1), _attv_cols(_AV, 1, 0),
]).astype(np.int32)
NQKV = int(_QKV_COLS.shape[0])
IN_W = int(_OFF[-1])
_GAIN_LANES = np.concatenate([_EV32, _EV32, _OD32, _OD32]).astype(np.int32)

_ATT_HEADS = tuple((c, b, (c // 2 + b) % 2, 2 + 2 * (c // 2) + b) for c in range(4) for b in (0, 1))

_NT = (((1,), (1,)), ((), ()))
_TN = (((0,), (0,)), ((), ()))


def _const_spec(shape):
    nd = len(shape)
    return pl.BlockSpec(shape, lambda *_: (0,) * nd, pipeline_mode=pl.Buffered(1))


def _params(*sem):
    return pltpu.CompilerParams(dimension_semantics=sem, vmem_limit_bytes=VMEM_LIMIT)


def _sigmoid(z):
    return 1.0 / (1.0 + jnp.exp(-z))


def _mod_kernel(c_ref, w_ref, b_ref, o_ref):
    cv = c_ref[...]
    s = cv * _sigmoid(cv)
    o_ref[0] = jnp.dot(s, w_ref[0], preferred_element_type=F32,
                       precision=lax.Precision.HIGHEST) + b_ref[0]


def _mod_call(cvec, w_mod, b_mod):
    rows = cvec.shape[0]
    return pl.pallas_call(
        _mod_kernel,
        grid=(DEPTH, 3),
        in_specs=[pl.BlockSpec((rows, D_MODEL), lambda l, j: (0, 0)),
                  pl.BlockSpec((1, D_MODEL, D_MODEL), lambda l, j: (l, 0, j)),
                  pl.BlockSpec((1, 1, D_MODEL), lambda l, j: (l, 0, j))],
        out_specs=pl.BlockSpec((1, rows, D_MODEL), lambda l, j: (l, 0, j)),
        out_shape=jax.ShapeDtypeStruct((DEPTH, rows, 3 * D_MODEL), F32),
        compiler_params=_params("arbitrary", "arbitrary"),
        name="adaln_mod",
    )(cvec, w_mod, b_mod.reshape(DEPTH, 1, 3 * D_MODEL))


def _sub_mod(mod_ref, modc_ref, j):
    if modc_ref is None or j > 0:
        return mod_ref[0]
    return jnp.where(pl.program_id(1) == 0, modc_ref[0], mod_ref[0])


def _x_operands(xx, tm, x_off=0):
    if not isinstance(xx, tuple):
        return [xx], [pl.BlockSpec((1, tm, D_MODEL), lambda b, t: (b, t + x_off, 0))]
    ctx, x = xx
    nsub = tm // SUB
    specs = [pl.BlockSpec((1, SUB, D_MODEL), lambda b, t: (b, 0, 0))]
    for sb in range(nsub):
        specs.append(pl.BlockSpec((1, SUB, D_MODEL), lambda b, t, sb=sb: (b, jnp.maximum(nsub * t + sb - 1, 0), 0)))
    return [ctx] + [x] * nsub, specs


def _read_x(x_refs, sb):
    if len(x_refs) == 1:
        return x_refs[0][0, sb * SUB:(sb + 1) * SUB, :]
    if sb == 0:
        return jnp.where(pl.program_id(1) == 0, x_refs[0][0], x_refs[1][0])
    return x_refs[1 + sb][0]


def _modulated_norm(x, mod, nw_ref):
    ms = jnp.mean(x * x, axis=-1, keepdims=True)
    y = x * lax.rsqrt(ms + EPS) * nw_ref[...]
    return (y * (1.0 + mod[:, D_MODEL:2 * D_MODEL]) + mod[:, 0:D_MODEL]).astype(BF16)


def _qkv_kernel(*refs, n_x):
    x_refs = refs[:n_x]
    (mod_ref, modc_ref, nw_ref, w_ref, cf_ref, sf_ref, cb_ref, sb_ref, ca_ref, sa_ref, gq_ref, gk_ref, seg_ref,
     qf_ref, qb_ref, kf_ref, kb_ref, rv_ref, wq_ref, wkv_ref, aq_ref, akv_ref) = refs[n_x:]
    k_scale = RET_DK ** -0.5
    q_scale = HEAD_DIM ** -0.5 * LOG2E
    low_half = lax.broadcasted_iota(jnp.int32, (SUB, LANES), 1) < 64

    def lanes(a, j):
        return a[:, j * LANES:(j + 1) * LANES]

    def qk_norm(v, g_ref):
        ss = jnp.dot((v * v).astype(BF16), seg_ref[...], preferred_element_type=F32)
        return v * lax.rsqrt(ss * (1.0 / HEAD_DIM) + EPS) * g_ref[...]

    for sb in range(qf_ref.shape[1] // SUB):
        rows = slice(sb * SUB, (sb + 1) * SUB)
        h = _modulated_norm(_read_x(x_refs, sb), _sub_mod(mod_ref, modc_ref, sb), nw_ref)

        def proj(c0):
            return jnp.dot(h, w_ref[:, c0:c0 + 512], preferred_element_type=F32)

        def rope(v, c_ref, s_ref):
            return v * c_ref[rows, :] + pltpu.roll(v, 64, 1) * s_ref[rows, :]

        def put(ref, j, val):
            ref[0, rows, j * LANES:(j + 1) * LANES] = val.astype(BF16)

        def store_kv(kv_ref, acc):
            va, vb = lanes(acc, 2), lanes(acc, 3)
            put(kv_ref, 2, jnp.where(low_half, va, 1.0))
            put(kv_ref, 3, jnp.where(low_half, 1.0, vb))
            put(kv_ref, 4, jnp.where(low_half, vb, 1.0))
            put(kv_ref, 5, jnp.where(low_half, 1.0, va))

        acc = proj(0)
        for j in range(4):
            put(qf_ref, j, rope(lanes(acc, j), cf_ref, sf_ref))
            put(qb_ref, j, rope(lanes(acc, j), cb_ref, sb_ref))
        acc = proj(512)
        for j in range(4):
            put(kf_ref, j, rope(lanes(acc, j), cf_ref, sf_ref) * k_scale)
            put(kb_ref, j, rope(lanes(acc, j), cb_ref, sb_ref) * k_scale)
        rv_ref[0, rows, :] = proj(1024).astype(BF16)

        acc = proj(1536)
        for j in range(4):
            put(wq_ref, j, rope(lanes(acc, j), ca_ref, sa_ref) * q_scale)
        acc = proj(2048)
        for j in range(2):
            put(wkv_ref, j, rope(lanes(acc, j), ca_ref, sa_ref))
        store_kv(wkv_ref, acc)

        acc = proj(2560)
        for j in range(4):
            put(aq_ref, j, rope(qk_norm(lanes(acc, j), gq_ref), ca_ref, sa_ref) * q_scale)
        acc = proj(3072)
        for j in range(2):
            put(akv_ref, j, rope(qk_norm(lanes(acc, j), gk_ref), ca_ref, sa_ref))
        store_kv(akv_ref, acc)


def _mod_specs():
    return [pl.BlockSpec((1, 1, 3 * D_MODEL), lambda b, t: (b, 0, 0)),
            pl.BlockSpec((1, 1, 3 * D_MODEL), lambda b, t: (BATCH, 0, 0))]


def _qkv_call(xx, mod_l, nw, wqkv, tabs, gq, gk, seg):
    tok_spec = pl.BlockSpec((1, TM, 512), lambda b, t: (b, t, 0))
    tab_spec = pl.BlockSpec((TM, LANES), lambda b, t: (t, 0))
    out_sds = jax.ShapeDtypeStruct((BATCH, TOK, 512), BF16)
    kv_spec = pl.BlockSpec((1, TM, KV_W), lambda b, t: (b, t, 0))
    kv_sds = jax.ShapeDtypeStruct((BATCH, TOK, KV_W), BF16)
    x_ops, x_specs = _x_operands(xx, TM)
    return pl.pallas_call(
        functools.partial(_qkv_kernel, n_x=len(x_ops)),
        grid=(BATCH, TOK // TM),
        in_specs=[*x_specs,
                  *_mod_specs(),
                  _const_spec((1, D_MODEL)),
                  _const_spec((D_MODEL, NQKV)),
                  tab_spec, tab_spec, tab_spec, tab_spec, tab_spec, tab_spec,
                  _const_spec((1, LANES)), _const_spec((1, LANES)), _const_spec((LANES, LANES))],
        out_specs=[tok_spec] * 6 + [kv_spec, tok_spec, kv_spec],
        out_shape=[out_sds] * 6 + [kv_sds, out_sds, kv_sds],
        compiler_params=_params("parallel", "arbitrary"),
        name="qkv_proj",
    )(*x_ops, mod_l, mod_l, nw, wqkv, *tabs, gq, gk, seg)


def _log_sigmoid(a):
    return jnp.minimum(a, 0.0) - jnp.log1p(jnp.exp(-jnp.abs(a)))


def _ret_kernel(af_ref, ab_ref, qf_ref, qb_ref, kf_ref, kb_ref, v_ref, o_ref, acc_ref, st_ref, dec_ref,
                *, need_ctx):
    ri = lax.broadcasted_iota(jnp.int32, (BLOCK, BLOCK), 0).astype(F32)
    ci = lax.broadcasted_iota(jnp.int32, (BLOCK, BLOCK), 1).astype(F32)
    d = ri - ci
    for h in range(RET_HEADS):
        lgf = _log_sigmoid(af_ref[h])[0:1, :]
        lgb = _log_sigmoid(ab_ref[h])[0:1, :]
        dec_ref[2 * h, 0] = jnp.where(d >= 0, jnp.exp(jnp.maximum(d, 0.0) * lgf), 0.0)
        dec_ref[2 * h, 1] = jnp.exp((ri + 1.0) * lgf)
        dec_ref[2 * h, 2] = jnp.exp((BLOCK - 1.0 - ri) * lgf)
        dec_ref[2 * h, 3] = jnp.exp(0.0 * ri + float(BLOCK) * lgf)
        dec_ref[2 * h + 1, 0] = jnp.where(d < 0, jnp.exp(jnp.maximum(-d, 0.0) * lgb), 0.0)
        dec_ref[2 * h + 1, 1] = jnp.exp((float(BLOCK) - ri) * lgb)
        dec_ref[2 * h + 1, 2] = jnp.exp(ri * lgb)
        dec_ref[2 * h + 1, 3] = jnp.exp(0.0 * ri + float(BLOCK) * lgb)
    st_ref[...] = jnp.zeros_like(st_ref)
    first_row = 0 if need_ctx else CTX_LEN

    def visit(i, in_ctx, first, want_o):
        cf = i
        cb = (CTX_BLKS - 1 - i) if in_ctx else (NBLK + CTX_BLKS - 1 - i)
        pending = []
        for h in range(RET_HEADS):
            lanes_h = slice(h * LANES, (h + 1) * LANES)
            for dr, (q_ref, k_ref, c) in enumerate(((qf_ref, kf_ref, cf), (qb_ref, kb_ref, cb))):
                r0 = c * BLOCK if in_ctx else pl.multiple_of(c * BLOCK, BLOCK)
                j = 2 * h + dr
                k = k_ref[0, pl.ds(r0, BLOCK), lanes_h]
                v = v_ref[0, pl.ds(r0, BLOCK), lanes_h]
                state = st_ref[j]
                kd = (k.astype(F32) * dec_ref[j, 2]).astype(BF16)
                st_ref[j] = dec_ref[j, 3] * state + lax.dot_general(kd, v, _TN, preferred_element_type=F32)
                if want_o:
                    q = q_ref[0, pl.ds(r0, BLOCK), lanes_h]
                    s = lax.dot_general(q, k, _NT, preferred_element_type=F32)
                    cross = jnp.dot(q, state.astype(BF16), preferred_element_type=F32)
                    pending.append((j, r0, lanes_h, v, s, cross))
        for j, r0, lanes_h, v, s, cross in pending:
            o = jnp.dot((s * dec_ref[j, 0]).astype(BF16), v, preferred_element_type=F32) + cross * dec_ref[j, 1]
            if first:
                acc_ref[pl.ds(r0, BLOCK), lanes_h] = o
            else:
                tot = acc_ref[pl.ds(r0, BLOCK), lanes_h] + o
                tot = tot * lax.rsqrt(jnp.mean(tot * tot, axis=-1, keepdims=True) + EPS)
                o_ref[0, pl.ds(r0 - first_row, BLOCK), lanes_h] = tot.astype(o_ref.dtype)

    for i in range(CTX_BLKS):
        visit(i, True, i < CTX_BLKS // 2, need_ctx)
    half = (NBLK + CTX_BLKS) // 2

    def first_pass(i, carry):
        visit(i, False, True, True)
        return carry

    def second_pass(i, carry):
        visit(i, False, False, True)
        return carry

    lax.fori_loop(CTX_BLKS, half, first_pass, 0)
    lax.fori_loop(half, NBLK, second_pass, 0)


def _ret_call(a_f, a_b, qf, qb, kf, kb, rv, need_ctx):
    n_out = TOK if need_ctx else SEQ
    tok_spec = pl.BlockSpec((1, TOK, 512), lambda b: (b, 0, 0))
    a_spec = _const_spec((RET_HEADS, 8, LANES))
    return pl.pallas_call(
        functools.partial(_ret_kernel, need_ctx=need_ctx),
        grid=(BATCH,),
        in_specs=[a_spec, a_spec, tok_spec, tok_spec, tok_spec, tok_spec, tok_spec],
        out_specs=pl.BlockSpec((1, n_out, 512), lambda b: (b, 0, 0)),
        out_shape=jax.ShapeDtypeStruct((BATCH, n_out, 512), BF16),
        scratch_shapes=[pltpu.VMEM((TOK, 512), F32),
                        pltpu.VMEM((2 * RET_HEADS, RET_DK, LANES), F32),
                        pltpu.VMEM((2 * RET_HEADS, 4, BLOCK, BLOCK), F32)],
        compiler_params=_params("parallel"),
        name="retention",
    )(a_f, a_b, qf, qb, kf, kb, rv)


def _slot_masks():
    lane = lax.broadcasted_iota(jnp.int32, (QB, LANES), 1)
    slot_a = (lane & 32) == 0
    return slot_a, lane < 64


def _head_queries(q_ref, c, use_b, slot_a):
    qv = q_ref[0, :, c * LANES:(c + 1) * LANES].astype(F32)
    qv = jnp.where(slot_a, 0.0, qv) if use_b else jnp.where(slot_a, qv, 0.0)
    return qv.astype(BF16)


def _pv(p, load_v, nkeys):
    acc = None
    for r0 in range(0, nkeys, PV_KC):
        d = jnp.dot(p[:, r0:r0 + PV_KC], load_v(r0, r0 + PV_KC), preferred_element_type=F32)
        acc = d if acc is None else acc + d
    return acc


def _store_heads(o_ref, res, low_half, extra=None):
    for c in range(4):
        ra, rb = res[2 * c], res[2 * c + 1]
        num = jnp.where(low_half, ra, rb)
        den = pltpu.roll(jnp.where(low_half, rb, ra), 64, 1)
        if extra is not None:
            den = den + jnp.where(low_half, extra[2 * c], extra[2 * c + 1])
        o_ref[0, :, c * LANES:(c + 1) * LANES] = (num * (1.0 / den)).astype(o_ref.dtype)


def _attn_specs(off):
    return dict(
        in_specs=[pl.BlockSpec((1, QB, 512), lambda b, i: (b, i + off, 0)),
                  pl.BlockSpec((1, QB, 512), lambda b, i: (b, jnp.minimum(i + off + 1, NQB - 1), 0)),
                  pl.BlockSpec((1, TOK, KV_W), lambda b, i: (b, 0, 0))],
        out_specs=pl.BlockSpec((1, QB, 512), lambda b, i: (b, i, 0)),
        out_shape=jax.ShapeDtypeStruct((BATCH, (NQB - off) * QB, 512), BF16),
    )


def _ax_kernel(q_ref, qn_ref, kv_ref, o_ref, s_ref, m_ref, *, blk_off):
    slot_a, low_half = _slot_masks()

    def scores(qr, head, nkeys):
        c, use_b, kb, _ = head
        qs = _head_queries(qr, c, use_b, slot_a)
        k = kv_ref[0, 0:nkeys, kb * LANES:(kb + 1) * LANES]
        return lax.dot_general(qs, k, _NT, preferred_element_type=F32)

    def carry(slot, s):
        s_ref[slot] = s
        m_ref[slot] = jnp.broadcast_to(jnp.max(s, axis=-1, keepdims=True), m_ref.shape[1:])

    def finish(s, head, nkeys, mx=None):
        _, use_b, _, vb = head
        if mx is None:
            mx = jnp.max(s, axis=-1, keepdims=True)
        p = jnp.exp2(s - mx).astype(BF16)
        v0 = (vb - use_b) * LANES
        wide = _pv(p, lambda r0, r1: kv_ref[0, r0:r1, v0:v0 + 2 * LANES], nkeys)
        return wide[:, use_b * LANES:(use_b + 1) * LANES]

    nh = len(_ATT_HEADS)

    def prologue(qr):
        for a in range(AX_AHEAD):
            carry(a, scores(qr, _ATT_HEADS[a], TOK))

    def ctx_block():
        res = [finish(scores(q_ref, head, CTX_LEN), head, CTX_LEN) for head in _ATT_HEADS]
        _store_heads(o_ref, res, low_half)
        prologue(qn_ref)

    def latent_block():
        res = []
        ahead = [(s_ref[a], m_ref[a, :, 0:1]) for a in range(AX_AHEAD)]
        for n, head in enumerate(_ATT_HEADS):
            m = n + AX_AHEAD
            ahead.append((scores(q_ref, _ATT_HEADS[m], TOK) if m < nh else scores(qn_ref, _ATT_HEADS[m - nh], TOK), None))
            s, mx = ahead.pop(0)
            res.append(finish(s, head, TOK, mx))
        for a in range(AX_AHEAD):
            carry(a, ahead[a][0])
        _store_heads(o_ref, res, low_half)

    blk = pl.program_id(1)
    if blk_off == 0:
        pl.when(blk == 0)(ctx_block)
        pl.when(blk > 0)(latent_block)
    else:
        pl.when(blk == 0)(lambda: prologue(q_ref))
        latent_block()


def _ax_call(q, kv, need_ctx):
    off = 0 if need_ctx else 1
    specs = _attn_specs(off)
    return pl.pallas_call(
        functools.partial(_ax_kernel, blk_off=off),
        grid=(BATCH, NQB - off),
        scratch_shapes=[pltpu.VMEM((AX_AHEAD, QB, TOK), F32), pltpu.VMEM((AX_AHEAD, QB, LANES), F32)],
        compiler_params=_params("parallel", "arbitrary"),
        name="axial_attn",
        **specs,
    )(q, q, kv)


def _win_kernel(sink_ref, q_ref, qn_ref, kv_ref, o_ref, s_ref, *, blk_off):
    slot_a, low_half = _slot_masks()
    nh = len(_ATT_HEADS)
    nk = CTX_LEN + QB + 2 * BLOCK

    def window_loader(n):
        left = jnp.maximum(2 * n - 1, 0)
        right = jnp.minimum(2 * n + 2, LAT_BLKS - 1)
        s_left = pl.multiple_of(CTX_LEN + left * BLOCK, BLOCK)
        s_mid = pl.multiple_of(CTX_LEN + n * QB, QB)
        s_right = pl.multiple_of(CTX_LEN + right * BLOCK, BLOCK)
        cache = {}

        def load(c0, w):
            if (c0, w) not in cache:
                cache[(c0, w)] = jnp.concatenate([kv_ref[0, 0:CTX_LEN, c0:c0 + w],
                                                  kv_ref[0, pl.ds(s_left, BLOCK), c0:c0 + w],
                                                  kv_ref[0, pl.ds(s_mid, QB), c0:c0 + w],
                                                  kv_ref[0, pl.ds(s_right, BLOCK), c0:c0 + w]], axis=0)
            return cache[(c0, w)]

        return load

    def window_mask(n):
        col = lax.broadcasted_iota(jnp.int32, (QB, nk), 1)
        qi = lax.broadcasted_iota(jnp.int32, (QB, nk), 0)
        jj = col - CTX_LEN
        dlt = jj - qi
        lo = jnp.where(n >= 1, 0, BLOCK)
        hi = jnp.where(n <= SEQ // QB - 2, QB + 2 * BLOCK, QB + BLOCK)
        in_win = (dlt >= 0) & (dlt <= 2 * BLOCK) & (jj >= lo) & (jj < hi)
        return (col < CTX_LEN) | in_win

    def scores(qr, head, load):
        c, use_b, kb, _ = head
        qs = _head_queries(qr, c, use_b, slot_a)
        return lax.dot_general(qs, load(kb * LANES, LANES), _NT, preferred_element_type=F32)

    def finish(s, head, load, valid):
        c, use_b, _, vb = head
        if valid is not None:
            s = jnp.where(valid, s, NEG)
        sk = sink_ref[2 * c + use_b] * LOG2E
        mx = jnp.maximum(jnp.max(s, axis=-1, keepdims=True), sk)
        p = jnp.exp2(s - mx).astype(BF16)
        v = load((vb - use_b) * LANES, 2 * LANES)
        wide = _pv(p, lambda r0, r1: v[r0:r1], v.shape[0])
        return wide[:, use_b * LANES:(use_b + 1) * LANES], jnp.exp2(sk - mx)

    def store(done):
        _store_heads(o_ref, [r for r, _ in done], low_half, [t for _, t in done])

    def prologue(qr, n):
        load = window_loader(n)
        for a in range(WIN_AHEAD):
            s_ref[a] = scores(qr, _ATT_HEADS[a], load)

    def ctx_block():
        def load(c0, w):
            return kv_ref[0, 0:CTX_LEN, c0:c0 + w]

        store([finish(scores(q_ref, head, load), head, load, None) for head in _ATT_HEADS])
        prologue(qn_ref, 0)

    def latent_block(n):
        load = window_loader(n)
        load_next = window_loader(jnp.minimum(n + 1, SEQ // QB - 1))
        valid = window_mask(n)
        ahead = [s_ref[a] for a in range(WIN_AHEAD)]
        done = []
        for i, head in enumerate(_ATT_HEADS):
            m = i + WIN_AHEAD
            ahead.append(scores(q_ref, _ATT_HEADS[m], load) if m < nh else scores(qn_ref, _ATT_HEADS[m - nh], load_next))
            done.append(finish(ahead.pop(0), head, load, valid))
        for a in range(WIN_AHEAD):
            s_ref[a] = ahead[a]
        store(done)

    blk = pl.program_id(1)
    if blk_off == 0:
        pl.when(blk == 0)(ctx_block)
        pl.when(blk > 0)(lambda: latent_block(blk - 1))
    else:
        pl.when(blk == 0)(lambda: prologue(q_ref, 0))
        latent_block(blk)


def _win_call(sink, q, kv, need_ctx):
    off = 0 if need_ctx else 1
    specs = _attn_specs(off)
    specs["in_specs"] = [pl.BlockSpec(memory_space=pltpu.SMEM)] + specs["in_specs"]
    return pl.pallas_call(
        functools.partial(_win_kernel, blk_off=off),
        grid=(BATCH, NQB - off),
        scratch_shapes=[pltpu.VMEM((WIN_AHEAD, QB, CTX_LEN + QB + 2 * BLOCK), F32)],
        compiler_params=_params("parallel", "arbitrary"),
        name="window_attn",
        **specs,
    )(sink, q, q, kv)


def _merge_kernel(*refs, n_x, final):
    x_refs = refs[:n_x]
    (mod_ref, modc_ref, nw_ref, or_ref, ow_ref, oa_ref, win_ref, wpr_ref, wpw_ref, wpa_ref,
     wo_ref, fw_ref, out_ref) = refs[n_x:]
    for sb in range(out_ref.shape[1] // SUB):
        rows = slice(sb * SUB, (sb + 1) * SUB)
        mod = _sub_mod(mod_ref, None if final else modc_ref, sb)
        x = _read_x(x_refs, sb)
        h = _modulated_norm(x, mod, nw_ref)

        def branch(o_ref, wp_ref, j):
            g0 = (_RG, _WG, _AG)[j]
            g = jnp.dot(h, win_ref[:, g0:g0 + 512], preferred_element_type=F32)
            u = (o_ref[0, rows, :].astype(F32) * (g * _sigmoid(g))).astype(BF16)
            b = jnp.dot(u, wp_ref[...], preferred_element_type=F32)
            m0 = _MG + j * D_MODEL
            m = _sigmoid(jnp.dot(h, win_ref[:, m0:m0 + D_MODEL], preferred_element_type=F32))
            return m * b

        mix = branch(or_ref, wpr_ref, 0) + branch(ow_ref, wpw_ref, 1) + branch(oa_ref, wpa_ref, 2)
        y = jnp.dot(mix.astype(BF16), wo_ref[...], preferred_element_type=F32)
        xn = x + mod[:, 2 * D_MODEL:3 * D_MODEL] * y
        if final:
            xn = xn * lax.rsqrt(jnp.mean(xn * xn, axis=-1, keepdims=True) + EPS) * fw_ref[...]
        out_ref[0, rows, :] = xn


def _merge_call(xx, mod_l, nw, o_ret, o_win, o_ax, w_in_l, wpr, wpw, wpa, wo, fw, final):
    tm = CTX_LEN if final else TM
    x_off = CTX_LEN // tm if final else 0
    n_tok = SEQ if final else TOK
    o_spec = pl.BlockSpec((1, tm, 512), lambda b, t: (b, t, 0))
    x_ops, x_specs = _x_operands(xx, tm, x_off)
    return pl.pallas_call(
        functools.partial(_merge_kernel, n_x=len(x_ops), final=final),
        grid=(BATCH, n_tok // tm),
        in_specs=[*x_specs,
                  *_mod_specs(),
                  _const_spec((1, D_MODEL)),
                  o_spec, o_spec, o_spec,
                  _const_spec((D_MODEL, IN_W)),
                  _const_spec((512, D_MODEL)), _const_spec((512, D_MODEL)), _const_spec((512, D_MODEL)),
                  _const_spec((D_MODEL, D_MODEL)),
                  _const_spec((1, D_MODEL))],
        out_specs=pl.BlockSpec((1, tm, D_MODEL), lambda b, t: (b, t, 0)),
        out_shape=jax.ShapeDtypeStruct((BATCH, n_tok, D_MODEL), F32),
        compiler_params=_params("parallel", "arbitrary"),
        name="merge_final" if final else "merge",
    )(*x_ops, mod_l, mod_l, nw, o_ret, o_win, o_ax, w_in_l, wpr, wpw, wpa, wo, fw)


def _rope_tables():
    t = jnp.arange(TOK, dtype=F32)
    theta = ROPE_BASE ** (-jnp.linspace(0.0, 1.0, RET_DK // 2, dtype=F32))
    pos_b = jnp.where(t < CTX_LEN, CTX_LEN - 1.0 - t, 2.0 * CTX_LEN + SEQ - 1.0 - t)

    def ret_pair(pos):
        ang = pos[:, None] * theta[None]
        c, s = jnp.cos(ang), jnp.sin(ang)
        return jnp.concatenate([c, c], axis=-1), jnp.concatenate([-s, s], axis=-1)

    cf, sf = ret_pair(t)
    cb, sb = ret_pair(pos_b)

    s_idx = jnp.arange(SEQ)
    quarter = HEAD_DIM // 4
    freqs = ROPE_BASE ** (-jnp.arange(quarter, dtype=F32) / quarter)
    r = (s_idx // GRID_W).astype(F32)
    col = (s_idx % GRID_W).astype(F32)
    ang = jnp.concatenate([r[:, None] * freqs[None], col[:, None] * freqs[None]], axis=-1)
    c, s = jnp.cos(ang), jnp.sin(ang)
    ca = jnp.concatenate([jnp.ones((CTX_LEN, LANES), F32), jnp.concatenate([c, c, c, c], axis=-1)], axis=0)
    sa = jnp.concatenate([jnp.zeros((CTX_LEN, LANES), F32), jnp.concatenate([-s, -s, s, s], axis=-1)], axis=0)
    return cf, sf, cb, sb, ca, sa


def _segment_matrix():
    slot = (np.arange(LANES) // 32) % 2
    return jnp.asarray((slot[:, None] == slot[None, :]).astype(np.float32), dtype=BF16)


def kernel(x, c, ctx, c_ctx, norm_w, w_mod, b_mod, w_in, ret_decay_fwd, ret_decay_bwd, win_sink,
           ax_q_gain, ax_k_gain, w_proj_ret, w_proj_win, w_proj_ax, w_out, final_norm_w):
    xx = (ctx, x)
    cvec = jnp.zeros((24, D_MODEL), F32).at[:BATCH].set(c).at[BATCH].set(c_ctx)
    mod = _mod_call(cvec, w_mod, b_mod)
    tabs = _rope_tables()
    seg = _segment_matrix()
    w_in_bf = w_in.astype(BF16)
    wqkv = jnp.take(w_in_bf, jnp.asarray(_QKV_COLS), axis=2)
    gain_lanes = jnp.asarray(_GAIN_LANES)
    fw = final_norm_w.reshape(1, D_MODEL)

    out = None
    for l in range(DEPTH):
        need_ctx = l < DEPTH - 1
        mod_l = mod[l].reshape(24, 1, 3 * D_MODEL)
        nw = norm_w[l].reshape(1, D_MODEL)
        gq = ax_q_gain[l][gain_lanes].reshape(1, LANES)
        gk = ax_k_gain[l][gain_lanes].reshape(1, LANES)
        qf, qb, kf, kb, rv, wq, wkv, aq, akv = _qkv_call(xx, mod_l, nw, wqkv[l], tabs, gq, gk, seg)
        a_f = jnp.broadcast_to(ret_decay_fwd[l][:, None, None], (RET_HEADS, 8, LANES))
        a_b = jnp.broadcast_to(ret_decay_bwd[l][:, None, None], (RET_HEADS, 8, LANES))
        o_ret = _ret_call(a_f, a_b, qf, qb, kf, kb, rv, need_ctx)
        o_win = _win_call(win_sink[l], wq, wkv, need_ctx)
        o_ax = _ax_call(aq, akv, need_ctx)
        res = _merge_call(xx, mod_l, nw, o_ret, o_win, o_ax, w_in_bf[l],
                          w_proj_ret[l].astype(BF16), w_proj_win[l].astype(BF16), w_proj_ax[l].astype(BF16),
                          w_out[l].astype(BF16), fw, final=not need_ctx)
        if need_ctx:
            xx = res
        else:
            out = res
    return out
```

```python
import functools

import numpy as np
import jax
import jax.numpy as jnp
from jax import lax
from jax.experimental import pallas as pl
from jax.experimental.pallas import tpu as pltpu

D_MODEL = 1024
BATCH = 16
SEQ = 2048
DEPTH = 4
CTX_LEN = 256
TOK = CTX_LEN + SEQ
GRID_W = 64
BLOCK = 128
RET_HEADS = 4
RET_DK = 128
HEAD_DIM = 64
ROPE_BASE = 10000.0
EPS = 1e-6
NEG = -1e30

LANES = 128
NBLK = TOK // BLOCK
CTX_BLKS = CTX_LEN // BLOCK
LAT_BLKS = SEQ // BLOCK
LOG2E = 1.4426950408889634
KV_W = 6 * LANES
QB = CTX_LEN
NQB = TOK // QB
AX_AHEAD = 2
WIN_AHEAD = 2
PV_KC = 256

SUB = CTX_LEN
TM = 3 * SUB
TM_FINAL = 4 * SUB
VMEM_LIMIT = 52 * 1024 * 1024

F32 = jnp.float32
BF16 = jnp.bfloat16

_IN_SIZES = (512, 512, 512, 512, 512, 128, 128, 512, 512, 128, 128, 512, 3 * D_MODEL)
_OFF = np.concatenate([[0], np.cumsum(_IN_SIZES)]).astype(np.int64)
(_RQ, _RK, _RV, _RG, _WQ, _WK, _WV, _WG, _AQ, _AK, _AV, _AG, _MG) = [int(o) for o in _OFF[:-1]]

_EV32 = np.arange(0, 64, 2)
_OD32 = np.arange(1, 64, 2)
NQKV = 3 * 512 + 2 * (512 + 4 * LANES)
IN_W = int(_OFF[-1])
_GAIN_LANES = np.concatenate([_EV32, _EV32, _OD32, _OD32]).astype(np.int32)


def _qkv_weights(w):
    n_l, d = w.shape[:2]

    def seg(c0, n):
        return w[:, :, c0:c0 + n]

    def ret_qk(x):
        return x.reshape(n_l, d, RET_HEADS, RET_DK // 2, 2).swapaxes(-1, -2).reshape(n_l, d, 512)

    def att_q(x):
        return x.reshape(n_l, d, 4, 2, 32, 2).transpose(0, 1, 2, 5, 3, 4).reshape(n_l, d, 512)

    def att_k(x, swapped):
        y = x.reshape(n_l, d, 2, 32, 2)
        if swapped:
            y = y[:, :, ::-1]
        return y.transpose(0, 1, 4, 2, 3).reshape(n_l, d, LANES)

    def attention(q0, k0, v0):
        k, v = seg(k0, LANES), seg(v0, LANES)
        return [att_q(seg(q0, 512)), att_k(k, False), att_k(k, True), v,
                jnp.concatenate([v[..., 64:], v[..., :64]], axis=-1)]

    return jnp.concatenate([ret_qk(seg(_RQ, 512)), ret_qk(seg(_RK, 512)), seg(_RV, 512)]
                           + attention(_WQ, _WK, _WV) + attention(_AQ, _AK, _AV), axis=-1)

_ATT_HEADS = tuple((c, b, (c // 2 + b) % 2, 2 + 2 * (c // 2) + b) for c in range(4) for b in (0, 1))

_NT = (((1,), (1,)), ((), ()))
_TN = (((0,), (0,)), ((), ()))


def _const_spec(shape):
    nd = len(shape)
    return pl.BlockSpec(shape, lambda *_: (0,) * nd, pipeline_mode=pl.Buffered(1))


def _layer_spec(shape, layer):
    nd = len(shape)
    return pl.BlockSpec((None, *shape), lambda *_: (layer,) + (0,) * nd, pipeline_mode=pl.Buffered(1))


def _params(*sem):
    return pltpu.CompilerParams(dimension_semantics=sem, vmem_limit_bytes=VMEM_LIMIT)


def _sigmoid(z):
    return 1.0 / (1.0 + jnp.exp(-z))


def _mod_kernel(c_ref, w_ref, b_ref, o_ref):
    cv = c_ref[...]
    s = cv * _sigmoid(cv)
    o_ref[0] = jnp.dot(s, w_ref[0], preferred_element_type=F32,
                       precision=lax.Precision.HIGHEST) + b_ref[0]


def _mod_call(cvec, w_mod, b_mod):
    rows = cvec.shape[0]
    return pl.pallas_call(
        _mod_kernel,
        grid=(DEPTH, 3),
        in_specs=[pl.BlockSpec((rows, D_MODEL), lambda l, j: (0, 0)),
                  pl.BlockSpec((1, D_MODEL, D_MODEL), lambda l, j: (l, 0, j)),
                  pl.BlockSpec((1, 1, D_MODEL), lambda l, j: (l, 0, j))],
        out_specs=pl.BlockSpec((1, rows, D_MODEL), lambda l, j: (l, 0, j)),
        out_shape=jax.ShapeDtypeStruct((DEPTH, rows, 3 * D_MODEL), F32),
        compiler_params=_params("arbitrary", "arbitrary"),
        name="adaln_mod",
    )(cvec, w_mod, b_mod.reshape(DEPTH, 1, 3 * D_MODEL))


def _sub_mod(mod_ref, modc_ref, j):
    if modc_ref is None or j > 0:
        return mod_ref[0]
    return jnp.where(pl.program_id(1) == 0, modc_ref[0], mod_ref[0])


def _x_operands(xx, tm, latent_only=False):
    nsub = tm // SUB
    if not isinstance(xx, tuple):
        if not latent_only:
            return [xx], [pl.BlockSpec((1, tm, D_MODEL), lambda b, t: (b, t, 0))], False
        specs = [pl.BlockSpec((1, SUB, D_MODEL), lambda b, t, sb=sb: (b, nsub * t + sb + 1, 0)) for sb in range(nsub)]
        return [xx] * nsub, specs, False
    ctx, x = xx
    specs = [pl.BlockSpec((1, SUB, D_MODEL), lambda b, t: (b, 0, 0))]
    for sb in range(nsub):
        specs.append(pl.BlockSpec((1, SUB, D_MODEL), lambda b, t, sb=sb: (b, jnp.maximum(nsub * t + sb - 1, 0), 0)))
    return [ctx] + [x] * nsub, specs, True


def _read_x(x_refs, sb, has_ctx):
    if len(x_refs) == 1:
        return x_refs[0][0, sb * SUB:(sb + 1) * SUB, :]
    if not has_ctx:
        return x_refs[sb][0]
    if sb == 0:
        return jnp.where(pl.program_id(1) == 0, x_refs[0][0], x_refs[1][0])
    return x_refs[1 + sb][0]


def _modulated_norm(x, mod, nw_ref):
    ms = jnp.mean(x * x, axis=-1, keepdims=True)
    y = x * lax.rsqrt(ms + EPS) * nw_ref[...]
    return (y * (1.0 + mod[:, D_MODEL:2 * D_MODEL]) + mod[:, 0:D_MODEL]).astype(BF16)


def _qkv_kernel(*refs, n_x, has_ctx):
    x_refs = refs[:n_x]
    (mod_ref, modc_ref, nw_ref, w_ref, cf_ref, sf_ref, cb_ref, sb_ref, ca_ref, sa_ref, gq_ref, gk_ref, seg_ref,
     qf_ref, qb_ref, kf_ref, kb_ref, rv_ref, wq_ref, wkv_ref, aq_ref, akv_ref) = refs[n_x:]
    k_scale = RET_DK ** -0.5
    q_scale = HEAD_DIM ** -0.5 * LOG2E
    low_half = lax.broadcasted_iota(jnp.int32, (SUB, LANES), 1) < 64

    def lanes(a, j):
        return a[:, j * LANES:(j + 1) * LANES]

    def qk_norm(v, g_ref):
        ss = jnp.dot((v * v).astype(BF16), seg_ref[...], preferred_element_type=F32)
        return v * lax.rsqrt(ss * (1.0 / HEAD_DIM) + EPS) * g_ref[...]

    for sb in range(qf_ref.shape[1] // SUB):
        rows = slice(sb * SUB, (sb + 1) * SUB)
        h = _modulated_norm(_read_x(x_refs, sb, has_ctx), _sub_mod(mod_ref, modc_ref, sb), nw_ref)

        def proj(c0):
            return jnp.dot(h, w_ref[:, c0:c0 + 512], preferred_element_type=F32)

        def rope(v, c_ref, s_ref):
            return v * c_ref[rows, :] + pltpu.roll(v, 64, 1) * s_ref[rows, :]

        def put(ref, j, val):
            ref[0, rows, j * LANES:(j + 1) * LANES] = val.astype(BF16)

        def store_kv(kv_ref, acc):
            va, vb = lanes(acc, 2), lanes(acc, 3)
            put(kv_ref, 2, jnp.where(low_half, va, 1.0))
            put(kv_ref, 3, jnp.where(low_half, 1.0, vb))
            put(kv_ref, 4, jnp.where(low_half, vb, 1.0))
            put(kv_ref, 5, jnp.where(low_half, 1.0, va))

        acc = proj(0)
        for j in range(4):
            put(qf_ref, j, rope(lanes(acc, j), cf_ref, sf_ref))
            put(qb_ref, j, rope(lanes(acc, j), cb_ref, sb_ref))
        acc = proj(512)
        for j in range(4):
            put(kf_ref, j, rope(lanes(acc, j), cf_ref, sf_ref) * k_scale)
            put(kb_ref, j, rope(lanes(acc, j), cb_ref, sb_ref) * k_scale)
        rv_ref[0, rows, :] = proj(1024).astype(BF16)

        acc = proj(1536)
        for j in range(4):
            put(wq_ref, j, rope(lanes(acc, j), ca_ref, sa_ref) * q_scale)
        acc = proj(2048)
        for j in range(2):
            put(wkv_ref, j, rope(lanes(acc, j), ca_ref, sa_ref))
        store_kv(wkv_ref, acc)

        acc = proj(2560)
        for j in range(4):
            put(aq_ref, j, rope(qk_norm(lanes(acc, j), gq_ref), ca_ref, sa_ref) * q_scale)
        acc = proj(3072)
        for j in range(2):
            put(akv_ref, j, rope(qk_norm(lanes(acc, j), gk_ref), ca_ref, sa_ref))
        store_kv(akv_ref, acc)


def _mod_specs(layer):
    return [pl.BlockSpec((None, 1, 1, 3 * D_MODEL), lambda b, t: (layer, b, 0, 0)),
            pl.BlockSpec((None, 1, 1, 3 * D_MODEL), lambda b, t: (layer, BATCH, 0, 0))]


def _qkv_call(layer, xx, mod, nw, wqkv, tabs, gq, gk, seg):
    tok_spec = pl.BlockSpec((1, TM, 512), lambda b, t: (b, t, 0))
    tab_spec = pl.BlockSpec((TM, LANES), lambda b, t: (t, 0))
    out_sds = jax.ShapeDtypeStruct((BATCH, TOK, 512), BF16)
    kv_spec = pl.BlockSpec((1, TM, KV_W), lambda b, t: (b, t, 0))
    kv_sds = jax.ShapeDtypeStruct((BATCH, TOK, KV_W), BF16)
    x_ops, x_specs, has_ctx = _x_operands(xx, TM)
    return pl.pallas_call(
        functools.partial(_qkv_kernel, n_x=len(x_ops), has_ctx=has_ctx),
        grid=(BATCH, TOK // TM),
        in_specs=[*x_specs,
                  *_mod_specs(layer),
                  _layer_spec((1, D_MODEL), layer),
                  _layer_spec((D_MODEL, NQKV), layer),
                  tab_spec, tab_spec, tab_spec, tab_spec, tab_spec, tab_spec,
                  _layer_spec((1, LANES), layer), _layer_spec((1, LANES), layer), _const_spec((LANES, LANES))],
        out_specs=[tok_spec] * 6 + [kv_spec, tok_spec, kv_spec],
        out_shape=[out_sds] * 6 + [kv_sds, out_sds, kv_sds],
        compiler_params=_params("parallel", "arbitrary"),
        name="qkv_proj",
    )(*x_ops, mod, mod, nw, wqkv, *tabs, gq, gk, seg)


def _log_sigmoid(a):
    return jnp.minimum(a, 0.0) - jnp.log1p(jnp.exp(-jnp.abs(a)))


def _ret_kernel(af_ref, ab_ref, qf_ref, qb_ref, kf_ref, kb_ref, v_ref, o_ref, acc_ref, st_ref, dec_ref,
                *, need_ctx):
    ri = lax.broadcasted_iota(jnp.int32, (BLOCK, BLOCK), 0).astype(F32)
    ci = lax.broadcasted_iota(jnp.int32, (BLOCK, BLOCK), 1).astype(F32)
    d = ri - ci
    for h in range(RET_HEADS):
        lgf = _log_sigmoid(af_ref[h])[0:1, :]
        lgb = _log_sigmoid(ab_ref[h])[0:1, :]
        dec_ref[2 * h, 0] = jnp.where(d >= 0, jnp.exp(jnp.maximum(d, 0.0) * lgf), 0.0)
        dec_ref[2 * h, 1] = jnp.exp((ri + 1.0) * lgf)
        dec_ref[2 * h, 2] = jnp.exp((BLOCK - 1.0 - ri) * lgf)
        dec_ref[2 * h, 3] = jnp.exp(0.0 * ri + float(BLOCK) * lgf)
        dec_ref[2 * h + 1, 0] = jnp.where(d < 0, jnp.exp(jnp.maximum(-d, 0.0) * lgb), 0.0)
        dec_ref[2 * h + 1, 1] = jnp.exp((float(BLOCK) - ri) * lgb)
        dec_ref[2 * h + 1, 2] = jnp.exp(ri * lgb)
        dec_ref[2 * h + 1, 3] = jnp.exp(0.0 * ri + float(BLOCK) * lgb)
    st_ref[...] = jnp.zeros_like(st_ref)
    first_row = 0 if need_ctx else CTX_LEN

    def visit(i, in_ctx, first, want_o):
        cf = i
        cb = (CTX_BLKS - 1 - i) if in_ctx else (NBLK + CTX_BLKS - 1 - i)
        pending = []
        for h in range(RET_HEADS):
            lanes_h = slice(h * LANES, (h + 1) * LANES)
            for dr, (q_ref, k_ref, c) in enumerate(((qf_ref, kf_ref, cf), (qb_ref, kb_ref, cb))):
                r0 = c * BLOCK if in_ctx else pl.multiple_of(c * BLOCK, BLOCK)
                j = 2 * h + dr
                k = k_ref[0, pl.ds(r0, BLOCK), lanes_h]
                v = v_ref[0, pl.ds(r0, BLOCK), lanes_h]
                state = st_ref[j]
                kd = (k.astype(F32) * dec_ref[j, 2]).astype(BF16)
                st_ref[j] = dec_ref[j, 3] * state + lax.dot_general(kd, v, _TN, preferred_element_type=F32)
                if want_o:
                    q = q_ref[0, pl.ds(r0, BLOCK), lanes_h]
                    s = lax.dot_general(q, k, _NT, preferred_element_type=F32)
                    cross = jnp.dot(q, state.astype(BF16), preferred_element_type=F32)
                    pending.append((j, r0, lanes_h, v, s, cross))
        for j, r0, lanes_h, v, s, cross in pending:
            o = jnp.dot((s * dec_ref[j, 0]).astype(BF16), v, preferred_element_type=F32) + cross * dec_ref[j, 1]
            if first:
                acc_ref[pl.ds(r0, BLOCK), lanes_h] = o
            else:
                tot = acc_ref[pl.ds(r0, BLOCK), lanes_h] + o
                tot = tot * lax.rsqrt(jnp.mean(tot * tot, axis=-1, keepdims=True) + EPS)
                o_ref[0, pl.ds(r0 - first_row, BLOCK), lanes_h] = tot.astype(o_ref.dtype)

    for i in range(CTX_BLKS):
        visit(i, True, i < CTX_BLKS // 2, need_ctx)
    half = (NBLK + CTX_BLKS) // 2

    def first_pass(i, carry):
        visit(i, False, True, True)
        return carry

    def second_pass(i, carry):
        visit(i, False, False, True)
        return carry

    lax.fori_loop(CTX_BLKS, half, first_pass, 0)
    lax.fori_loop(half, NBLK, second_pass, 0)


def _ret_call(layer, a_f, a_b, qf, qb, kf, kb, rv, need_ctx):
    n_out = TOK if need_ctx else SEQ
    tok_spec = pl.BlockSpec((1, TOK, 512), lambda b: (b, 0, 0))
    a_spec = _layer_spec((RET_HEADS, 8, LANES), layer)
    return pl.pallas_call(
        functools.partial(_ret_kernel, need_ctx=need_ctx),
        grid=(BATCH,),
        in_specs=[a_spec, a_spec, tok_spec, tok_spec, tok_spec, tok_spec, tok_spec],
        out_specs=pl.BlockSpec((1, n_out, 512), lambda b: (b, 0, 0)),
        out_shape=jax.ShapeDtypeStruct((BATCH, n_out, 512), BF16),
        scratch_shapes=[pltpu.VMEM((TOK, 512), F32),
                        pltpu.VMEM((2 * RET_HEADS, RET_DK, LANES), F32),
                        pltpu.VMEM((2 * RET_HEADS, 4, BLOCK, BLOCK), F32)],
        compiler_params=_params("parallel"),
        name="retention",
    )(a_f, a_b, qf, qb, kf, kb, rv)


def _slot_masks():
    lane = lax.broadcasted_iota(jnp.int32, (QB, LANES), 1)
    slot_a = (lane & 32) == 0
    return slot_a, lane < 64


def _head_queries(q_ref, c, use_b, slot_a):
    qv = q_ref[0, :, c * LANES:(c + 1) * LANES].astype(F32)
    qv = jnp.where(slot_a, 0.0, qv) if use_b else jnp.where(slot_a, qv, 0.0)
    return qv.astype(BF16)


def _pv(p, load_v, nkeys):
    acc = None
    for r0 in range(0, nkeys, PV_KC):
        d = jnp.dot(p[:, r0:r0 + PV_KC], load_v(r0, r0 + PV_KC), preferred_element_type=F32)
        acc = d if acc is None else acc + d
    return acc


def _store_heads(o_ref, res, low_half, extra=None):
    for c in range(4):
        ra, rb = res[2 * c], res[2 * c + 1]
        num = jnp.where(low_half, ra[:, :LANES], rb[:, LANES:])
        den = jnp.where(low_half, ra[:, LANES:], rb[:, :LANES])
        if extra is not None:
            den = den + jnp.where(low_half, extra[2 * c], extra[2 * c + 1])
        o_ref[0, :, c * LANES:(c + 1) * LANES] = (num * (1.0 / den)).astype(o_ref.dtype)


def _attn_specs(off):
    return dict(
        in_specs=[pl.BlockSpec((1, QB, 512), lambda b, i: (b, i + off, 0)),
                  pl.BlockSpec((1, QB, 512), lambda b, i: (b, jnp.minimum(i + off + 1, NQB - 1), 0)),
                  pl.BlockSpec((1, TOK, KV_W), lambda b, i: (b, 0, 0))],
        out_specs=pl.BlockSpec((1, QB, 512), lambda b, i: (b, i, 0)),
        out_shape=jax.ShapeDtypeStruct((BATCH, (NQB - off) * QB, 512), BF16),
    )


def _ax_kernel(q_ref, qn_ref, kv_ref, o_ref, s_ref, m_ref, *, blk_off):
    slot_a, low_half = _slot_masks()

    def scores(qr, head, nkeys):
        c, use_b, kb, _ = head
        qs = _head_queries(qr, c, use_b, slot_a)
        k = kv_ref[0, 0:nkeys, kb * LANES:(kb + 1) * LANES]
        return lax.dot_general(qs, k, _NT, preferred_element_type=F32)

    def carry(slot, s):
        s_ref[slot] = s
        m_ref[slot] = jnp.broadcast_to(jnp.max(s, axis=-1, keepdims=True), m_ref.shape[1:])

    def finish(s, head, nkeys, mx=None):
        _, use_b, _, vb = head
        if mx is None:
            mx = jnp.max(s, axis=-1, keepdims=True)
        p = jnp.exp2(s - mx).astype(BF16)
        v0 = (vb - use_b) * LANES
        return _pv(p, lambda r0, r1: kv_ref[0, r0:r1, v0:v0 + 2 * LANES], nkeys)

    nh = len(_ATT_HEADS)

    def prologue(qr):
        for a in range(AX_AHEAD):
            carry(a, scores(qr, _ATT_HEADS[a], TOK))

    def ctx_block():
        res = [finish(scores(q_ref, head, CTX_LEN), head, CTX_LEN) for head in _ATT_HEADS]
        _store_heads(o_ref, res, low_half)
        prologue(qn_ref)

    def latent_block():
        res = []
        ahead = [(s_ref[a], m_ref[a, :, 0:1]) for a in range(AX_AHEAD)]
        for n, head in enumerate(_ATT_HEADS):
            m = n + AX_AHEAD
            ahead.append((scores(q_ref, _ATT_HEADS[m], TOK) if m < nh else scores(qn_ref, _ATT_HEADS[m - nh], TOK), None))
            s, mx = ahead.pop(0)
            res.append(finish(s, head, TOK, mx))
        for a in range(AX_AHEAD):
            carry(a, ahead[a][0])
        _store_heads(o_ref, res, low_half)

    blk = pl.program_id(1)
    if blk_off == 0:
        pl.when(blk == 0)(ctx_block)
        pl.when(blk > 0)(latent_block)
    else:
        pl.when(blk == 0)(lambda: prologue(q_ref))
        latent_block()


def _ax_call(q, kv, need_ctx):
    off = 0 if need_ctx else 1
    specs = _attn_specs(off)
    return pl.pallas_call(
        functools.partial(_ax_kernel, blk_off=off),
        grid=(BATCH, NQB - off),
        scratch_shapes=[pltpu.VMEM((AX_AHEAD, QB, TOK), F32), pltpu.VMEM((AX_AHEAD, QB, LANES), F32)],
        compiler_params=_params("parallel", "arbitrary"),
        name="axial_attn",
        **specs,
    )(q, q, kv)


def _win_kernel(sink_ref, q_ref, qn_ref, kv_ref, o_ref, s_ref, *, layer, blk_off):
    slot_a, low_half = _slot_masks()
    nh = len(_ATT_HEADS)
    nk = CTX_LEN + QB + 2 * BLOCK

    def window_loader(n):
        left = jnp.maximum(2 * n - 1, 0)
        right = jnp.minimum(2 * n + 2, LAT_BLKS - 1)
        s_left = pl.multiple_of(CTX_LEN + left * BLOCK, BLOCK)
        s_mid = pl.multiple_of(CTX_LEN + n * QB, QB)
        s_right = pl.multiple_of(CTX_LEN + right * BLOCK, BLOCK)
        cache = {}

        def load(c0, w):
            if (c0, w) not in cache:
                cache[(c0, w)] = jnp.concatenate([kv_ref[0, 0:CTX_LEN, c0:c0 + w],
                                                  kv_ref[0, pl.ds(s_left, BLOCK), c0:c0 + w],
                                                  kv_ref[0, pl.ds(s_mid, QB), c0:c0 + w],
                                                  kv_ref[0, pl.ds(s_right, BLOCK), c0:c0 + w]], axis=0)
            return cache[(c0, w)]

        return load

    def window_mask(n):
        col = lax.broadcasted_iota(jnp.int32, (QB, nk), 1)
        qi = lax.broadcasted_iota(jnp.int32, (QB, nk), 0)
        jj = col - CTX_LEN
        dlt = jj - qi
        lo = jnp.where(n >= 1, 0, BLOCK)
        hi = jnp.where(n <= SEQ // QB - 2, QB + 2 * BLOCK, QB + BLOCK)
        in_win = (dlt >= 0) & (dlt <= 2 * BLOCK) & (jj >= lo) & (jj < hi)
        return (col < CTX_LEN) | in_win

    def scores(qr, head, load):
        c, use_b, kb, _ = head
        qs = _head_queries(qr, c, use_b, slot_a)
        return lax.dot_general(qs, load(kb * LANES, LANES), _NT, preferred_element_type=F32)

    def finish(s, head, load, valid):
        c, use_b, _, vb = head
        if valid is not None:
            s = jnp.where(valid, s, NEG)
        sk = sink_ref[layer, 2 * c + use_b] * LOG2E
        mx = jnp.maximum(jnp.max(s, axis=-1, keepdims=True), sk)
        p = jnp.exp2(s - mx).astype(BF16)
        v = load((vb - use_b) * LANES, 2 * LANES)
        return _pv(p, lambda r0, r1: v[r0:r1], v.shape[0]), jnp.exp2(sk - mx)

    def store(done):
        _store_heads(o_ref, [r for r, _ in done], low_half, [t for _, t in done])

    def prologue(qr, n):
        load = window_loader(n)
        for a in range(WIN_AHEAD):
            s_ref[a] = scores(qr, _ATT_HEADS[a], load)

    def ctx_block():
        def load(c0, w):
            return kv_ref[0, 0:CTX_LEN, c0:c0 + w]

        store([finish(scores(q_ref, head, load), head, load, None) for head in _ATT_HEADS])
        prologue(qn_ref, 0)

    def latent_block(n):
        load = window_loader(n)
        load_next = window_loader(jnp.minimum(n + 1, SEQ // QB - 1))
        valid = window_mask(n)
        ahead = [s_ref[a] for a in range(WIN_AHEAD)]
        done = []
        for i, head in enumerate(_ATT_HEADS):
            m = i + WIN_AHEAD
            ahead.append(scores(q_ref, _ATT_HEADS[m], load) if m < nh else scores(qn_ref, _ATT_HEADS[m - nh], load_next))
            done.append(finish(ahead.pop(0), head, load, valid))
        for a in range(WIN_AHEAD):
            s_ref[a] = ahead[a]
        store(done)

    blk = pl.program_id(1)
    if blk_off == 0:
        pl.when(blk == 0)(ctx_block)
        pl.when(blk > 0)(lambda: latent_block(blk - 1))
    else:
        pl.when(blk == 0)(lambda: prologue(q_ref, 0))
        latent_block(blk)


def _win_call(layer, sink, q, kv, need_ctx):
    off = 0 if need_ctx else 1
    specs = _attn_specs(off)
    specs["in_specs"] = [pl.BlockSpec(memory_space=pltpu.SMEM)] + specs["in_specs"]
    return pl.pallas_call(
        functools.partial(_win_kernel, layer=layer, blk_off=off),
        grid=(BATCH, NQB - off),
        scratch_shapes=[pltpu.VMEM((WIN_AHEAD, QB, CTX_LEN + QB + 2 * BLOCK), F32)],
        compiler_params=_params("parallel", "arbitrary"),
        name="window_attn",
        **specs,
    )(sink, q, q, kv)


def _merge_kernel(*refs, n_x, has_ctx, final):
    x_refs = refs[:n_x]
    (mod_ref, modc_ref, nw_ref, or_ref, ow_ref, oa_ref, win_ref, wpr_ref, wpw_ref, wpa_ref,
     wo_ref, fw_ref, out_ref) = refs[n_x:]
    nsub = out_ref.shape[1] // SUB
    branches = ((or_ref, wpr_ref, _RG), (ow_ref, wpw_ref, _WG), (oa_ref, wpa_ref, _AG))

    def from_h(sb):
        mod = _sub_mod(mod_ref, None if final else modc_ref, sb)
        x = _read_x(x_refs, sb, has_ctx)
        h = _modulated_norm(x, mod, nw_ref)
        gates = [jnp.dot(h, win_ref[:, g0:g0 + 512], preferred_element_type=F32) for _, _, g0 in branches]
        logits = [jnp.dot(h, win_ref[:, _MG + j * D_MODEL:_MG + (j + 1) * D_MODEL], preferred_element_type=F32)
                  for j in range(3)]
        return x, mod, gates, logits

    def finish(sb, x, mod, gates, logits):
        rows = slice(sb * SUB, (sb + 1) * SUB)
        mix = None
        for (o_ref, wp_ref, _), g, ml in zip(branches, gates, logits):
            u = (o_ref[0, rows, :].astype(F32) * (g * _sigmoid(g))).astype(BF16)
            t = _sigmoid(ml) * jnp.dot(u, wp_ref[...], preferred_element_type=F32)
            mix = t if mix is None else mix + t
        y = jnp.dot(mix.astype(BF16), wo_ref[...], preferred_element_type=F32)
        xn = x + mod[:, 2 * D_MODEL:3 * D_MODEL] * y
        if final:
            xn = xn * lax.rsqrt(jnp.mean(xn * xn, axis=-1, keepdims=True) + EPS) * fw_ref[...]
        out_ref[0, rows, :] = xn

    cur = from_h(0)
    for sb in range(nsub):
        nxt = from_h(sb + 1) if sb + 1 < nsub else None
        finish(sb, *cur)
        cur = nxt


def _merge_call(layer, xx, mod, nw, o_ret, o_win, o_ax, w_in_bf, wpr, wpw, wpa, wo, fw, final):
    tm = TM_FINAL if final else TM
    n_tok = SEQ if final else TOK
    o_spec = pl.BlockSpec((1, tm, 512), lambda b, t: (b, t, 0))
    x_ops, x_specs, has_ctx = _x_operands(xx, tm, latent_only=final)
    return pl.pallas_call(
        functools.partial(_merge_kernel, n_x=len(x_ops), has_ctx=has_ctx, final=final),
        grid=(BATCH, n_tok // tm),
        in_specs=[*x_specs,
                  *_mod_specs(layer),
                  _layer_spec((1, D_MODEL), layer),
                  o_spec, o_spec, o_spec,
                  _layer_spec((D_MODEL, IN_W), layer),
                  _layer_spec((512, D_MODEL), layer), _layer_spec((512, D_MODEL), layer),
                  _layer_spec((512, D_MODEL), layer),
                  _layer_spec((D_MODEL, D_MODEL), layer),
                  _const_spec((1, D_MODEL))],
        out_specs=pl.BlockSpec((1, tm, D_MODEL), lambda b, t: (b, t, 0)),
        out_shape=jax.ShapeDtypeStruct((BATCH, n_tok, D_MODEL), F32),
        compiler_params=_params("parallel", "arbitrary"),
        name="merge_final" if final else "merge",
    )(*x_ops, mod, mod, nw, o_ret, o_win, o_ax, w_in_bf, wpr, wpw, wpa, wo, fw)


def _rope_tables():
    t = jnp.arange(TOK, dtype=F32)
    theta = ROPE_BASE ** (-jnp.linspace(0.0, 1.0, RET_DK // 2, dtype=F32))
    pos_b = jnp.where(t < CTX_LEN, CTX_LEN - 1.0 - t, 2.0 * CTX_LEN + SEQ - 1.0 - t)

    def ret_pair(pos):
        ang = pos[:, None] * theta[None]
        c, s = jnp.cos(ang), jnp.sin(ang)
        return jnp.concatenate([c, c], axis=-1), jnp.concatenate([-s, s], axis=-1)

    cf, sf = ret_pair(t)
    cb, sb = ret_pair(pos_b)

    s_idx = jnp.arange(SEQ)
    quarter = HEAD_DIM // 4
    freqs = ROPE_BASE ** (-jnp.arange(quarter, dtype=F32) / quarter)
    r = (s_idx // GRID_W).astype(F32)
    col = (s_idx % GRID_W).astype(F32)
    ang = jnp.concatenate([r[:, None] * freqs[None], col[:, None] * freqs[None]], axis=-1)
    c, s = jnp.cos(ang), jnp.sin(ang)
    ca = jnp.concatenate([jnp.ones((CTX_LEN, LANES), F32), jnp.concatenate([c, c, c, c], axis=-1)], axis=0)
    sa = jnp.concatenate([jnp.zeros((CTX_LEN, LANES), F32), jnp.concatenate([-s, -s, s, s], axis=-1)], axis=0)
    return cf, sf, cb, sb, ca, sa


def _segment_matrix():
    slot = (np.arange(LANES) // 32) % 2
    return jnp.asarray((slot[:, None] == slot[None, :]).astype(np.float32), dtype=BF16)


def kernel(x, c, ctx, c_ctx, norm_w, w_mod, b_mod, w_in, ret_decay_fwd, ret_decay_bwd, win_sink,
           ax_q_gain, ax_k_gain, w_proj_ret, w_proj_win, w_proj_ax, w_out, final_norm_w):
    xx = (ctx, x)
    cvec = jnp.zeros((24, D_MODEL), F32).at[:BATCH].set(c).at[BATCH].set(c_ctx)
    mod = _mod_call(cvec, w_mod, b_mod).reshape(DEPTH, 24, 1, 3 * D_MODEL)
    tabs = _rope_tables()
    seg = _segment_matrix()
    w_in_bf = w_in.astype(BF16)
    wqkv = _qkv_weights(w_in_bf)
    wpr, wpw, wpa, wo = (w.astype(BF16) for w in (w_proj_ret, w_proj_win, w_proj_ax, w_out))
    nw = norm_w.reshape(DEPTH, 1, D_MODEL)
    gain_lanes = jnp.asarray(_GAIN_LANES)
    gq = ax_q_gain[:, gain_lanes].reshape(DEPTH, 1, LANES)
    gk = ax_k_gain[:, gain_lanes].reshape(DEPTH, 1, LANES)
    a_f = jnp.broadcast_to(ret_decay_fwd[:, :, None, None], (DEPTH, RET_HEADS, 8, LANES))
    a_b = jnp.broadcast_to(ret_decay_bwd[:, :, None, None], (DEPTH, RET_HEADS, 8, LANES))
    fw = final_norm_w.reshape(1, D_MODEL)

    for l in range(DEPTH):
        need_ctx = l < DEPTH - 1
        qf, qb, kf, kb, rv, wq, wkv, aq, akv = _qkv_call(l, xx, mod, nw, wqkv, tabs, gq, gk, seg)
        o_ret = _ret_call(l, a_f, a_b, qf, qb, kf, kb, rv, need_ctx)
        o_win = _win_call(l, win_sink, wq, wkv, need_ctx)
        o_ax = _ax_call(aq, akv, need_ctx)
        xx = _merge_call(l, xx, mod, nw, o_ret, o_win, o_ax, w_in_bf, wpr, wpw, wpa, wo, fw, final=not need_ctx)
    return xx
```

```python
import functools

import numpy as np
import jax
import jax.numpy as jnp
from jax import lax
from jax.experimental import pallas as pl
from jax.experimental.pallas import tpu as pltpu

D_MODEL = 1024
BATCH = 16
SEQ = 2048
DEPTH = 4
CTX_LEN = 256
TOK = CTX_LEN + SEQ
GRID_W = 64
BLOCK = 128
RET_HEADS = 4
RET_DK = 128
HEAD_DIM = 64
ROPE_BASE = 10000.0
EPS = 1e-6
NEG = -1e30

LANES = 128
NBLK = TOK // BLOCK
CTX_BLKS = CTX_LEN // BLOCK
LAT_BLKS = SEQ // BLOCK
LOG2E = 1.4426950408889634
KV_W = 6 * LANES
QB = CTX_LEN
NQB = TOK // QB
AX_AHEAD = 2
WIN_AHEAD = 2
PV_KC = 256

SUB = CTX_LEN
TM = 3 * SUB
TM_FINAL = 4 * SUB
VMEM_LIMIT = 52 * 1024 * 1024

F32 = jnp.float32
BF16 = jnp.bfloat16

_IN_SIZES = (512, 512, 512, 512, 512, 128, 128, 512, 512, 128, 128, 512, 3 * D_MODEL)
_OFF = np.concatenate([[0], np.cumsum(_IN_SIZES)]).astype(np.int64)
(_RQ, _RK, _RV, _RG, _WQ, _WK, _WV, _WG, _AQ, _AK, _AV, _AG, _MG) = [int(o) for o in _OFF[:-1]]

_EV32 = np.arange(0, 64, 2)
_OD32 = np.arange(1, 64, 2)
IN_W = int(_OFF[-1])
_GAIN_LANES = np.concatenate([_EV32, _EV32, _OD32, _OD32]).astype(np.int32)


def _qkv_weights(w):
    n_l, d = w.shape[:2]

    def seg(c0, n):
        return w[:, :, c0:c0 + n]

    def ret_qk(x):
        return x.reshape(n_l, d, RET_HEADS, RET_DK // 2, 2).swapaxes(-1, -2).reshape(n_l, d, 512)

    def att_q(x):
        return x.reshape(n_l, d, 4, 2, 32, 2).transpose(0, 1, 2, 5, 3, 4).reshape(n_l, d, 512)

    def att_k(x, swapped):
        y = x.reshape(n_l, d, 2, 32, 2)
        if swapped:
            y = y[:, :, ::-1]
        return y.transpose(0, 1, 4, 2, 3).reshape(n_l, d, LANES)

    def attention(q0, k0, v0):
        k, v = seg(k0, LANES), seg(v0, LANES)
        return [att_q(seg(q0, 512)),
                jnp.concatenate([att_k(k, False), att_k(k, True), v, v[..., 64:], v[..., :64]], axis=-1)]

    return (ret_qk(seg(_RQ, 512)), ret_qk(seg(_RK, 512)),
            *attention(_WQ, _WK, _WV), *attention(_AQ, _AK, _AV))

_ATT_HEADS = tuple((c, b, (c // 2 + b) % 2, 2 + 2 * (c // 2) + b) for c in range(4) for b in (0, 1))

_NT = (((1,), (1,)), ((), ()))
_TN = (((0,), (0,)), ((), ()))


def _const_spec(shape):
    nd = len(shape)
    return pl.BlockSpec(shape, lambda *_: (0,) * nd, pipeline_mode=pl.Buffered(1))


def _layer_spec(shape, layer):
    nd = len(shape)
    return pl.BlockSpec((None, *shape), lambda *_: (layer,) + (0,) * nd, pipeline_mode=pl.Buffered(1))


def _params(*sem):
    return pltpu.CompilerParams(dimension_semantics=sem, vmem_limit_bytes=VMEM_LIMIT)


def _sigmoid(z):
    return 1.0 / (1.0 + jnp.exp(-z))


def _mod_kernel(c_ref, w_ref, b_ref, o_ref):
    cv = c_ref[...]
    s = cv * _sigmoid(cv)
    o_ref[0] = jnp.dot(s, w_ref[0], preferred_element_type=F32,
                       precision=lax.Precision.HIGHEST) + b_ref[0]


def _mod_call(cvec, w_mod, b_mod):
    rows = cvec.shape[0]
    return pl.pallas_call(
        _mod_kernel,
        grid=(DEPTH, 3),
        in_specs=[pl.BlockSpec((rows, D_MODEL), lambda l, j: (0, 0)),
                  pl.BlockSpec((1, D_MODEL, D_MODEL), lambda l, j: (l, 0, j)),
                  pl.BlockSpec((1, 1, D_MODEL), lambda l, j: (l, 0, j))],
        out_specs=pl.BlockSpec((1, rows, D_MODEL), lambda l, j: (l, 0, j)),
        out_shape=jax.ShapeDtypeStruct((DEPTH, rows, 3 * D_MODEL), F32),
        compiler_params=_params("arbitrary", "arbitrary"),
        name="adaln_mod",
    )(cvec, w_mod, b_mod.reshape(DEPTH, 1, 3 * D_MODEL))


def _sub_mod(mod_ref, modc_ref, j):
    if modc_ref is None or j > 0:
        return mod_ref[0]
    return jnp.where(pl.program_id(1) == 0, modc_ref[0], mod_ref[0])


def _x_operands(xx, tm, latent_only=False):
    nsub = tm // SUB
    if not isinstance(xx, tuple):
        if not latent_only:
            return [xx], [pl.BlockSpec((1, tm, D_MODEL), lambda b, t: (b, t, 0))], False
        specs = [pl.BlockSpec((1, SUB, D_MODEL), lambda b, t, sb=sb: (b, nsub * t + sb + 1, 0)) for sb in range(nsub)]
        return [xx] * nsub, specs, False
    ctx, x = xx
    specs = [pl.BlockSpec((1, SUB, D_MODEL), lambda b, t: (b, 0, 0))]
    for sb in range(nsub):
        specs.append(pl.BlockSpec((1, SUB, D_MODEL), lambda b, t, sb=sb: (b, jnp.maximum(nsub * t + sb - 1, 0), 0)))
    return [ctx] + [x] * nsub, specs, True


def _read_x(x_refs, sb, has_ctx):
    if len(x_refs) == 1:
        return x_refs[0][0, sb * SUB:(sb + 1) * SUB, :]
    if not has_ctx:
        return x_refs[sb][0]
    if sb == 0:
        return jnp.where(pl.program_id(1) == 0, x_refs[0][0], x_refs[1][0])
    return x_refs[1 + sb][0]


def _modulated_norm(x, mod, nw_ref):
    ms = jnp.mean(x * x, axis=-1, keepdims=True)
    y = x * lax.rsqrt(ms + EPS) * nw_ref[...]
    return (y * (1.0 + mod[:, D_MODEL:2 * D_MODEL]) + mod[:, 0:D_MODEL]).astype(BF16)


def _qkv_kernel(*refs, n_x, has_ctx):
    x_refs = refs[:n_x]
    (mod_ref, modc_ref, nw_ref, w_rq, w_rk, w_rv, w_wq, w_wkv, w_aq, w_akv,
     cf_ref, sf_ref, cb_ref, sb_ref, ca_ref, sa_ref, gq_ref, gk_ref, seg_ref,
     qf_ref, qb_ref, kf_ref, kb_ref, rv_ref, wq_ref, wkv_ref, aq_ref, akv_ref) = refs[n_x:]
    k_scale = RET_DK ** -0.5
    q_scale = HEAD_DIM ** -0.5 * LOG2E
    low_half = lax.broadcasted_iota(jnp.int32, (SUB, LANES), 1) < 64

    def lanes(a, j):
        return a[:, j * LANES:(j + 1) * LANES]

    def qk_norm(acc, j0, g_ref):
        v2 = acc[:, j0 * LANES:(j0 + 2) * LANES]
        ss = jnp.dot((v2 * v2).astype(BF16), seg_ref[...], preferred_element_type=F32)
        r = lax.rsqrt(ss * (1.0 / HEAD_DIM) + EPS)
        return [lanes(v2, i) * lanes(r, i) * g_ref[...] for i in range(2)]

    for sb in range(qf_ref.shape[1] // SUB):
        rows = slice(sb * SUB, (sb + 1) * SUB)
        h = _modulated_norm(_read_x(x_refs, sb, has_ctx), _sub_mod(mod_ref, modc_ref, sb), nw_ref)

        def proj(w_ref):
            return jnp.dot(h, w_ref[...], preferred_element_type=F32)

        def rope(v, c_ref, s_ref):
            return v * c_ref[rows, :] + pltpu.roll(v, 64, 1) * s_ref[rows, :]

        def put(ref, j, val):
            ref[0, rows, j * LANES:(j + 1) * LANES] = val.astype(BF16)

        def store_kv(kv_ref, acc):
            va, vb = lanes(acc, 2), lanes(acc, 3)
            put(kv_ref, 2, jnp.where(low_half, va, 1.0))
            put(kv_ref, 3, jnp.where(low_half, 1.0, vb))
            put(kv_ref, 4, jnp.where(low_half, vb, 1.0))
            put(kv_ref, 5, jnp.where(low_half, 1.0, va))

        acc = proj(w_rq)
        for j in range(4):
            put(qf_ref, j, rope(lanes(acc, j), cf_ref, sf_ref))
            put(qb_ref, j, rope(lanes(acc, j), cb_ref, sb_ref))
        acc = proj(w_rk)
        for j in range(4):
            put(kf_ref, j, rope(lanes(acc, j), cf_ref, sf_ref) * k_scale)
            put(kb_ref, j, rope(lanes(acc, j), cb_ref, sb_ref) * k_scale)
        rv_ref[0, rows, :] = proj(w_rv).astype(BF16)

        acc = proj(w_wq)
        for j in range(4):
            put(wq_ref, j, rope(lanes(acc, j), ca_ref, sa_ref) * q_scale)
        acc = proj(w_wkv)
        for j in range(2):
            put(wkv_ref, j, rope(lanes(acc, j), ca_ref, sa_ref))
        store_kv(wkv_ref, acc)

        acc = proj(w_aq)
        for j, qn in enumerate(qk_norm(acc, 0, gq_ref) + qk_norm(acc, 2, gq_ref)):
            put(aq_ref, j, rope(qn, ca_ref, sa_ref) * q_scale)
        acc = proj(w_akv)
        for j, kn in enumerate(qk_norm(acc, 0, gk_ref)):
            put(akv_ref, j, rope(kn, ca_ref, sa_ref))
        store_kv(akv_ref, acc)


def _mod_specs(layer):
    return [pl.BlockSpec((None, 1, 1, 3 * D_MODEL), lambda b, t: (layer, b, 0, 0)),
            pl.BlockSpec((None, 1, 1, 3 * D_MODEL), lambda b, t: (layer, BATCH, 0, 0))]


def _qkv_call(layer, xx, mod, nw, w_in_bf, wqkv, tabs, gq, gk, seg):
    w_rq, w_rk, w_wq, w_wkv, w_aq, w_akv = wqkv
    w_spec = _layer_spec((D_MODEL, 512), layer)
    rv_spec = pl.BlockSpec((None, D_MODEL, 512), lambda *_: (layer, 0, _RV // 512), pipeline_mode=pl.Buffered(1))
    tok_spec = pl.BlockSpec((1, TM, 512), lambda b, t: (b, t, 0))
    tab_spec = pl.BlockSpec((TM, LANES), lambda b, t: (t, 0))
    out_sds = jax.ShapeDtypeStruct((BATCH, TOK, 512), BF16)
    kv_spec = pl.BlockSpec((1, TM, KV_W), lambda b, t: (b, t, 0))
    kv_sds = jax.ShapeDtypeStruct((BATCH, TOK, KV_W), BF16)
    x_ops, x_specs, has_ctx = _x_operands(xx, TM)
    return pl.pallas_call(
        functools.partial(_qkv_kernel, n_x=len(x_ops), has_ctx=has_ctx),
        grid=(BATCH, TOK // TM),
        in_specs=[*x_specs,
                  *_mod_specs(layer),
                  _layer_spec((1, D_MODEL), layer),
                  w_spec, w_spec, rv_spec, w_spec, w_spec, w_spec, w_spec,
                  tab_spec, tab_spec, tab_spec, tab_spec, tab_spec, tab_spec,
                  _layer_spec((1, LANES), layer), _layer_spec((1, LANES), layer),
                  _const_spec((2 * LANES, 2 * LANES))],
        out_specs=[tok_spec] * 6 + [kv_spec, tok_spec, kv_spec],
        out_shape=[out_sds] * 6 + [kv_sds, out_sds, kv_sds],
        compiler_params=_params("parallel", "arbitrary"),
        name="qkv_proj",
    )(*x_ops, mod, mod, nw, w_rq, w_rk, w_in_bf, w_wq, w_wkv, w_aq, w_akv, *tabs, gq, gk, seg)


def _log_sigmoid(a):
    return jnp.minimum(a, 0.0) - jnp.log1p(jnp.exp(-jnp.abs(a)))


def _ret_kernel(af_ref, ab_ref, qf_ref, qb_ref, kf_ref, kb_ref, v_ref, o_ref, acc_ref, st_ref, dec_ref,
                *, need_ctx):
    @pl.when(pl.program_id(0) == 0)
    def _():
        ri = lax.broadcasted_iota(jnp.int32, (BLOCK, BLOCK), 0).astype(F32)
        ci = lax.broadcasted_iota(jnp.int32, (BLOCK, BLOCK), 1).astype(F32)
        d = ri - ci
        for h in range(RET_HEADS):
            lgf = _log_sigmoid(af_ref[h])[0:1, :]
            lgb = _log_sigmoid(ab_ref[h])[0:1, :]
            dec_ref[2 * h, 0] = jnp.where(d >= 0, jnp.exp(jnp.maximum(d, 0.0) * lgf), 0.0)
            dec_ref[2 * h, 1] = jnp.exp((ri + 1.0) * lgf)
            dec_ref[2 * h, 2] = jnp.exp((BLOCK - 1.0 - ri) * lgf)
            dec_ref[2 * h, 3] = jnp.exp(0.0 * ri + float(BLOCK) * lgf)
            dec_ref[2 * h + 1, 0] = jnp.where(d < 0, jnp.exp(jnp.maximum(-d, 0.0) * lgb), 0.0)
            dec_ref[2 * h + 1, 1] = jnp.exp((float(BLOCK) - ri) * lgb)
            dec_ref[2 * h + 1, 2] = jnp.exp(ri * lgb)
            dec_ref[2 * h + 1, 3] = jnp.exp(0.0 * ri + float(BLOCK) * lgb)

    st_ref[...] = jnp.zeros_like(st_ref)
    first_row = 0 if need_ctx else CTX_LEN

    def visit(i, in_ctx, first, want_o):
        cf = i
        cb = (CTX_BLKS - 1 - i) if in_ctx else (NBLK + CTX_BLKS - 1 - i)
        pending = []
        for h in range(RET_HEADS):
            lanes_h = slice(h * LANES, (h + 1) * LANES)
            for dr, (q_ref, k_ref, c) in enumerate(((qf_ref, kf_ref, cf), (qb_ref, kb_ref, cb))):
                r0 = c * BLOCK if in_ctx else pl.multiple_of(c * BLOCK, BLOCK)
                j = 2 * h + dr
                k = k_ref[0, pl.ds(r0, BLOCK), lanes_h]
                v = v_ref[0, pl.ds(r0, BLOCK), lanes_h]
                state = st_ref[j]
                kd = (k.astype(F32) * dec_ref[j, 2]).astype(BF16)
                st_ref[j] = dec_ref[j, 3] * state + lax.dot_general(kd, v, _TN, preferred_element_type=F32)
                if want_o:
                    q = q_ref[0, pl.ds(r0, BLOCK), lanes_h]
                    s = lax.dot_general(q, k, _NT, preferred_element_type=F32)
                    cross = jnp.dot(q, state.astype(BF16), preferred_element_type=F32)
                    pending.append((j, r0, lanes_h, v, s, cross))
        for j, r0, lanes_h, v, s, cross in pending:
            o = jnp.dot((s * dec_ref[j, 0]).astype(BF16), v, preferred_element_type=F32) + cross * dec_ref[j, 1]
            if first:
                acc_ref[pl.ds(r0, BLOCK), lanes_h] = o
            else:
                tot = acc_ref[pl.ds(r0, BLOCK), lanes_h] + o
                tot = tot * lax.rsqrt(jnp.mean(tot * tot, axis=-1, keepdims=True) + EPS)
                o_ref[0, pl.ds(r0 - first_row, BLOCK), lanes_h] = tot.astype(o_ref.dtype)

    for i in range(CTX_BLKS):
        visit(i, True, i < CTX_BLKS // 2, need_ctx)
    half = (NBLK + CTX_BLKS) // 2

    def first_pass(i, carry):
        visit(i, False, True, True)
        return carry

    def second_pass(i, carry):
        visit(i, False, False, True)
        return carry

    lax.fori_loop(CTX_BLKS, half, first_pass, 0)
    lax.fori_loop(half, NBLK, second_pass, 0)


def _ret_call(layer, a_f, a_b, qf, qb, kf, kb, rv, need_ctx):
    n_out = TOK if need_ctx else SEQ
    tok_spec = pl.BlockSpec((1, TOK, 512), lambda b: (b, 0, 0))
    a_spec = _layer_spec((RET_HEADS, 8, LANES), layer)
    return pl.pallas_call(
        functools.partial(_ret_kernel, need_ctx=need_ctx),
        grid=(BATCH,),
        in_specs=[a_spec, a_spec, tok_spec, tok_spec, tok_spec, tok_spec, tok_spec],
        out_specs=pl.BlockSpec((1, n_out, 512), lambda b: (b, 0, 0)),
        out_shape=jax.ShapeDtypeStruct((BATCH, n_out, 512), BF16),
        scratch_shapes=[pltpu.VMEM((TOK, 512), F32),
                        pltpu.VMEM((2 * RET_HEADS, RET_DK, LANES), F32),
                        pltpu.VMEM((2 * RET_HEADS, 4, BLOCK, BLOCK), F32)],
        compiler_params=_params("arbitrary"),
        name="retention",
    )(a_f, a_b, qf, qb, kf, kb, rv)


def _slot_masks():
    lane = lax.broadcasted_iota(jnp.int32, (QB, LANES), 1)
    slot_a = (lane & 32) == 0
    return slot_a, lane < 64


def _head_queries(q_ref, c, use_b, slot_a):
    qv = q_ref[0, :, c * LANES:(c + 1) * LANES].astype(F32)
    qv = jnp.where(slot_a, 0.0, qv) if use_b else jnp.where(slot_a, qv, 0.0)
    return qv.astype(BF16)


def _pv(p, load_v, nkeys):
    acc = None
    for r0 in range(0, nkeys, PV_KC):
        d = jnp.dot(p[:, r0:r0 + PV_KC], load_v(r0, r0 + PV_KC), preferred_element_type=F32)
        acc = d if acc is None else acc + d
    return acc


def _store_heads(o_ref, res, low_half, extra=None):
    for c in range(4):
        ra, rb = res[2 * c], res[2 * c + 1]
        num = jnp.where(low_half, ra[:, :LANES], rb[:, LANES:])
        den = jnp.where(low_half, ra[:, LANES:], rb[:, :LANES])
        if extra is not None:
            den = den + jnp.where(low_half, extra[2 * c], extra[2 * c + 1])
        o_ref[0, :, c * LANES:(c + 1) * LANES] = (num * (1.0 / den)).astype(o_ref.dtype)


def _attn_specs(off):
    return dict(
        in_specs=[pl.BlockSpec((1, QB, 512), lambda b, i: (b, i + off, 0)),
                  pl.BlockSpec((1, QB, 512), lambda b, i: (b, jnp.minimum(i + off + 1, NQB - 1), 0)),
                  pl.BlockSpec((1, TOK, KV_W), lambda b, i: (b, 0, 0))],
        out_specs=pl.BlockSpec((1, QB, 512), lambda b, i: (b, i, 0)),
        out_shape=jax.ShapeDtypeStruct((BATCH, (NQB - off) * QB, 512), BF16),
    )


def _ax_kernel(q_ref, qn_ref, kv_ref, o_ref, s_ref, m_ref, *, blk_off):
    slot_a, low_half = _slot_masks()

    def scores(qr, head, nkeys):
        c, use_b, kb, _ = head
        qs = _head_queries(qr, c, use_b, slot_a)
        k = kv_ref[0, 0:nkeys, kb * LANES:(kb + 1) * LANES]
        return lax.dot_general(qs, k, _NT, preferred_element_type=F32)

    def carry(slot, s):
        s_ref[slot] = s
        m_ref[slot] = jnp.broadcast_to(jnp.max(s, axis=-1, keepdims=True), m_ref.shape[1:])

    def finish(s, head, nkeys, mx=None):
        _, use_b, _, vb = head
        if mx is None:
            mx = jnp.max(s, axis=-1, keepdims=True)
        p = jnp.exp2(s - mx).astype(BF16)
        v0 = (vb - use_b) * LANES
        return _pv(p, lambda r0, r1: kv_ref[0, r0:r1, v0:v0 + 2 * LANES], nkeys)

    nh = len(_ATT_HEADS)

    def prologue(qr):
        for a in range(AX_AHEAD):
            carry(a, scores(qr, _ATT_HEADS[a], TOK))

    def ctx_block():
        res = [finish(scores(q_ref, head, CTX_LEN), head, CTX_LEN) for head in _ATT_HEADS]
        _store_heads(o_ref, res, low_half)
        prologue(qn_ref)

    def latent_block():
        res = []
        ahead = [(s_ref[a], m_ref[a, :, 0:1]) for a in range(AX_AHEAD)]
        for n, head in enumerate(_ATT_HEADS):
            m = n + AX_AHEAD
            ahead.append((scores(q_ref, _ATT_HEADS[m], TOK) if m < nh else scores(qn_ref, _ATT_HEADS[m - nh], TOK), None))
            s, mx = ahead.pop(0)
            res.append(finish(s, head, TOK, mx))
        for a in range(AX_AHEAD):
            carry(a, ahead[a][0])
        _store_heads(o_ref, res, low_half)

    blk = pl.program_id(1)
    if blk_off == 0:
        pl.when(blk == 0)(ctx_block)
        pl.when(blk > 0)(latent_block)
    else:
        pl.when(blk == 0)(lambda: prologue(q_ref))
        latent_block()


def _ax_call(q, kv, need_ctx):
    off = 0 if need_ctx else 1
    specs = _attn_specs(off)
    return pl.pallas_call(
        functools.partial(_ax_kernel, blk_off=off),
        grid=(BATCH, NQB - off),
        scratch_shapes=[pltpu.VMEM((AX_AHEAD, QB, TOK), F32), pltpu.VMEM((AX_AHEAD, QB, LANES), F32)],
        compiler_params=_params("parallel", "arbitrary"),
        name="axial_attn",
        **specs,
    )(q, q, kv)


def _win_kernel(sink_ref, q_ref, qn_ref, kv_ref, o_ref, s_ref, *, layer, blk_off):
    slot_a, low_half = _slot_masks()
    nh = len(_ATT_HEADS)
    nk = CTX_LEN + QB + 2 * BLOCK

    def window_loader(n):
        left = jnp.maximum(2 * n - 1, 0)
        right = jnp.minimum(2 * n + 2, LAT_BLKS - 1)
        s_left = pl.multiple_of(CTX_LEN + left * BLOCK, BLOCK)
        s_mid = pl.multiple_of(CTX_LEN + n * QB, QB)
        s_right = pl.multiple_of(CTX_LEN + right * BLOCK, BLOCK)
        cache = {}

        def load(c0, w):
            if (c0, w) not in cache:
                cache[(c0, w)] = jnp.concatenate([kv_ref[0, 0:CTX_LEN, c0:c0 + w],
                                                  kv_ref[0, pl.ds(s_left, BLOCK), c0:c0 + w],
                                                  kv_ref[0, pl.ds(s_mid, QB), c0:c0 + w],
                                                  kv_ref[0, pl.ds(s_right, BLOCK), c0:c0 + w]], axis=0)
            return cache[(c0, w)]

        return load

    def window_mask(n):
        col = lax.broadcasted_iota(jnp.int32, (QB, nk), 1)
        qi = lax.broadcasted_iota(jnp.int32, (QB, nk), 0)
        jj = col - CTX_LEN
        dlt = jj - qi
        lo = jnp.where(n >= 1, 0, BLOCK)
        hi = jnp.where(n <= SEQ // QB - 2, QB + 2 * BLOCK, QB + BLOCK)
        in_win = (dlt >= 0) & (dlt <= 2 * BLOCK) & (jj >= lo) & (jj < hi)
        return (col < CTX_LEN) | in_win

    def scores(qr, head, load):
        c, use_b, kb, _ = head
        qs = _head_queries(qr, c, use_b, slot_a)
        return lax.dot_general(qs, load(kb * LANES, LANES), _NT, preferred_element_type=F32)

    def finish(s, head, load, valid):
        c, use_b, _, vb = head
        if valid is not None:
            s = jnp.where(valid, s, NEG)
        sk = sink_ref[layer, 2 * c + use_b] * LOG2E
        mx = jnp.maximum(jnp.max(s, axis=-1, keepdims=True), sk)
        p = jnp.exp2(s - mx).astype(BF16)
        v = load((vb - use_b) * LANES, 2 * LANES)
        return _pv(p, lambda r0, r1: v[r0:r1], v.shape[0]), jnp.exp2(sk - mx)

    def store(done):
        _store_heads(o_ref, [r for r, _ in done], low_half, [t for _, t in done])

    def prologue(qr, n):
        load = window_loader(n)
        for a in range(WIN_AHEAD):
            s_ref[a] = scores(qr, _ATT_HEADS[a], load)

    def ctx_block():
        def load(c0, w):
            return kv_ref[0, 0:CTX_LEN, c0:c0 + w]

        store([finish(scores(q_ref, head, load), head, load, None) for head in _ATT_HEADS])
        prologue(qn_ref, 0)

    def latent_block(n):
        load = window_loader(n)
        load_next = window_loader(jnp.minimum(n + 1, SEQ // QB - 1))
        valid = window_mask(n)
        ahead = [s_ref[a] for a in range(WIN_AHEAD)]
        done = []
        for i, head in enumerate(_ATT_HEADS):
            m = i + WIN_AHEAD
            ahead.append(scores(q_ref, _ATT_HEADS[m], load) if m < nh else scores(qn_ref, _ATT_HEADS[m - nh], load_next))
            done.append(finish(ahead.pop(0), head, load, valid))
        for a in range(WIN_AHEAD):
            s_ref[a] = ahead[a]
        store(done)

    blk = pl.program_id(1)
    if blk_off == 0:
        pl.when(blk == 0)(ctx_block)
        pl.when(blk > 0)(lambda: latent_block(blk - 1))
    else:
        pl.when(blk == 0)(lambda: prologue(q_ref, 0))
        latent_block(blk)


def _win_call(layer, sink, q, kv, need_ctx):
    off = 0 if need_ctx else 1
    specs = _attn_specs(off)
    specs["in_specs"] = [pl.BlockSpec(memory_space=pltpu.SMEM)] + specs["in_specs"]
    return pl.pallas_call(
        functools.partial(_win_kernel, layer=layer, blk_off=off),
        grid=(BATCH, NQB - off),
        scratch_shapes=[pltpu.VMEM((WIN_AHEAD, QB, CTX_LEN + QB + 2 * BLOCK), F32)],
        compiler_params=_params("parallel", "arbitrary"),
        name="window_attn",
        **specs,
    )(sink, q, q, kv)


def _merge_kernel(*refs, n_x, has_ctx, final):
    x_refs = refs[:n_x]
    (mod_ref, modc_ref, nw_ref, or_ref, ow_ref, oa_ref, win_ref, wpr_ref, wpw_ref, wpa_ref,
     wo_ref, fw_ref, out_ref) = refs[n_x:]
    nsub = out_ref.shape[1] // SUB
    branches = ((or_ref, wpr_ref, _RG), (ow_ref, wpw_ref, _WG), (oa_ref, wpa_ref, _AG))

    def from_h(sb):
        mod = _sub_mod(mod_ref, None if final else modc_ref, sb)
        x = _read_x(x_refs, sb, has_ctx)
        h = _modulated_norm(x, mod, nw_ref)
        gates = [jnp.dot(h, win_ref[:, g0:g0 + 512], preferred_element_type=F32) for _, _, g0 in branches]
        logits = [jnp.dot(h, win_ref[:, _MG + j * D_MODEL:_MG + (j + 1) * D_MODEL], preferred_element_type=F32)
                  for j in range(3)]
        return x, mod, gates, logits

    def finish(sb, x, mod, gates, logits):
        rows = slice(sb * SUB, (sb + 1) * SUB)
        mix = None
        for (o_ref, wp_ref, _), g, ml in zip(branches, gates, logits):
            u = (o_ref[0, rows, :].astype(F32) * (g * _sigmoid(g))).astype(BF16)
            t = _sigmoid(ml) * jnp.dot(u, wp_ref[...], preferred_element_type=F32)
            mix = t if mix is None else mix + t
        y = jnp.dot(mix.astype(BF16), wo_ref[...], preferred_element_type=F32)
        xn = x + mod[:, 2 * D_MODEL:3 * D_MODEL] * y
        if final:
            xn = xn * lax.rsqrt(jnp.mean(xn * xn, axis=-1, keepdims=True) + EPS) * fw_ref[...]
        out_ref[0, rows, :] = xn

    cur = from_h(0)
    for sb in range(nsub):
        nxt = from_h(sb + 1) if sb + 1 < nsub else None
        finish(sb, *cur)
        cur = nxt


def _merge_call(layer, xx, mod, nw, o_ret, o_win, o_ax, w_in_bf, wpr, wpw, wpa, wo, fw, final):
    tm = TM_FINAL if final else TM
    n_tok = SEQ if final else TOK
    o_spec = pl.BlockSpec((1, tm, 512), lambda b, t: (b, t, 0))
    x_ops, x_specs, has_ctx = _x_operands(xx, tm, latent_only=final)
    return pl.pallas_call(
        functools.partial(_merge_kernel, n_x=len(x_ops), has_ctx=has_ctx, final=final),
        grid=(BATCH, n_tok // tm),
        in_specs=[*x_specs,
                  *_mod_specs(layer),
                  _layer_spec((1, D_MODEL), layer),
                  o_spec, o_spec, o_spec,
                  _layer_spec((D_MODEL, IN_W), layer),
                  _layer_spec((512, D_MODEL), layer), _layer_spec((512, D_MODEL), layer),
                  _layer_spec((512, D_MODEL), layer),
                  _layer_spec((D_MODEL, D_MODEL), layer),
                  _const_spec((1, D_MODEL))],
        out_specs=pl.BlockSpec((1, tm, D_MODEL), lambda b, t: (b, t, 0)),
        out_shape=jax.ShapeDtypeStruct((BATCH, n_tok, D_MODEL), F32),
        compiler_params=_params("parallel", "arbitrary"),
        name="merge_final" if final else "merge",
    )(*x_ops, mod, mod, nw, o_ret, o_win, o_ax, w_in_bf, wpr, wpw, wpa, wo, fw)


def _rope_tables():
    t = jnp.arange(TOK, dtype=F32)
    theta = ROPE_BASE ** (-jnp.linspace(0.0, 1.0, RET_DK // 2, dtype=F32))
    pos_b = jnp.where(t < CTX_LEN, CTX_LEN - 1.0 - t, 2.0 * CTX_LEN + SEQ - 1.0 - t)

    def ret_pair(pos):
        ang = pos[:, None] * theta[None]
        c, s = jnp.cos(ang), jnp.sin(ang)
        return jnp.concatenate([c, c], axis=-1), jnp.concatenate([-s, s], axis=-1)

    cf, sf = ret_pair(t)
    cb, sb = ret_pair(pos_b)

    s_idx = jnp.arange(SEQ)
    quarter = HEAD_DIM // 4
    freqs = ROPE_BASE ** (-jnp.arange(quarter, dtype=F32) / quarter)
    r = (s_idx // GRID_W).astype(F32)
    col = (s_idx % GRID_W).astype(F32)
    ang = jnp.concatenate([r[:, None] * freqs[None], col[:, None] * freqs[None]], axis=-1)
    c, s = jnp.cos(ang), jnp.sin(ang)
    ca = jnp.concatenate([jnp.ones((CTX_LEN, LANES), F32), jnp.concatenate([c, c, c, c], axis=-1)], axis=0)
    sa = jnp.concatenate([jnp.zeros((CTX_LEN, LANES), F32), jnp.concatenate([-s, -s, s, s], axis=-1)], axis=0)
    return cf, sf, cb, sb, ca, sa


def _segment_matrix():
    lane = np.arange(2 * LANES)
    head = 2 * (lane // LANES) + (lane // 32) % 2
    return jnp.asarray((head[:, None] == head[None, :]).astype(np.float32), dtype=BF16)


def kernel(x, c, ctx, c_ctx, norm_w, w_mod, b_mod, w_in, ret_decay_fwd, ret_decay_bwd, win_sink,
           ax_q_gain, ax_k_gain, w_proj_ret, w_proj_win, w_proj_ax, w_out, final_norm_w):
    xx = (ctx, x)
    cvec = jnp.zeros((24, D_MODEL), F32).at[:BATCH].set(c).at[BATCH].set(c_ctx)
    mod = _mod_call(cvec, w_mod, b_mod).reshape(DEPTH, 24, 1, 3 * D_MODEL)
    tabs = _rope_tables()
    seg = _segment_matrix()
    w_in_bf = w_in.astype(BF16)
    wqkv = _qkv_weights(w_in_bf)
    wpr, wpw, wpa, wo = (w.astype(BF16) for w in (w_proj_ret, w_proj_win, w_proj_ax, w_out))
    nw = norm_w.reshape(DEPTH, 1, D_MODEL)
    gain_lanes = jnp.asarray(_GAIN_LANES)
    gq = ax_q_gain[:, gain_lanes].reshape(DEPTH, 1, LANES)
    gk = ax_k_gain[:, gain_lanes].reshape(DEPTH, 1, LANES)
    a_f = jnp.broadcast_to(ret_decay_fwd[:, :, None, None], (DEPTH, RET_HEADS, 8, LANES))
    a_b = jnp.broadcast_to(ret_decay_bwd[:, :, None, None], (DEPTH, RET_HEADS, 8, LANES))
    fw = final_norm_w.reshape(1, D_MODEL)

    for l in range(DEPTH):
        need_ctx = l < DEPTH - 1
        qf, qb, kf, kb, rv, wq, wkv, aq, akv = _qkv_call(l, xx, mod, nw, w_in_bf, wqkv, tabs, gq, gk, seg)
        o_ret = _ret_call(l, a_f, a_b, qf, qb, kf, kb, rv, need_ctx)
        o_win = _win_call(l, win_sink, wq, wkv, need_ctx)
        o_ax = _ax_call(aq, akv, need_ctx)
        xx = _merge_call(l, xx, mod, nw, o_ret, o_win, o_ax, w_in_bf, wpr, wpw, wpa, wo, fw, final=not need_ctx)
    return xx
```

```python
import functools

import numpy as np
import jax
import jax.numpy as jnp
from jax import lax
from jax.experimental import pallas as pl
from jax.experimental.pallas import tpu as pltpu

D_MODEL = 1024
BATCH = 16
SEQ = 2048
DEPTH = 4
CTX_LEN = 256
TOK = CTX_LEN + SEQ
GRID_W = 64
BLOCK = 128
RET_HEADS = 4
RET_DK = 128
HEAD_DIM = 64
ROPE_BASE = 10000.0
EPS = 1e-6
NEG = -1e30

LANES = 128
NBLK = TOK // BLOCK
CTX_BLKS = CTX_LEN // BLOCK
LAT_BLKS = SEQ // BLOCK
LOG2E = 1.4426950408889634
KV_W = 6 * LANES
QB = CTX_LEN
NQB = TOK // QB
AX_AHEAD = 2
WIN_AHEAD = 2
PV_KC = 256

SUB = CTX_LEN
TM = 3 * SUB
TM_FINAL = 4 * SUB
VMEM_LIMIT = 52 * 1024 * 1024

F32 = jnp.float32
BF16 = jnp.bfloat16

_IN_SIZES = (512, 512, 512, 512, 512, 128, 128, 512, 512, 128, 128, 512, 3 * D_MODEL)
_OFF = np.concatenate([[0], np.cumsum(_IN_SIZES)]).astype(np.int64)
(_RQ, _RK, _RV, _RG, _WQ, _WK, _WV, _WG, _AQ, _AK, _AV, _AG, _MG) = [int(o) for o in _OFF[:-1]]

IN_W = int(_OFF[-1])


def _kv_weights(w, k0, v0):
    k, v = w[:, :, k0:k0 + LANES], w[:, :, v0:v0 + LANES]

    def swapped(x):
        return jnp.concatenate([x[..., HEAD_DIM:], x[..., :HEAD_DIM]], axis=-1)

    return jnp.concatenate([k, swapped(k), v, swapped(v)], axis=-1)

_ATT_HEADS = tuple((c, b, (c // 2 + b) % 2, 2 + 2 * (c // 2) + b) for c in range(4) for b in (0, 1))

_NT = (((1,), (1,)), ((), ()))
_TN = (((0,), (0,)), ((), ()))


def _const_spec(shape):
    nd = len(shape)
    return pl.BlockSpec(shape, lambda *_: (0,) * nd, pipeline_mode=pl.Buffered(1))


def _layer_spec(shape, layer):
    nd = len(shape)
    return pl.BlockSpec((None, *shape), lambda *_: (layer,) + (0,) * nd, pipeline_mode=pl.Buffered(1))


def _params(*sem):
    return pltpu.CompilerParams(dimension_semantics=sem, vmem_limit_bytes=VMEM_LIMIT)


def _sigmoid(z):
    return 1.0 / (1.0 + jnp.exp(-z))


def _mod_kernel(c_ref, w_ref, b_ref, o_ref):
    cv = c_ref[...]
    s = cv * _sigmoid(cv)
    o_ref[0] = jnp.dot(s, w_ref[0], preferred_element_type=F32,
                       precision=lax.Precision.HIGHEST) + b_ref[0]


def _mod_call(cvec, w_mod, b_mod):
    rows = cvec.shape[0]
    return pl.pallas_call(
        _mod_kernel,
        grid=(DEPTH, 3),
        in_specs=[pl.BlockSpec((rows, D_MODEL), lambda l, j: (0, 0)),
                  pl.BlockSpec((1, D_MODEL, D_MODEL), lambda l, j: (l, 0, j)),
                  pl.BlockSpec((1, 1, D_MODEL), lambda l, j: (l, 0, j))],
        out_specs=pl.BlockSpec((1, rows, D_MODEL), lambda l, j: (l, 0, j)),
        out_shape=jax.ShapeDtypeStruct((DEPTH, rows, 3 * D_MODEL), F32),
        compiler_params=_params("arbitrary", "arbitrary"),
        name="adaln_mod",
    )(cvec, w_mod, b_mod.reshape(DEPTH, 1, 3 * D_MODEL))


def _sub_mod(mod_ref, modc_ref, j):
    if modc_ref is None or j > 0:
        return mod_ref[0]
    return jnp.where(pl.program_id(1) == 0, modc_ref[0], mod_ref[0])


def _x_operands(xx, tm, latent_only=False):
    nsub = tm // SUB
    if not isinstance(xx, tuple):
        if not latent_only:
            return [xx], [pl.BlockSpec((1, tm, D_MODEL), lambda b, t: (b, t, 0))], False
        specs = [pl.BlockSpec((1, SUB, D_MODEL), lambda b, t, sb=sb: (b, nsub * t + sb + 1, 0)) for sb in range(nsub)]
        return [xx] * nsub, specs, False
    ctx, x = xx
    specs = [pl.BlockSpec((1, SUB, D_MODEL), lambda b, t: (b, 0, 0))]
    for sb in range(nsub):
        specs.append(pl.BlockSpec((1, SUB, D_MODEL), lambda b, t, sb=sb: (b, jnp.maximum(nsub * t + sb - 1, 0), 0)))
    return [ctx] + [x] * nsub, specs, True


def _read_x(x_refs, sb, has_ctx):
    if len(x_refs) == 1:
        return x_refs[0][0, sb * SUB:(sb + 1) * SUB, :]
    if not has_ctx:
        return x_refs[sb][0]
    if sb == 0:
        return jnp.where(pl.program_id(1) == 0, x_refs[0][0], x_refs[1][0])
    return x_refs[1 + sb][0]


def _modulated_norm(x, mod, nw_ref):
    ms = jnp.mean(x * x, axis=-1, keepdims=True)
    y = x * lax.rsqrt(ms + EPS) * nw_ref[...]
    return (y * (1.0 + mod[:, D_MODEL:2 * D_MODEL]) + mod[:, 0:D_MODEL]).astype(BF16)


def _qkv_kernel(*refs, n_x, has_ctx):
    x_refs = refs[:n_x]
    (mod_ref, modc_ref, nw_ref, w_rq, w_rk, w_rv, w_wq, w_wkv, w_aq, w_akv,
     cf_ref, sf_ref, cb_ref, sb_ref, ca_ref, sa_ref, gq_ref, gk_ref, seg_ref,
     qf_ref, qb_ref, kf_ref, kb_ref, rv_ref, wq_ref, wkv_ref, aq_ref, akv_ref) = refs[n_x:]
    k_scale = RET_DK ** -0.5
    q_scale = HEAD_DIM ** -0.5 * LOG2E
    lane = lax.broadcasted_iota(jnp.int32, (SUB, LANES), 1)
    low_half = lane < HEAD_DIM
    even_lane = (lane & 1) == 0

    def lanes(a, j):
        return a[:, j * LANES:(j + 1) * LANES]

    def qk_norm(acc, j0, g_ref):
        v2 = acc[:, j0 * LANES:(j0 + 2) * LANES]
        ss = jnp.dot((v2 * v2).astype(BF16), seg_ref[...], preferred_element_type=F32)
        r = lax.rsqrt(ss * (1.0 / HEAD_DIM) + EPS)
        return [lanes(v2, i) * lanes(r, i) * g_ref[...] for i in range(2)]

    for sb in range(qf_ref.shape[1] // SUB):
        rows = slice(sb * SUB, (sb + 1) * SUB)
        h = _modulated_norm(_read_x(x_refs, sb, has_ctx), _sub_mod(mod_ref, modc_ref, sb), nw_ref)

        def proj(w_ref):
            return jnp.dot(h, w_ref[...], preferred_element_type=F32)

        def rope(v, c_ref, s_ref):
            partner = jnp.where(even_lane, pltpu.roll(v, LANES - 1, 1), pltpu.roll(v, 1, 1))
            return v * c_ref[rows, :] + partner * s_ref[rows, :]

        def put(ref, j, val):
            ref[0, rows, j * LANES:(j + 1) * LANES] = val.astype(BF16)

        def store_kv(kv_ref, acc):
            va, vb = lanes(acc, 2), lanes(acc, 3)
            put(kv_ref, 2, jnp.where(low_half, va, 1.0))
            put(kv_ref, 3, jnp.where(low_half, 1.0, vb))
            put(kv_ref, 4, jnp.where(low_half, vb, 1.0))
            put(kv_ref, 5, jnp.where(low_half, 1.0, va))

        acc = proj(w_rq)
        for j in range(4):
            put(qf_ref, j, rope(lanes(acc, j), cf_ref, sf_ref))
            put(qb_ref, j, rope(lanes(acc, j), cb_ref, sb_ref))
        acc = proj(w_rk)
        for j in range(4):
            put(kf_ref, j, rope(lanes(acc, j), cf_ref, sf_ref) * k_scale)
            put(kb_ref, j, rope(lanes(acc, j), cb_ref, sb_ref) * k_scale)
        rv_ref[0, rows, :] = proj(w_rv).astype(BF16)

        acc = proj(w_wq)
        for j in range(4):
            put(wq_ref, j, rope(lanes(acc, j), ca_ref, sa_ref) * q_scale)
        acc = proj(w_wkv)
        for j in range(2):
            put(wkv_ref, j, rope(lanes(acc, j), ca_ref, sa_ref))
        store_kv(wkv_ref, acc)

        acc = proj(w_aq)
        for j, qn in enumerate(qk_norm(acc, 0, gq_ref) + qk_norm(acc, 2, gq_ref)):
            put(aq_ref, j, rope(qn, ca_ref, sa_ref) * q_scale)
        acc = proj(w_akv)
        for j, kn in enumerate(qk_norm(acc, 0, gk_ref)):
            put(akv_ref, j, rope(kn, ca_ref, sa_ref))
        store_kv(akv_ref, acc)


def _mod_specs(layer):
    return [pl.BlockSpec((None, 1, 1, 3 * D_MODEL), lambda b, t: (layer, b, 0, 0)),
            pl.BlockSpec((None, 1, 1, 3 * D_MODEL), lambda b, t: (layer, BATCH, 0, 0))]


def _qkv_call(layer, xx, mod, nw, w_in_bf, w_wkv, w_aq, w_akv, tabs, gq, gk, seg):
    w_spec = _layer_spec((D_MODEL, 512), layer)

    def col_spec(c0):
        return pl.BlockSpec((None, D_MODEL, 512), lambda *_: (layer, 0, c0 // 512), pipeline_mode=pl.Buffered(1))

    tok_spec = pl.BlockSpec((1, TM, 512), lambda b, t: (b, t, 0))
    tab_spec = pl.BlockSpec((TM, LANES), lambda b, t: (t, 0))
    out_sds = jax.ShapeDtypeStruct((BATCH, TOK, 512), BF16)
    kv_spec = pl.BlockSpec((1, TM, KV_W), lambda b, t: (b, t, 0))
    kv_sds = jax.ShapeDtypeStruct((BATCH, TOK, KV_W), BF16)
    x_ops, x_specs, has_ctx = _x_operands(xx, TM)
    return pl.pallas_call(
        functools.partial(_qkv_kernel, n_x=len(x_ops), has_ctx=has_ctx),
        grid=(BATCH, TOK // TM),
        in_specs=[*x_specs,
                  *_mod_specs(layer),
                  _layer_spec((1, D_MODEL), layer),
                  col_spec(_RQ), col_spec(_RK), col_spec(_RV), col_spec(_WQ), w_spec, w_spec, w_spec,
                  tab_spec, tab_spec, tab_spec, tab_spec, tab_spec, tab_spec,
                  _layer_spec((1, LANES), layer), _layer_spec((1, LANES), layer),
                  _const_spec((2 * LANES, 2 * LANES))],
        out_specs=[tok_spec] * 6 + [kv_spec, tok_spec, kv_spec],
        out_shape=[out_sds] * 6 + [kv_sds, out_sds, kv_sds],
        compiler_params=_params("parallel", "arbitrary"),
        name="qkv_proj",
    )(*x_ops, mod, mod, nw, w_in_bf, w_in_bf, w_in_bf, w_in_bf, w_wkv, w_aq, w_akv, *tabs, gq, gk, seg)


def _log_sigmoid(a):
    return jnp.minimum(a, 0.0) - jnp.log1p(jnp.exp(-jnp.abs(a)))


def _ret_kernel(af_ref, ab_ref, qf_ref, qb_ref, kf_ref, kb_ref, v_ref, o_ref, acc_ref, st_ref, dec_ref,
                *, need_ctx):
    ri = lax.broadcasted_iota(jnp.int32, (BLOCK, BLOCK), 0).astype(F32)
    ci = lax.broadcasted_iota(jnp.int32, (BLOCK, BLOCK), 1).astype(F32)
    d = ri - ci
    for h in range(RET_HEADS):
        lgf = _log_sigmoid(af_ref[h])[0:1, :]
        lgb = _log_sigmoid(ab_ref[h])[0:1, :]
        dec_ref[2 * h, 0] = jnp.where(d >= 0, jnp.exp(jnp.maximum(d, 0.0) * lgf), 0.0)
        dec_ref[2 * h, 1] = jnp.exp((ri + 1.0) * lgf)
        dec_ref[2 * h, 2] = jnp.exp((BLOCK - 1.0 - ri) * lgf)
        dec_ref[2 * h, 3] = jnp.exp(0.0 * ri + float(BLOCK) * lgf)
        dec_ref[2 * h + 1, 0] = jnp.where(d < 0, jnp.exp(jnp.maximum(-d, 0.0) * lgb), 0.0)
        dec_ref[2 * h + 1, 1] = jnp.exp((float(BLOCK) - ri) * lgb)
        dec_ref[2 * h + 1, 2] = jnp.exp(ri * lgb)
        dec_ref[2 * h + 1, 3] = jnp.exp(0.0 * ri + float(BLOCK) * lgb)
    st_ref[...] = jnp.zeros_like(st_ref)
    first_row = 0 if need_ctx else CTX_LEN

    def visit(i, in_ctx, first, want_o):
        cf = i
        cb = (CTX_BLKS - 1 - i) if in_ctx else (NBLK + CTX_BLKS - 1 - i)
        pending = []
        for h in range(RET_HEADS):
            lanes_h = slice(h * LANES, (h + 1) * LANES)
            for dr, (q_ref, k_ref, c) in enumerate(((qf_ref, kf_ref, cf), (qb_ref, kb_ref, cb))):
                r0 = c * BLOCK if in_ctx else pl.multiple_of(c * BLOCK, BLOCK)
                j = 2 * h + dr
                k = k_ref[0, pl.ds(r0, BLOCK), lanes_h]
                v = v_ref[0, pl.ds(r0, BLOCK), lanes_h]
                state = st_ref[j]
                kd = (k.astype(F32) * dec_ref[j, 2]).astype(BF16)
                st_ref[j] = dec_ref[j, 3] * state + lax.dot_general(kd, v, _TN, preferred_element_type=F32)
                if want_o:
                    q = q_ref[0, pl.ds(r0, BLOCK), lanes_h]
                    s = lax.dot_general(q, k, _NT, preferred_element_type=F32)
                    cross = jnp.dot(q, state.astype(BF16), preferred_element_type=F32)
                    pending.append((j, r0, lanes_h, v, s, cross))
        for j, r0, lanes_h, v, s, cross in pending:
            o = jnp.dot((s * dec_ref[j, 0]).astype(BF16), v, preferred_element_type=F32) + cross * dec_ref[j, 1]
            if first:
                acc_ref[pl.ds(r0, BLOCK), lanes_h] = o
            else:
                tot = acc_ref[pl.ds(r0, BLOCK), lanes_h] + o
                tot = tot * lax.rsqrt(jnp.mean(tot * tot, axis=-1, keepdims=True) + EPS)
                o_ref[0, pl.ds(r0 - first_row, BLOCK), lanes_h] = tot.astype(o_ref.dtype)

    for i in range(CTX_BLKS):
        visit(i, True, i < CTX_BLKS // 2, need_ctx)
    half = (NBLK + CTX_BLKS) // 2

    def first_pass(i, carry):
        visit(i, False, True, True)
        return carry

    def second_pass(i, carry):
        visit(i, False, False, True)
        return carry

    lax.fori_loop(CTX_BLKS, half, first_pass, 0)
    lax.fori_loop(half, NBLK, second_pass, 0)


def _ret_call(layer, a_f, a_b, qf, qb, kf, kb, rv, need_ctx):
    n_out = TOK if need_ctx else SEQ
    tok_spec = pl.BlockSpec((1, TOK, 512), lambda b: (b, 0, 0))
    a_spec = _layer_spec((RET_HEADS, 8, LANES), layer)
    return pl.pallas_call(
        functools.partial(_ret_kernel, need_ctx=need_ctx),
        grid=(BATCH,),
        in_specs=[a_spec, a_spec, tok_spec, tok_spec, tok_spec, tok_spec, tok_spec],
        out_specs=pl.BlockSpec((1, n_out, 512), lambda b: (b, 0, 0)),
        out_shape=jax.ShapeDtypeStruct((BATCH, n_out, 512), BF16),
        scratch_shapes=[pltpu.VMEM((TOK, 512), F32),
                        pltpu.VMEM((2 * RET_HEADS, RET_DK, LANES), F32),
                        pltpu.VMEM((2 * RET_HEADS, 4, BLOCK, BLOCK), F32)],
        compiler_params=_params("parallel"),
        name="retention",
    )(a_f, a_b, qf, qb, kf, kb, rv)


def _low_half_mask():
    return lax.broadcasted_iota(jnp.int32, (QB, LANES), 1) < HEAD_DIM


def _head_queries(q_ref, c, use_b, low_half):
    qv = q_ref[0, :, c * LANES:(c + 1) * LANES].astype(F32)
    qv = jnp.where(low_half, 0.0, qv) if use_b else jnp.where(low_half, qv, 0.0)
    return qv.astype(BF16)


def _pv(p, load_v, nkeys):
    acc = None
    for r0 in range(0, nkeys, PV_KC):
        d = jnp.dot(p[:, r0:r0 + PV_KC], load_v(r0, r0 + PV_KC), preferred_element_type=F32)
        acc = d if acc is None else acc + d
    return acc


def _store_heads(o_ref, res, low_half, extra=None):
    for c in range(4):
        ra, rb = res[2 * c], res[2 * c + 1]
        num = jnp.where(low_half, ra[:, :LANES], rb[:, LANES:])
        den = jnp.where(low_half, ra[:, LANES:], rb[:, :LANES])
        if extra is not None:
            den = den + jnp.where(low_half, extra[2 * c], extra[2 * c + 1])
        o_ref[0, :, c * LANES:(c + 1) * LANES] = (num * (1.0 / den)).astype(o_ref.dtype)


def _attn_specs(off):
    return dict(
        in_specs=[pl.BlockSpec((1, QB, 512), lambda b, i: (b, i + off, 0)),
                  pl.BlockSpec((1, QB, 512), lambda b, i: (b, jnp.minimum(i + off + 1, NQB - 1), 0)),
                  pl.BlockSpec((1, TOK, KV_W), lambda b, i: (b, 0, 0))],
        out_specs=pl.BlockSpec((1, QB, 512), lambda b, i: (b, i, 0)),
        out_shape=jax.ShapeDtypeStruct((BATCH, (NQB - off) * QB, 512), BF16),
    )


def _ax_kernel(q_ref, qn_ref, kv_ref, o_ref, s_ref, m_ref, *, blk_off):
    low_half = _low_half_mask()

    def scores(qr, head, nkeys):
        c, use_b, kb, _ = head
        qs = _head_queries(qr, c, use_b, low_half)
        k = kv_ref[0, 0:nkeys, kb * LANES:(kb + 1) * LANES]
        return lax.dot_general(qs, k, _NT, preferred_element_type=F32)

    def carry(slot, s):
        s_ref[slot] = s
        m_ref[slot] = jnp.broadcast_to(jnp.max(s, axis=-1, keepdims=True), m_ref.shape[1:])

    def finish(s, head, nkeys, mx=None):
        _, use_b, _, vb = head
        if mx is None:
            mx = jnp.max(s, axis=-1, keepdims=True)
        p = jnp.exp2(s - mx).astype(BF16)
        v0 = (vb - use_b) * LANES
        return _pv(p, lambda r0, r1: kv_ref[0, r0:r1, v0:v0 + 2 * LANES], nkeys)

    nh = len(_ATT_HEADS)

    def prologue(qr):
        for a in range(AX_AHEAD):
            carry(a, scores(qr, _ATT_HEADS[a], TOK))

    def ctx_block():
        res = [finish(scores(q_ref, head, CTX_LEN), head, CTX_LEN) for head in _ATT_HEADS]
        _store_heads(o_ref, res, low_half)
        prologue(qn_ref)

    def latent_block():
        res = []
        ahead = [(s_ref[a], m_ref[a, :, 0:1]) for a in range(AX_AHEAD)]
        for n, head in enumerate(_ATT_HEADS):
            m = n + AX_AHEAD
            ahead.append((scores(q_ref, _ATT_HEADS[m], TOK) if m < nh else scores(qn_ref, _ATT_HEADS[m - nh], TOK), None))
            s, mx = ahead.pop(0)
            res.append(finish(s, head, TOK, mx))
        for a in range(AX_AHEAD):
            carry(a, ahead[a][0])
        _store_heads(o_ref, res, low_half)

    blk = pl.program_id(1)
    if blk_off == 0:
        pl.when(blk == 0)(ctx_block)
        pl.when(blk > 0)(latent_block)
    else:
        pl.when(blk == 0)(lambda: prologue(q_ref))
        latent_block()


def _ax_call(q, kv, need_ctx):
    off = 0 if need_ctx else 1
    specs = _attn_specs(off)
    return pl.pallas_call(
        functools.partial(_ax_kernel, blk_off=off),
        grid=(BATCH, NQB - off),
        scratch_shapes=[pltpu.VMEM((AX_AHEAD, QB, TOK), F32), pltpu.VMEM((AX_AHEAD, QB, LANES), F32)],
        compiler_params=_params("parallel", "arbitrary"),
        name="axial_attn",
        **specs,
    )(q, q, kv)


def _win_kernel(sink_ref, q_ref, qn_ref, kv_ref, o_ref, s_ref, *, layer, blk_off):
    low_half = _low_half_mask()
    nh = len(_ATT_HEADS)
    nk = CTX_LEN + QB + 2 * BLOCK

    def window_loader(n):
        left = jnp.maximum(2 * n - 1, 0)
        right = jnp.minimum(2 * n + 2, LAT_BLKS - 1)
        s_left = pl.multiple_of(CTX_LEN + left * BLOCK, BLOCK)
        s_mid = pl.multiple_of(CTX_LEN + n * QB, QB)
        s_right = pl.multiple_of(CTX_LEN + right * BLOCK, BLOCK)
        cache = {}

        def load(c0, w):
            if (c0, w) not in cache:
                cache[(c0, w)] = jnp.concatenate([kv_ref[0, 0:CTX_LEN, c0:c0 + w],
                                                  kv_ref[0, pl.ds(s_left, BLOCK), c0:c0 + w],
                                                  kv_ref[0, pl.ds(s_mid, QB), c0:c0 + w],
                                                  kv_ref[0, pl.ds(s_right, BLOCK), c0:c0 + w]], axis=0)
            return cache[(c0, w)]

        return load

    def window_mask(n):
        col = lax.broadcasted_iota(jnp.int32, (QB, nk), 1)
        qi = lax.broadcasted_iota(jnp.int32, (QB, nk), 0)
        jj = col - CTX_LEN
        dlt = jj - qi
        lo = jnp.where(n >= 1, 0, BLOCK)
        hi = jnp.where(n <= SEQ // QB - 2, QB + 2 * BLOCK, QB + BLOCK)
        in_win = (dlt >= 0) & (dlt <= 2 * BLOCK) & (jj >= lo) & (jj < hi)
        return (col < CTX_LEN) | in_win

    def scores(qr, head, load):
        c, use_b, kb, _ = head
        qs = _head_queries(qr, c, use_b, low_half)
        return lax.dot_general(qs, load(kb * LANES, LANES), _NT, preferred_element_type=F32)

    def finish(s, head, load, valid):
        c, use_b, _, vb = head
        if valid is not None:
            s = jnp.where(valid, s, NEG)
        sk = sink_ref[layer, 2 * c + use_b] * LOG2E
        mx = jnp.maximum(jnp.max(s, axis=-1, keepdims=True), sk)
        p = jnp.exp2(s - mx).astype(BF16)
        v = load((vb - use_b) * LANES, 2 * LANES)
        return _pv(p, lambda r0, r1: v[r0:r1], v.shape[0]), jnp.exp2(sk - mx)

    def store(done):
        _store_heads(o_ref, [r for r, _ in done], low_half, [t for _, t in done])

    def prologue(qr, n):
        load = window_loader(n)
        for a in range(WIN_AHEAD):
            s_ref[a] = scores(qr, _ATT_HEADS[a], load)

    def ctx_block():
        def load(c0, w):
            return kv_ref[0, 0:CTX_LEN, c0:c0 + w]

        store([finish(scores(q_ref, head, load), head, load, None) for head in _ATT_HEADS])
        prologue(qn_ref, 0)

    def latent_block(n):
        load = window_loader(n)
        load_next = window_loader(jnp.minimum(n + 1, SEQ // QB - 1))
        valid = window_mask(n)
        ahead = [s_ref[a] for a in range(WIN_AHEAD)]
        done = []
        for i, head in enumerate(_ATT_HEADS):
            m = i + WIN_AHEAD
            ahead.append(scores(q_ref, _ATT_HEADS[m], load) if m < nh else scores(qn_ref, _ATT_HEADS[m - nh], load_next))
            done.append(finish(ahead.pop(0), head, load, valid))
        for a in range(WIN_AHEAD):
            s_ref[a] = ahead[a]
        store(done)

    blk = pl.program_id(1)
    if blk_off == 0:
        pl.when(blk == 0)(ctx_block)
        pl.when(blk > 0)(lambda: latent_block(blk - 1))
    else:
        pl.when(blk == 0)(lambda: prologue(q_ref, 0))
        latent_block(blk)


def _win_call(layer, sink, q, kv, need_ctx):
    off = 0 if need_ctx else 1
    specs = _attn_specs(off)
    specs["in_specs"] = [pl.BlockSpec(memory_space=pltpu.SMEM)] + specs["in_specs"]
    return pl.pallas_call(
        functools.partial(_win_kernel, layer=layer, blk_off=off),
        grid=(BATCH, NQB - off),
        scratch_shapes=[pltpu.VMEM((WIN_AHEAD, QB, CTX_LEN + QB + 2 * BLOCK), F32)],
        compiler_params=_params("parallel", "arbitrary"),
        name="window_attn",
        **specs,
    )(sink, q, q, kv)


def _merge_kernel(*refs, n_x, has_ctx, final):
    x_refs = refs[:n_x]
    (mod_ref, modc_ref, nw_ref, or_ref, ow_ref, oa_ref, win_ref, wpr_ref, wpw_ref, wpa_ref,
     wo_ref, fw_ref, out_ref) = refs[n_x:]
    nsub = out_ref.shape[1] // SUB
    branches = ((or_ref, wpr_ref, _RG), (ow_ref, wpw_ref, _WG), (oa_ref, wpa_ref, _AG))

    def from_h(sb):
        mod = _sub_mod(mod_ref, None if final else modc_ref, sb)
        x = _read_x(x_refs, sb, has_ctx)
        h = _modulated_norm(x, mod, nw_ref)
        gates = [jnp.dot(h, win_ref[:, g0:g0 + 512], preferred_element_type=F32) for _, _, g0 in branches]
        logits = [jnp.dot(h, win_ref[:, _MG + j * D_MODEL:_MG + (j + 1) * D_MODEL], preferred_element_type=F32)
                  for j in range(3)]
        return x, mod, gates, logits

    def finish(sb, x, mod, gates, logits):
        rows = slice(sb * SUB, (sb + 1) * SUB)
        mix = None
        for (o_ref, wp_ref, _), g, ml in zip(branches, gates, logits):
            u = (o_ref[0, rows, :].astype(F32) * (g * _sigmoid(g))).astype(BF16)
            t = _sigmoid(ml) * jnp.dot(u, wp_ref[...], preferred_element_type=F32)
            mix = t if mix is None else mix + t
        y = jnp.dot(mix.astype(BF16), wo_ref[...], preferred_element_type=F32)
        xn = x + mod[:, 2 * D_MODEL:3 * D_MODEL] * y
        if final:
            xn = xn * lax.rsqrt(jnp.mean(xn * xn, axis=-1, keepdims=True) + EPS) * fw_ref[...]
        out_ref[0, rows, :] = xn

    cur = from_h(0)
    for sb in range(nsub):
        nxt = from_h(sb + 1) if sb + 1 < nsub else None
        finish(sb, *cur)
        cur = nxt


def _merge_call(layer, xx, mod, nw, o_ret, o_win, o_ax, w_in_bf, wpr, wpw, wpa, wo, fw, final):
    tm = TM_FINAL if final else TM
    n_tok = SEQ if final else TOK
    o_spec = pl.BlockSpec((1, tm, 512), lambda b, t: (b, t, 0))
    x_ops, x_specs, has_ctx = _x_operands(xx, tm, latent_only=final)
    return pl.pallas_call(
        functools.partial(_merge_kernel, n_x=len(x_ops), has_ctx=has_ctx, final=final),
        grid=(BATCH, n_tok // tm),
        in_specs=[*x_specs,
                  *_mod_specs(layer),
                  _layer_spec((1, D_MODEL), layer),
                  o_spec, o_spec, o_spec,
                  _layer_spec((D_MODEL, IN_W), layer),
                  _layer_spec((512, D_MODEL), layer), _layer_spec((512, D_MODEL), layer),
                  _layer_spec((512, D_MODEL), layer),
                  _layer_spec((D_MODEL, D_MODEL), layer),
                  _const_spec((1, D_MODEL))],
        out_specs=pl.BlockSpec((1, tm, D_MODEL), lambda b, t: (b, t, 0)),
        out_shape=jax.ShapeDtypeStruct((BATCH, n_tok, D_MODEL), F32),
        compiler_params=_params("parallel", "arbitrary"),
        name="merge_final" if final else "merge",
    )(*x_ops, mod, mod, nw, o_ret, o_win, o_ax, w_in_bf, wpr, wpw, wpa, wo, fw)


def _rope_tables():
    t = jnp.arange(TOK, dtype=F32)
    theta = ROPE_BASE ** (-jnp.linspace(0.0, 1.0, RET_DK // 2, dtype=F32))
    pos_b = jnp.where(t < CTX_LEN, CTX_LEN - 1.0 - t, 2.0 * CTX_LEN + SEQ - 1.0 - t)

    def pair_tables(ang):
        c, s = jnp.cos(ang), jnp.sin(ang)
        return jnp.repeat(c, 2, axis=-1), jnp.stack([-s, s], axis=-1).reshape(ang.shape[0], -1)

    cf, sf = pair_tables(t[:, None] * theta[None])
    cb, sb = pair_tables(pos_b[:, None] * theta[None])

    s_idx = jnp.arange(SEQ)
    quarter = HEAD_DIM // 4
    freqs = ROPE_BASE ** (-jnp.arange(quarter, dtype=F32) / quarter)
    r = (s_idx // GRID_W).astype(F32)
    col = (s_idx % GRID_W).astype(F32)
    ang = jnp.concatenate([r[:, None] * freqs[None], col[:, None] * freqs[None]], axis=-1)
    c, s = pair_tables(ang)
    ca = jnp.concatenate([jnp.ones((CTX_LEN, LANES), F32), jnp.concatenate([c, c], axis=-1)], axis=0)
    sa = jnp.concatenate([jnp.zeros((CTX_LEN, LANES), F32), jnp.concatenate([s, s], axis=-1)], axis=0)
    return cf, sf, cb, sb, ca, sa


def _segment_matrix():
    head = np.arange(2 * LANES) // HEAD_DIM
    return jnp.asarray((head[:, None] == head[None, :]).astype(np.float32), dtype=BF16)


def kernel(x, c, ctx, c_ctx, norm_w, w_mod, b_mod, w_in, ret_decay_fwd, ret_decay_bwd, win_sink,
           ax_q_gain, ax_k_gain, w_proj_ret, w_proj_win, w_proj_ax, w_out, final_norm_w):
    xx = (ctx, x)
    cvec = jnp.zeros((24, D_MODEL), F32).at[:BATCH].set(c).at[BATCH].set(c_ctx)
    mod = _mod_call(cvec, w_mod, b_mod).reshape(DEPTH, 24, 1, 3 * D_MODEL)
    tabs = _rope_tables()
    seg = _segment_matrix()
    w_in_bf = w_in.astype(BF16)
    w_wkv = _kv_weights(w_in_bf, _WK, _WV)
    w_akv = _kv_weights(w_in_bf, _AK, _AV)
    w_aq = w_in_bf[:, :, _AQ:_AQ + 512]
    wpr, wpw, wpa, wo = (w.astype(BF16) for w in (w_proj_ret, w_proj_win, w_proj_ax, w_out))
    nw = norm_w.reshape(DEPTH, 1, D_MODEL)
    gq = jnp.tile(ax_q_gain, (1, 2)).reshape(DEPTH, 1, LANES)
    gk = jnp.tile(ax_k_gain, (1, 2)).reshape(DEPTH, 1, LANES)
    a_f = jnp.broadcast_to(ret_decay_fwd[:, :, None, None], (DEPTH, RET_HEADS, 8, LANES))
    a_b = jnp.broadcast_to(ret_decay_bwd[:, :, None, None], (DEPTH, RET_HEADS, 8, LANES))
    fw = final_norm_w.reshape(1, D_MODEL)

    for l in range(DEPTH):
        need_ctx = l < DEPTH - 1
        qf, qb, kf, kb, rv, wq, wkv, aq, akv = _qkv_call(l, xx, mod, nw, w_in_bf, w_wkv, w_aq, w_akv, tabs, gq, gk, seg)
        o_ret = _ret_call(l, a_f, a_b, qf, qb, kf, kb, rv, need_ctx)
        o_win = _win_call(l, win_sink, wq, wkv, need_ctx)
        o_ax = _ax_call(aq, akv, need_ctx)
        xx = _merge_call(l, xx, mod, nw, o_ret, o_win, o_ax, w_in_bf, wpr, wpw, wpa, wo, fw, final=not need_ctx)
    return xx
```

```python
import functools

import numpy as np
import jax
import jax.numpy as jnp
from jax import lax
from jax.experimental import pallas as pl
from jax.experimental.pallas import tpu as pltpu

D_MODEL = 1024
BATCH = 16
SEQ = 2048
DEPTH = 4
CTX_LEN = 256
TOK = CTX_LEN + SEQ
GRID_W = 64
BLOCK = 128
RET_HEADS = 4
RET_DK = 128
HEAD_DIM = 64
ROPE_BASE = 10000.0
EPS = 1e-6
NEG = -1e30

LANES = 128
NBLK = TOK // BLOCK
CTX_BLKS = CTX_LEN // BLOCK
LAT_BLKS = SEQ // BLOCK
LOG2E = 1.4426950408889634
KV_W = 6 * LANES
QB = CTX_LEN
NQB = TOK // QB
AX_AHEAD = 2
WIN_AHEAD = 2
PV_KC = 256

SUB = CTX_LEN
TM = 3 * SUB
TM_FINAL = 4 * SUB
VMEM_LIMIT = 52 * 1024 * 1024

F32 = jnp.float32
BF16 = jnp.bfloat16

_IN_SIZES = (512, 512, 512, 512, 512, 128, 128, 512, 512, 128, 128, 512, 3 * D_MODEL)
_OFF = np.concatenate([[0], np.cumsum(_IN_SIZES)]).astype(np.int64)
(_RQ, _RK, _RV, _RG, _WQ, _WK, _WV, _WG, _AQ, _AK, _AV, _AG, _MG) = [int(o) for o in _OFF[:-1]]

IN_W = int(_OFF[-1])


def _kv_weights(w, k0, v0):
    k, v = w[:, :, k0:k0 + LANES], w[:, :, v0:v0 + LANES]

    def swapped(x):
        return jnp.concatenate([x[..., HEAD_DIM:], x[..., :HEAD_DIM]], axis=-1)

    return jnp.concatenate([k, swapped(k), v, swapped(v)], axis=-1)

_ATT_HEADS = tuple((c, b, (c // 2 + b) % 2, 2 + 2 * (c // 2) + b) for c in range(4) for b in (0, 1))

_NT = (((1,), (1,)), ((), ()))
_TN = (((0,), (0,)), ((), ()))


def _const_spec(shape):
    nd = len(shape)
    return pl.BlockSpec(shape, lambda *_: (0,) * nd, pipeline_mode=pl.Buffered(1))


def _layer_spec(shape, layer):
    nd = len(shape)
    return pl.BlockSpec((None, *shape), lambda *_: (layer,) + (0,) * nd, pipeline_mode=pl.Buffered(1))


def _params(*sem):
    return pltpu.CompilerParams(dimension_semantics=sem, vmem_limit_bytes=VMEM_LIMIT)


def _sigmoid(z):
    return 1.0 / (1.0 + jnp.exp(-z))


def _mod_kernel(c_ref, w_ref, b_ref, o_ref):
    cv = c_ref[...]
    s = cv * _sigmoid(cv)
    o_ref[0] = jnp.dot(s, w_ref[0], preferred_element_type=F32,
                       precision=lax.Precision.HIGHEST) + b_ref[0]


def _mod_call(cvec, w_mod, b_mod):
    rows = cvec.shape[0]
    return pl.pallas_call(
        _mod_kernel,
        grid=(DEPTH, 3),
        in_specs=[pl.BlockSpec((rows, D_MODEL), lambda l, j: (0, 0)),
                  pl.BlockSpec((1, D_MODEL, D_MODEL), lambda l, j: (l, 0, j)),
                  pl.BlockSpec((1, 1, D_MODEL), lambda l, j: (l, 0, j))],
        out_specs=pl.BlockSpec((1, rows, D_MODEL), lambda l, j: (l, 0, j)),
        out_shape=jax.ShapeDtypeStruct((DEPTH, rows, 3 * D_MODEL), F32),
        compiler_params=_params("arbitrary", "arbitrary"),
        name="adaln_mod",
    )(cvec, w_mod, b_mod.reshape(DEPTH, 1, 3 * D_MODEL))


def _sub_mod(mod_ref, modc_ref, j):
    if modc_ref is None or j > 0:
        return mod_ref[0]
    return jnp.where(pl.program_id(1) == 0, modc_ref[0], mod_ref[0])


def _x_operands(xx, tm, latent_only=False):
    nsub = tm // SUB
    if not isinstance(xx, tuple):
        if not latent_only:
            return [xx], [pl.BlockSpec((1, tm, D_MODEL), lambda b, t: (b, t, 0))], False
        specs = [pl.BlockSpec((1, SUB, D_MODEL), lambda b, t, sb=sb: (b, nsub * t + sb + 1, 0)) for sb in range(nsub)]
        return [xx] * nsub, specs, False
    ctx, x = xx
    specs = [pl.BlockSpec((1, SUB, D_MODEL), lambda b, t: (b, 0, 0))]
    for sb in range(nsub):
        specs.append(pl.BlockSpec((1, SUB, D_MODEL), lambda b, t, sb=sb: (b, jnp.maximum(nsub * t + sb - 1, 0), 0)))
    return [ctx] + [x] * nsub, specs, True


def _read_x(x_refs, sb, has_ctx):
    if len(x_refs) == 1:
        return x_refs[0][0, sb * SUB:(sb + 1) * SUB, :]
    if not has_ctx:
        return x_refs[sb][0]
    if sb == 0:
        return jnp.where(pl.program_id(1) == 0, x_refs[0][0], x_refs[1][0])
    return x_refs[1 + sb][0]


def _modulated_norm(x, mod, nw_ref):
    ms = jnp.mean(x * x, axis=-1, keepdims=True)
    y = x * lax.rsqrt(ms + EPS) * nw_ref[...]
    return (y * (1.0 + mod[:, D_MODEL:2 * D_MODEL]) + mod[:, 0:D_MODEL]).astype(BF16)


def _qkv_kernel(*refs, n_x, has_ctx):
    x_refs = refs[:n_x]
    (mod_ref, modc_ref, nw_ref, w_rq, w_rk, w_rv, w_wq, w_wkv, w_aq, w_akv,
     cf_ref, sf_ref, cb_ref, sb_ref, ca_ref, sa_ref, gq_ref, gk_ref, seg_ref,
     qf_ref, qb_ref, kf_ref, kb_ref, rv_ref, wq_ref, wkv_ref, aq_ref, akv_ref) = refs[n_x:]
    k_scale = RET_DK ** -0.5
    q_scale = HEAD_DIM ** -0.5 * LOG2E
    lane = lax.broadcasted_iota(jnp.int32, (SUB, LANES), 1)
    low_half = lane < HEAD_DIM
    even_lane = (lane & 1) == 0

    def lanes(a, j):
        return a[:, j * LANES:(j + 1) * LANES]

    def qk_norm(acc, j0, g_ref):
        v2 = acc[:, j0 * LANES:(j0 + 2) * LANES]
        ss = jnp.dot((v2 * v2).astype(BF16), seg_ref[...], preferred_element_type=F32)
        r = lax.rsqrt(ss * (1.0 / HEAD_DIM) + EPS)
        return [lanes(v2, i) * lanes(r, i) * g_ref[...] for i in range(2)]

    for sb in range(qf_ref.shape[1] // SUB):
        rows = slice(sb * SUB, (sb + 1) * SUB)
        h = _modulated_norm(_read_x(x_refs, sb, has_ctx), _sub_mod(mod_ref, modc_ref, sb), nw_ref)

        def proj(w_ref):
            return jnp.dot(h, w_ref[...], preferred_element_type=F32)

        def rope(v, c_ref, s_ref):
            partner = jnp.where(even_lane, pltpu.roll(v, LANES - 1, 1), pltpu.roll(v, 1, 1))
            return v * c_ref[rows, :] + partner * s_ref[rows, :]

        def put(ref, j, val):
            ref[0, rows, j * LANES:(j + 1) * LANES] = val.astype(BF16)

        def store_kv(kv_ref, acc):
            va, vb = lanes(acc, 2), lanes(acc, 3)
            put(kv_ref, 2, jnp.where(low_half, va, 1.0))
            put(kv_ref, 3, jnp.where(low_half, 1.0, vb))
            put(kv_ref, 4, jnp.where(low_half, vb, 1.0))
            put(kv_ref, 5, jnp.where(low_half, 1.0, va))

        acc = proj(w_aq)
        for j, qn in enumerate(qk_norm(acc, 0, gq_ref) + qk_norm(acc, 2, gq_ref)):
            put(aq_ref, j, rope(qn, ca_ref, sa_ref) * q_scale)
        acc = proj(w_akv)
        for j, kn in enumerate(qk_norm(acc, 0, gk_ref)):
            put(akv_ref, j, rope(kn, ca_ref, sa_ref))
        store_kv(akv_ref, acc)

        acc = proj(w_rq)
        for j in range(4):
            put(qf_ref, j, rope(lanes(acc, j), cf_ref, sf_ref))
            put(qb_ref, j, rope(lanes(acc, j), cb_ref, sb_ref))
        acc = proj(w_rk)
        for j in range(4):
            put(kf_ref, j, rope(lanes(acc, j), cf_ref, sf_ref) * k_scale)
            put(kb_ref, j, rope(lanes(acc, j), cb_ref, sb_ref) * k_scale)
        rv_ref[0, rows, :] = proj(w_rv).astype(BF16)

        acc = proj(w_wq)
        for j in range(4):
            put(wq_ref, j, rope(lanes(acc, j), ca_ref, sa_ref) * q_scale)
        acc = proj(w_wkv)
        for j in range(2):
            put(wkv_ref, j, rope(lanes(acc, j), ca_ref, sa_ref))
        store_kv(wkv_ref, acc)


def _mod_specs(layer):
    return [pl.BlockSpec((None, 1, 1, 3 * D_MODEL), lambda b, t: (layer, b, 0, 0)),
            pl.BlockSpec((None, 1, 1, 3 * D_MODEL), lambda b, t: (layer, BATCH, 0, 0))]


def _qkv_call(layer, xx, mod, nw, w_in_bf, w_wkv, w_aq, w_akv, tabs, gq, gk, seg):
    w_spec = _layer_spec((D_MODEL, 512), layer)

    def col_spec(c0):
        return pl.BlockSpec((None, D_MODEL, 512), lambda *_: (layer, 0, c0 // 512), pipeline_mode=pl.Buffered(1))

    tok_spec = pl.BlockSpec((1, TM, 512), lambda b, t: (b, t, 0))
    tab_spec = pl.BlockSpec((TM, LANES), lambda b, t: (t, 0))
    out_sds = jax.ShapeDtypeStruct((BATCH, TOK, 512), BF16)
    kv_spec = pl.BlockSpec((1, TM, KV_W), lambda b, t: (b, t, 0))
    kv_sds = jax.ShapeDtypeStruct((BATCH, TOK, KV_W), BF16)
    x_ops, x_specs, has_ctx = _x_operands(xx, TM)
    return pl.pallas_call(
        functools.partial(_qkv_kernel, n_x=len(x_ops), has_ctx=has_ctx),
        grid=(BATCH, TOK // TM),
        in_specs=[*x_specs,
                  *_mod_specs(layer),
                  _layer_spec((1, D_MODEL), layer),
                  col_spec(_RQ), col_spec(_RK), col_spec(_RV), col_spec(_WQ), w_spec, w_spec, w_spec,
                  tab_spec, tab_spec, tab_spec, tab_spec, tab_spec, tab_spec,
                  _layer_spec((1, LANES), layer), _layer_spec((1, LANES), layer),
                  _const_spec((2 * LANES, 2 * LANES))],
        out_specs=[tok_spec] * 6 + [kv_spec, tok_spec, kv_spec],
        out_shape=[out_sds] * 6 + [kv_sds, out_sds, kv_sds],
        compiler_params=_params("parallel", "arbitrary"),
        name="qkv_proj",
    )(*x_ops, mod, mod, nw, w_in_bf, w_in_bf, w_in_bf, w_in_bf, w_wkv, w_aq, w_akv, *tabs, gq, gk, seg)


def _log_sigmoid(a):
    return jnp.minimum(a, 0.0) - jnp.log1p(jnp.exp(-jnp.abs(a)))


def _ret_kernel(af_ref, ab_ref, qf_ref, qb_ref, kf_ref, kb_ref, v_ref, o_ref, acc_ref, st_ref, dec_ref,
                *, need_ctx):
    ri = lax.broadcasted_iota(jnp.int32, (BLOCK, BLOCK), 0).astype(F32)
    ci = lax.broadcasted_iota(jnp.int32, (BLOCK, BLOCK), 1).astype(F32)
    d = ri - ci
    for h in range(RET_HEADS):
        lgf = _log_sigmoid(af_ref[h])[0:1, :]
        lgb = _log_sigmoid(ab_ref[h])[0:1, :]
        dec_ref[2 * h, 0] = jnp.where(d >= 0, jnp.exp(jnp.maximum(d, 0.0) * lgf), 0.0)
        dec_ref[2 * h, 1] = jnp.exp((ri + 1.0) * lgf)
        dec_ref[2 * h, 2] = jnp.exp((BLOCK - 1.0 - ri) * lgf)
        dec_ref[2 * h, 3] = jnp.exp(0.0 * ri + float(BLOCK) * lgf)
        dec_ref[2 * h + 1, 0] = jnp.where(d < 0, jnp.exp(jnp.maximum(-d, 0.0) * lgb), 0.0)
        dec_ref[2 * h + 1, 1] = jnp.exp((float(BLOCK) - ri) * lgb)
        dec_ref[2 * h + 1, 2] = jnp.exp(ri * lgb)
        dec_ref[2 * h + 1, 3] = jnp.exp(0.0 * ri + float(BLOCK) * lgb)
    st_ref[...] = jnp.zeros_like(st_ref)
    first_row = 0 if need_ctx else CTX_LEN

    def visit(i, in_ctx, first, want_o):
        cf = i
        cb = (CTX_BLKS - 1 - i) if in_ctx else (NBLK + CTX_BLKS - 1 - i)
        pending = []
        for h in range(RET_HEADS):
            lanes_h = slice(h * LANES, (h + 1) * LANES)
            for dr, (q_ref, k_ref, c) in enumerate(((qf_ref, kf_ref, cf), (qb_ref, kb_ref, cb))):
                r0 = c * BLOCK if in_ctx else pl.multiple_of(c * BLOCK, BLOCK)
                j = 2 * h + dr
                k = k_ref[0, pl.ds(r0, BLOCK), lanes_h]
                v = v_ref[0, pl.ds(r0, BLOCK), lanes_h]
                state = st_ref[j]
                kd = (k.astype(F32) * dec_ref[j, 2]).astype(BF16)
                st_ref[j] = dec_ref[j, 3] * state + lax.dot_general(kd, v, _TN, preferred_element_type=F32)
                if want_o:
                    q = q_ref[0, pl.ds(r0, BLOCK), lanes_h]
                    s = lax.dot_general(q, k, _NT, preferred_element_type=F32)
                    cross = jnp.dot(q, state.astype(BF16), preferred_element_type=F32)
                    pending.append((j, r0, lanes_h, v, s, cross))
        for j, r0, lanes_h, v, s, cross in pending:
            o = jnp.dot((s * dec_ref[j, 0]).astype(BF16), v, preferred_element_type=F32) + cross * dec_ref[j, 1]
            if first:
                acc_ref[pl.ds(r0, BLOCK), lanes_h] = o
            else:
                tot = acc_ref[pl.ds(r0, BLOCK), lanes_h] + o
                tot = tot * lax.rsqrt(jnp.mean(tot * tot, axis=-1, keepdims=True) + EPS)
                o_ref[0, pl.ds(r0 - first_row, BLOCK), lanes_h] = tot.astype(o_ref.dtype)

    for i in range(CTX_BLKS):
        visit(i, True, i < CTX_BLKS // 2, need_ctx)
    half = (NBLK + CTX_BLKS) // 2

    def first_pass(i, carry):
        visit(i, False, True, True)
        return carry

    def second_pass(i, carry):
        visit(i, False, False, True)
        return carry

    lax.fori_loop(CTX_BLKS, half, first_pass, 0)
    lax.fori_loop(half, NBLK, second_pass, 0)


def _ret_call(layer, a_f, a_b, qf, qb, kf, kb, rv, need_ctx):
    n_out = TOK if need_ctx else SEQ
    tok_spec = pl.BlockSpec((1, TOK, 512), lambda b: (b, 0, 0))
    a_spec = _layer_spec((RET_HEADS, 8, LANES), layer)
    return pl.pallas_call(
        functools.partial(_ret_kernel, need_ctx=need_ctx),
        grid=(BATCH,),
        in_specs=[a_spec, a_spec, tok_spec, tok_spec, tok_spec, tok_spec, tok_spec],
        out_specs=pl.BlockSpec((1, n_out, 512), lambda b: (b, 0, 0)),
        out_shape=jax.ShapeDtypeStruct((BATCH, n_out, 512), BF16),
        scratch_shapes=[pltpu.VMEM((TOK, 512), F32),
                        pltpu.VMEM((2 * RET_HEADS, RET_DK, LANES), F32),
                        pltpu.VMEM((2 * RET_HEADS, 4, BLOCK, BLOCK), F32)],
        compiler_params=_params("parallel"),
        name="retention",
    )(a_f, a_b, qf, qb, kf, kb, rv)


def _low_half_mask():
    return lax.broadcasted_iota(jnp.int32, (QB, LANES), 1) < HEAD_DIM


def _head_queries(q_ref, c, use_b, low_half):
    qv = q_ref[0, :, c * LANES:(c + 1) * LANES].astype(F32)
    qv = jnp.where(low_half, 0.0, qv) if use_b else jnp.where(low_half, qv, 0.0)
    return qv.astype(BF16)


def _pv(p, load_v, nkeys):
    acc = None
    for r0 in range(0, nkeys, PV_KC):
        d = jnp.dot(p[:, r0:r0 + PV_KC], load_v(r0, r0 + PV_KC), preferred_element_type=F32)
        acc = d if acc is None else acc + d
    return acc


def _store_heads(o_ref, res, low_half, extra=None):
    for c in range(4):
        ra, rb = res[2 * c], res[2 * c + 1]
        num = jnp.where(low_half, ra[:, :LANES], rb[:, LANES:])
        den = jnp.where(low_half, ra[:, LANES:], rb[:, :LANES])
        if extra is not None:
            den = den + jnp.where(low_half, extra[2 * c], extra[2 * c + 1])
        o_ref[0, :, c * LANES:(c + 1) * LANES] = (num * (1.0 / den)).astype(o_ref.dtype)


def _attn_specs(off):
    return dict(
        in_specs=[pl.BlockSpec((1, QB, 512), lambda b, i: (b, i + off, 0)),
                  pl.BlockSpec((1, QB, 512), lambda b, i: (b, jnp.minimum(i + off + 1, NQB - 1), 0)),
                  pl.BlockSpec((1, TOK, KV_W), lambda b, i: (b, 0, 0))],
        out_specs=pl.BlockSpec((1, QB, 512), lambda b, i: (b, i, 0)),
        out_shape=jax.ShapeDtypeStruct((BATCH, (NQB - off) * QB, 512), BF16),
    )


def _ax_kernel(q_ref, qn_ref, kv_ref, o_ref, s_ref, m_ref, *, blk_off):
    low_half = _low_half_mask()

    def scores(qr, head, nkeys):
        c, use_b, kb, _ = head
        qs = _head_queries(qr, c, use_b, low_half)
        k = kv_ref[0, 0:nkeys, kb * LANES:(kb + 1) * LANES]
        return lax.dot_general(qs, k, _NT, preferred_element_type=F32)

    def carry(slot, s):
        s_ref[slot] = s
        m_ref[slot] = jnp.broadcast_to(jnp.max(s, axis=-1, keepdims=True), m_ref.shape[1:])

    def finish(s, head, nkeys, mx=None):
        _, use_b, _, vb = head
        if mx is None:
            mx = jnp.max(s, axis=-1, keepdims=True)
        p = jnp.exp2(s - mx).astype(BF16)
        v0 = (vb - use_b) * LANES
        return _pv(p, lambda r0, r1: kv_ref[0, r0:r1, v0:v0 + 2 * LANES], nkeys)

    nh = len(_ATT_HEADS)

    def prologue(qr):
        for a in range(AX_AHEAD):
            carry(a, scores(qr, _ATT_HEADS[a], TOK))

    def ctx_block():
        res = [finish(scores(q_ref, head, CTX_LEN), head, CTX_LEN) for head in _ATT_HEADS]
        _store_heads(o_ref, res, low_half)
        prologue(qn_ref)

    def latent_block():
        res = []
        ahead = [(s_ref[a], m_ref[a, :, 0:1]) for a in range(AX_AHEAD)]
        for n, head in enumerate(_ATT_HEADS):
            m = n + AX_AHEAD
            ahead.append((scores(q_ref, _ATT_HEADS[m], TOK) if m < nh else scores(qn_ref, _ATT_HEADS[m - nh], TOK), None))
            s, mx = ahead.pop(0)
            res.append(finish(s, head, TOK, mx))
        for a in range(AX_AHEAD):
            carry(a, ahead[a][0])
        _store_heads(o_ref, res, low_half)

    blk = pl.program_id(1)
    if blk_off == 0:
        pl.when(blk == 0)(ctx_block)
        pl.when(blk > 0)(latent_block)
    else:
        pl.when(blk == 0)(lambda: prologue(q_ref))
        latent_block()


def _ax_call(q, kv, need_ctx):
    off = 0 if need_ctx else 1
    specs = _attn_specs(off)
    return pl.pallas_call(
        functools.partial(_ax_kernel, blk_off=off),
        grid=(BATCH, NQB - off),
        scratch_shapes=[pltpu.VMEM((AX_AHEAD, QB, TOK), F32), pltpu.VMEM((AX_AHEAD, QB, LANES), F32)],
        compiler_params=_params("parallel", "arbitrary"),
        name="axial_attn",
        **specs,
    )(q, q, kv)


def _win_kernel(sink_ref, q_ref, qn_ref, kv_ref, o_ref, s_ref, *, layer, blk_off):
    low_half = _low_half_mask()
    nh = len(_ATT_HEADS)
    nk = CTX_LEN + QB + 2 * BLOCK

    def window_loader(n):
        left = jnp.maximum(2 * n - 1, 0)
        right = jnp.minimum(2 * n + 2, LAT_BLKS - 1)
        s_left = pl.multiple_of(CTX_LEN + left * BLOCK, BLOCK)
        s_mid = pl.multiple_of(CTX_LEN + n * QB, QB)
        s_right = pl.multiple_of(CTX_LEN + right * BLOCK, BLOCK)
        cache = {}

        def load(c0, w):
            if (c0, w) not in cache:
                cache[(c0, w)] = jnp.concatenate([kv_ref[0, 0:CTX_LEN, c0:c0 + w],
                                                  kv_ref[0, pl.ds(s_left, BLOCK), c0:c0 + w],
                                                  kv_ref[0, pl.ds(s_mid, QB), c0:c0 + w],
                                                  kv_ref[0, pl.ds(s_right, BLOCK), c0:c0 + w]], axis=0)
            return cache[(c0, w)]

        return load

    def window_mask(n):
        col = lax.broadcasted_iota(jnp.int32, (QB, nk), 1)
        qi = lax.broadcasted_iota(jnp.int32, (QB, nk), 0)
        jj = col - CTX_LEN
        dlt = jj - qi
        lo = jnp.where(n >= 1, 0, BLOCK)
        hi = jnp.where(n <= SEQ // QB - 2, QB + 2 * BLOCK, QB + BLOCK)
        in_win = (dlt >= 0) & (dlt <= 2 * BLOCK) & (jj >= lo) & (jj < hi)
        return (col < CTX_LEN) | in_win

    def scores(qr, head, load):
        c, use_b, kb, _ = head
        qs = _head_queries(qr, c, use_b, low_half)
        return lax.dot_general(qs, load(kb * LANES, LANES), _NT, preferred_element_type=F32)

    def finish(s, head, load, valid):
        c, use_b, _, vb = head
        if valid is not None:
            s = jnp.where(valid, s, NEG)
        sk = sink_ref[layer, 2 * c + use_b] * LOG2E
        mx = jnp.maximum(jnp.max(s, axis=-1, keepdims=True), sk)
        p = jnp.exp2(s - mx).astype(BF16)
        v = load((vb - use_b) * LANES, 2 * LANES)
        return _pv(p, lambda r0, r1: v[r0:r1], v.shape[0]), jnp.exp2(sk - mx)

    def store(done):
        _store_heads(o_ref, [r for r, _ in done], low_half, [t for _, t in done])

    def prologue(qr, n):
        load = window_loader(n)
        for a in range(WIN_AHEAD):
            s_ref[a] = scores(qr, _ATT_HEADS[a], load)

    def ctx_block():
        def load(c0, w):
            return kv_ref[0, 0:CTX_LEN, c0:c0 + w]

        store([finish(scores(q_ref, head, load), head, load, None) for head in _ATT_HEADS])
        prologue(qn_ref, 0)

    def latent_block(n):
        load = window_loader(n)
        load_next = window_loader(jnp.minimum(n + 1, SEQ // QB - 1))
        valid = window_mask(n)
        ahead = [s_ref[a] for a in range(WIN_AHEAD)]
        done = []
        for i, head in enumerate(_ATT_HEADS):
            m = i + WIN_AHEAD
            ahead.append(scores(q_ref, _ATT_HEADS[m], load) if m < nh else scores(qn_ref, _ATT_HEADS[m - nh], load_next))
            done.append(finish(ahead.pop(0), head, load, valid))
        for a in range(WIN_AHEAD):
            s_ref[a] = ahead[a]
        store(done)

    blk = pl.program_id(1)
    if blk_off == 0:
        pl.when(blk == 0)(ctx_block)
        pl.when(blk > 0)(lambda: latent_block(blk - 1))
    else:
        pl.when(blk == 0)(lambda: prologue(q_ref, 0))
        latent_block(blk)


def _win_call(layer, sink, q, kv, need_ctx):
    off = 0 if need_ctx else 1
    specs = _attn_specs(off)
    specs["in_specs"] = [pl.BlockSpec(memory_space=pltpu.SMEM)] + specs["in_specs"]
    return pl.pallas_call(
        functools.partial(_win_kernel, layer=layer, blk_off=off),
        grid=(BATCH, NQB - off),
        scratch_shapes=[pltpu.VMEM((WIN_AHEAD, QB, CTX_LEN + QB + 2 * BLOCK), F32)],
        compiler_params=_params("parallel", "arbitrary"),
        name="window_attn",
        **specs,
    )(sink, q, q, kv)


def _merge_kernel(*refs, n_x, has_ctx, final):
    x_refs = refs[:n_x]
    (mod_ref, modc_ref, nw_ref, or_ref, ow_ref, oa_ref, win_ref, wpr_ref, wpw_ref, wpa_ref,
     wo_ref, fw_ref, out_ref) = refs[n_x:]
    nsub = out_ref.shape[1] // SUB
    branches = ((or_ref, wpr_ref, _RG), (ow_ref, wpw_ref, _WG), (oa_ref, wpa_ref, _AG))

    def from_h(sb):
        mod = _sub_mod(mod_ref, None if final else modc_ref, sb)
        x = _read_x(x_refs, sb, has_ctx)
        h = _modulated_norm(x, mod, nw_ref)
        gates = [jnp.dot(h, win_ref[:, g0:g0 + 512], preferred_element_type=F32) for _, _, g0 in branches]
        logits = [jnp.dot(h, win_ref[:, _MG + j * D_MODEL:_MG + (j + 1) * D_MODEL], preferred_element_type=F32)
                  for j in range(3)]
        return x, mod, gates, logits

    def finish(sb, x, mod, gates, logits):
        rows = slice(sb * SUB, (sb + 1) * SUB)
        mix = None
        for (o_ref, wp_ref, _), g, ml in zip(branches, gates, logits):
            u = (o_ref[0, rows, :].astype(F32) * (g * _sigmoid(g))).astype(BF16)
            t = _sigmoid(ml) * jnp.dot(u, wp_ref[...], preferred_element_type=F32)
            mix = t if mix is None else mix + t
        y = jnp.dot(mix.astype(BF16), wo_ref[...], preferred_element_type=F32)
        xn = x + mod[:, 2 * D_MODEL:3 * D_MODEL] * y
        if final:
            xn = xn * lax.rsqrt(jnp.mean(xn * xn, axis=-1, keepdims=True) + EPS) * fw_ref[...]
        out_ref[0, rows, :] = xn

    cur = from_h(0)
    for sb in range(nsub):
        nxt = from_h(sb + 1) if sb + 1 < nsub else None
        finish(sb, *cur)
        cur = nxt


def _merge_call(layer, xx, mod, nw, o_ret, o_win, o_ax, w_in_bf, wpr, wpw, wpa, wo, fw, final):
    tm = TM_FINAL if final else TM
    n_tok = SEQ if final else TOK
    o_spec = pl.BlockSpec((1, tm, 512), lambda b, t: (b, t, 0))
    x_ops, x_specs, has_ctx = _x_operands(xx, tm, latent_only=final)
    return pl.pallas_call(
        functools.partial(_merge_kernel, n_x=len(x_ops), has_ctx=has_ctx, final=final),
        grid=(BATCH, n_tok // tm),
        in_specs=[*x_specs,
                  *_mod_specs(layer),
                  _layer_spec((1, D_MODEL), layer),
                  o_spec, o_spec, o_spec,
                  _layer_spec((D_MODEL, IN_W), layer),
                  _layer_spec((512, D_MODEL), layer), _layer_spec((512, D_MODEL), layer),
                  _layer_spec((512, D_MODEL), layer),
                  _layer_spec((D_MODEL, D_MODEL), layer),
                  _const_spec((1, D_MODEL))],
        out_specs=pl.BlockSpec((1, tm, D_MODEL), lambda b, t: (b, t, 0)),
        out_shape=jax.ShapeDtypeStruct((BATCH, n_tok, D_MODEL), F32),
        compiler_params=_params("parallel", "arbitrary"),
        name="merge_final" if final else "merge",
    )(*x_ops, mod, mod, nw, o_ret, o_win, o_ax, w_in_bf, wpr, wpw, wpa, wo, fw)


def _rope_tables():
    t = jnp.arange(TOK, dtype=F32)
    theta = ROPE_BASE ** (-jnp.linspace(0.0, 1.0, RET_DK // 2, dtype=F32))
    pos_b = jnp.where(t < CTX_LEN, CTX_LEN - 1.0 - t, 2.0 * CTX_LEN + SEQ - 1.0 - t)

    def pair_tables(ang):
        c, s = jnp.cos(ang), jnp.sin(ang)
        return jnp.repeat(c, 2, axis=-1), jnp.stack([-s, s], axis=-1).reshape(ang.shape[0], -1)

    cf, sf = pair_tables(t[:, None] * theta[None])
    cb, sb = pair_tables(pos_b[:, None] * theta[None])

    s_idx = jnp.arange(SEQ)
    quarter = HEAD_DIM // 4
    freqs = ROPE_BASE ** (-jnp.arange(quarter, dtype=F32) / quarter)
    r = (s_idx // GRID_W).astype(F32)
    col = (s_idx % GRID_W).astype(F32)
    ang = jnp.concatenate([r[:, None] * freqs[None], col[:, None] * freqs[None]], axis=-1)
    c, s = pair_tables(ang)
    ca = jnp.concatenate([jnp.ones((CTX_LEN, LANES), F32), jnp.concatenate([c, c], axis=-1)], axis=0)
    sa = jnp.concatenate([jnp.zeros((CTX_LEN, LANES), F32), jnp.concatenate([s, s], axis=-1)], axis=0)
    return cf, sf, cb, sb, ca, sa


def _segment_matrix():
    head = np.arange(2 * LANES) // HEAD_DIM
    return jnp.asarray((head[:, None] == head[None, :]).astype(np.float32), dtype=BF16)


def kernel(x, c, ctx, c_ctx, norm_w, w_mod, b_mod, w_in, ret_decay_fwd, ret_decay_bwd, win_sink,
           ax_q_gain, ax_k_gain, w_proj_ret, w_proj_win, w_proj_ax, w_out, final_norm_w):
    xx = (ctx, x)
    cvec = jnp.zeros((24, D_MODEL), F32).at[:BATCH].set(c).at[BATCH].set(c_ctx)
    mod = _mod_call(cvec, w_mod, b_mod).reshape(DEPTH, 24, 1, 3 * D_MODEL)
    tabs = _rope_tables()
    seg = _segment_matrix()
    w_in_bf = w_in.astype(BF16)
    w_wkv = _kv_weights(w_in_bf, _WK, _WV)
    w_akv = _kv_weights(w_in_bf, _AK, _AV)
    w_aq = w_in_bf[:, :, _AQ:_AQ + 512]
    wpr, wpw, wpa, wo = (w.astype(BF16) for w in (w_proj_ret, w_proj_win, w_proj_ax, w_out))
    nw = norm_w.reshape(DEPTH, 1, D_MODEL)
    gq = jnp.tile(ax_q_gain, (1, 2)).reshape(DEPTH, 1, LANES)
    gk = jnp.tile(ax_k_gain, (1, 2)).reshape(DEPTH, 1, LANES)
    a_f = jnp.broadcast_to(ret_decay_fwd[:, :, None, None], (DEPTH, RET_HEADS, 8, LANES))
    a_b = jnp.broadcast_to(ret_decay_bwd[:, :, None, None], (DEPTH, RET_HEADS, 8, LANES))
    fw = final_norm_w.reshape(1, D_MODEL)

    for l in range(DEPTH):
        need_ctx = l < DEPTH - 1
        qf, qb, kf, kb, rv, wq, wkv, aq, akv = _qkv_call(l, xx, mod, nw, w_in_bf, w_wkv, w_aq, w_akv, tabs, gq, gk, seg)
        o_ret = _ret_call(l, a_f, a_b, qf, qb, kf, kb, rv, need_ctx)
        o_win = _win_call(l, win_sink, wq, wkv, need_ctx)
        o_ax = _ax_call(aq, akv, need_ctx)
        xx = _merge_call(l, xx, mod, nw, o_ret, o_win, o_ax, w_in_bf, wpr, wpw, wpa, wo, fw, final=not need_ctx)
    return xx
```

```python
import functools

import numpy as np
import jax
import jax.numpy as jnp
from jax import lax
from jax.experimental import pallas as pl
from jax.experimental.pallas import tpu as pltpu

D_MODEL = 1024
BATCH = 16
SEQ = 2048
DEPTH = 4
CTX_LEN = 256
TOK = CTX_LEN + SEQ
GRID_W = 64
BLOCK = 128
RET_HEADS = 4
RET_DK = 128
HEAD_DIM = 64
ROPE_BASE = 10000.0
EPS = 1e-6
NEG = -1e30

LANES = 128
NBLK = TOK // BLOCK
CTX_BLKS = CTX_LEN // BLOCK
LAT_BLKS = SEQ // BLOCK
LOG2E = 1.4426950408889634
KV_W = 6 * LANES
QB = CTX_LEN
NQB = TOK // QB
AX_AHEAD = 2
WIN_AHEAD = 2
RET_UNROLL = 8
PV_KC = 256

SUB = CTX_LEN
TM = 3 * SUB
TM_FINAL = 4 * SUB
VMEM_LIMIT = 52 * 1024 * 1024

F32 = jnp.float32
BF16 = jnp.bfloat16

_IN_SIZES = (512, 512, 512, 512, 512, 128, 128, 512, 512, 128, 128, 512, 3 * D_MODEL)
_OFF = np.concatenate([[0], np.cumsum(_IN_SIZES)]).astype(np.int64)
(_RQ, _RK, _RV, _RG, _WQ, _WK, _WV, _WG, _AQ, _AK, _AV, _AG, _MG) = [int(o) for o in _OFF[:-1]]

IN_W = int(_OFF[-1])


def _kv_weights(w, k0, v0):
    k, v = w[:, :, k0:k0 + LANES], w[:, :, v0:v0 + LANES]

    def swapped(x):
        return jnp.concatenate([x[..., HEAD_DIM:], x[..., :HEAD_DIM]], axis=-1)

    return jnp.concatenate([k, swapped(k), v, swapped(v)], axis=-1)

_ATT_HEADS = tuple((c, b, (c // 2 + b) % 2, 2 + 2 * (c // 2) + b) for c in range(4) for b in (0, 1))

_NT = (((1,), (1,)), ((), ()))
_TN = (((0,), (0,)), ((), ()))


def _const_spec(shape):
    nd = len(shape)
    return pl.BlockSpec(shape, lambda *_: (0,) * nd, pipeline_mode=pl.Buffered(1))


def _layer_spec(shape, layer):
    nd = len(shape)
    return pl.BlockSpec((None, *shape), lambda *_: (layer,) + (0,) * nd, pipeline_mode=pl.Buffered(1))


def _params(*sem):
    return pltpu.CompilerParams(dimension_semantics=sem, vmem_limit_bytes=VMEM_LIMIT)


def _sigmoid(z):
    return 1.0 / (1.0 + jnp.exp(-z))


def _mod_kernel(c_ref, w_ref, b_ref, o_ref):
    cv = c_ref[...]
    s = cv * _sigmoid(cv)
    o_ref[0] = jnp.dot(s, w_ref[0], preferred_element_type=F32,
                       precision=lax.Precision.HIGHEST) + b_ref[0]


def _mod_call(cvec, w_mod, b_mod):
    rows = cvec.shape[0]
    return pl.pallas_call(
        _mod_kernel,
        grid=(DEPTH, 3),
        in_specs=[pl.BlockSpec((rows, D_MODEL), lambda l, j: (0, 0)),
                  pl.BlockSpec((1, D_MODEL, D_MODEL), lambda l, j: (l, 0, j)),
                  pl.BlockSpec((1, 1, D_MODEL), lambda l, j: (l, 0, j))],
        out_specs=pl.BlockSpec((1, rows, D_MODEL), lambda l, j: (l, 0, j)),
        out_shape=jax.ShapeDtypeStruct((DEPTH, rows, 3 * D_MODEL), F32),
        compiler_params=_params("arbitrary", "arbitrary"),
        name="adaln_mod",
    )(cvec, w_mod, b_mod.reshape(DEPTH, 1, 3 * D_MODEL))


def _sub_mod(mod_ref, modc_ref, j):
    if modc_ref is None or j > 0:
        return mod_ref[0]
    return jnp.where(pl.program_id(1) == 0, modc_ref[0], mod_ref[0])


def _x_operands(xx, tm, latent_only=False):
    nsub = tm // SUB
    if not isinstance(xx, tuple):
        if not latent_only:
            return [xx], [pl.BlockSpec((1, tm, D_MODEL), lambda b, t: (b, t, 0))], False
        specs = [pl.BlockSpec((1, SUB, D_MODEL), lambda b, t, sb=sb: (b, nsub * t + sb + 1, 0)) for sb in range(nsub)]
        return [xx] * nsub, specs, False
    ctx, x = xx
    specs = [pl.BlockSpec((1, SUB, D_MODEL), lambda b, t: (b, 0, 0))]
    for sb in range(nsub):
        specs.append(pl.BlockSpec((1, SUB, D_MODEL), lambda b, t, sb=sb: (b, jnp.maximum(nsub * t + sb - 1, 0), 0)))
    return [ctx] + [x] * nsub, specs, True


def _read_x(x_refs, sb, has_ctx):
    if len(x_refs) == 1:
        return x_refs[0][0, sb * SUB:(sb + 1) * SUB, :]
    if not has_ctx:
        return x_refs[sb][0]
    if sb == 0:
        return jnp.where(pl.program_id(1) == 0, x_refs[0][0], x_refs[1][0])
    return x_refs[1 + sb][0]


def _modulated_norm(x, mod, nw_ref):
    ms = jnp.mean(x * x, axis=-1, keepdims=True)
    y = x * lax.rsqrt(ms + EPS) * nw_ref[...]
    return (y * (1.0 + mod[:, D_MODEL:2 * D_MODEL]) + mod[:, 0:D_MODEL]).astype(BF16)


def _qkv_kernel(*refs, n_x, has_ctx):
    x_refs = refs[:n_x]
    (mod_ref, modc_ref, nw_ref, w_rq, w_rk, w_rv, w_wq, w_wkv, w_aq, w_akv,
     cf_ref, sf_ref, cb_ref, sb_ref, ca_ref, sa_ref, gq_ref, gk_ref, seg_ref,
     qf_ref, qb_ref, kf_ref, kb_ref, rv_ref, wq_ref, wkv_ref, aq_ref, akv_ref) = refs[n_x:]
    k_scale = RET_DK ** -0.5
    q_scale = HEAD_DIM ** -0.5 * LOG2E
    lane = lax.broadcasted_iota(jnp.int32, (SUB, LANES), 1)
    low_half = lane < HEAD_DIM
    even_lane = (lane & 1) == 0

    def lanes(a, j):
        return a[:, j * LANES:(j + 1) * LANES]

    def qk_norm(acc, j0, g_ref):
        v2 = acc[:, j0 * LANES:(j0 + 2) * LANES]
        ss = jnp.dot((v2 * v2).astype(BF16), seg_ref[...], preferred_element_type=F32)
        r = lax.rsqrt(ss * (1.0 / HEAD_DIM) + EPS)
        return [lanes(v2, i) * lanes(r, i) * g_ref[...] for i in range(2)]

    for sb in range(qf_ref.shape[1] // SUB):
        rows = slice(sb * SUB, (sb + 1) * SUB)
        h = _modulated_norm(_read_x(x_refs, sb, has_ctx), _sub_mod(mod_ref, modc_ref, sb), nw_ref)

        def proj(w_ref):
            return jnp.dot(h, w_ref[...], preferred_element_type=F32)

        def rope(v, c_ref, s_ref):
            partner = jnp.where(even_lane, pltpu.roll(v, LANES - 1, 1), pltpu.roll(v, 1, 1))
            return v * c_ref[rows, :] + partner * s_ref[rows, :]

        def put(ref, j, val):
            ref[0, rows, j * LANES:(j + 1) * LANES] = val.astype(BF16)

        def store_kv(kv_ref, acc):
            va, vb = lanes(acc, 2), lanes(acc, 3)
            put(kv_ref, 2, jnp.where(low_half, va, 1.0))
            put(kv_ref, 3, jnp.where(low_half, 1.0, vb))
            put(kv_ref, 4, jnp.where(low_half, vb, 1.0))
            put(kv_ref, 5, jnp.where(low_half, 1.0, va))

        acc = proj(w_aq)
        for j, qn in enumerate(qk_norm(acc, 0, gq_ref) + qk_norm(acc, 2, gq_ref)):
            put(aq_ref, j, rope(qn, ca_ref, sa_ref) * q_scale)
        acc = proj(w_akv)
        for j, kn in enumerate(qk_norm(acc, 0, gk_ref)):
            put(akv_ref, j, rope(kn, ca_ref, sa_ref))
        store_kv(akv_ref, acc)

        acc = proj(w_rq)
        for j in range(4):
            put(qf_ref, j, rope(lanes(acc, j), cf_ref, sf_ref))
            put(qb_ref, j, rope(lanes(acc, j), cb_ref, sb_ref))
        acc = proj(w_rk)
        for j in range(4):
            put(kf_ref, j, rope(lanes(acc, j), cf_ref, sf_ref) * k_scale)
            put(kb_ref, j, rope(lanes(acc, j), cb_ref, sb_ref) * k_scale)
        rv_ref[0, rows, :] = proj(w_rv).astype(BF16)

        acc = proj(w_wq)
        for j in range(4):
            put(wq_ref, j, rope(lanes(acc, j), ca_ref, sa_ref) * q_scale)
        acc = proj(w_wkv)
        for j in range(2):
            put(wkv_ref, j, rope(lanes(acc, j), ca_ref, sa_ref))
        store_kv(wkv_ref, acc)


def _mod_specs(layer):
    return [pl.BlockSpec((None, 1, 1, 3 * D_MODEL), lambda b, t: (layer, b, 0, 0)),
            pl.BlockSpec((None, 1, 1, 3 * D_MODEL), lambda b, t: (layer, BATCH, 0, 0))]


def _qkv_call(layer, xx, mod, nw, w_in_bf, w_wkv, w_aq, w_akv, tabs, gq, gk, seg):
    w_spec = _layer_spec((D_MODEL, 512), layer)

    def col_spec(c0):
        return pl.BlockSpec((None, D_MODEL, 512), lambda *_: (layer, 0, c0 // 512), pipeline_mode=pl.Buffered(1))

    tok_spec = pl.BlockSpec((1, TM, 512), lambda b, t: (b, t, 0))
    tab_spec = pl.BlockSpec((TM, LANES), lambda b, t: (t, 0))
    out_sds = jax.ShapeDtypeStruct((BATCH, TOK, 512), BF16)
    kv_spec = pl.BlockSpec((1, TM, KV_W), lambda b, t: (b, t, 0))
    kv_sds = jax.ShapeDtypeStruct((BATCH, TOK, KV_W), BF16)
    x_ops, x_specs, has_ctx = _x_operands(xx, TM)
    return pl.pallas_call(
        functools.partial(_qkv_kernel, n_x=len(x_ops), has_ctx=has_ctx),
        grid=(BATCH, TOK // TM),
        in_specs=[*x_specs,
                  *_mod_specs(layer),
                  _layer_spec((1, D_MODEL), layer),
                  col_spec(_RQ), col_spec(_RK), col_spec(_RV), col_spec(_WQ), w_spec, w_spec, w_spec,
                  tab_spec, tab_spec, tab_spec, tab_spec, tab_spec, tab_spec,
                  _layer_spec((1, LANES), layer), _layer_spec((1, LANES), layer),
                  _const_spec((2 * LANES, 2 * LANES))],
        out_specs=[tok_spec] * 6 + [kv_spec, tok_spec, kv_spec],
        out_shape=[out_sds] * 6 + [kv_sds, out_sds, kv_sds],
        compiler_params=_params("parallel", "arbitrary"),
        name="qkv_proj",
    )(*x_ops, mod, mod, nw, w_in_bf, w_in_bf, w_in_bf, w_in_bf, w_wkv, w_aq, w_akv, *tabs, gq, gk, seg)


def _log_sigmoid(a):
    return jnp.minimum(a, 0.0) - jnp.log1p(jnp.exp(-jnp.abs(a)))


def _ret_kernel(af_ref, ab_ref, qf_ref, qb_ref, kf_ref, kb_ref, v_ref, o_ref, acc_ref, st_ref, dec_ref,
                *, need_ctx):
    ri = lax.broadcasted_iota(jnp.int32, (BLOCK, BLOCK), 0).astype(F32)
    ci = lax.broadcasted_iota(jnp.int32, (BLOCK, BLOCK), 1).astype(F32)
    d = ri - ci
    for h in range(RET_HEADS):
        lgf = _log_sigmoid(af_ref[h])[0:1, :]
        lgb = _log_sigmoid(ab_ref[h])[0:1, :]
        dec_ref[2 * h, 0] = jnp.where(d >= 0, jnp.exp(jnp.maximum(d, 0.0) * lgf), 0.0)
        dec_ref[2 * h, 1] = jnp.exp((ri + 1.0) * lgf)
        dec_ref[2 * h, 2] = jnp.exp((BLOCK - 1.0 - ri) * lgf)
        dec_ref[2 * h, 3] = jnp.exp(0.0 * ri + float(BLOCK) * lgf)
        dec_ref[2 * h + 1, 0] = jnp.where(d < 0, jnp.exp(jnp.maximum(-d, 0.0) * lgb), 0.0)
        dec_ref[2 * h + 1, 1] = jnp.exp((float(BLOCK) - ri) * lgb)
        dec_ref[2 * h + 1, 2] = jnp.exp(ri * lgb)
        dec_ref[2 * h + 1, 3] = jnp.exp(0.0 * ri + float(BLOCK) * lgb)
    st_ref[...] = jnp.zeros_like(st_ref)
    first_row = 0 if need_ctx else CTX_LEN

    def visit(i, in_ctx, first, want_o):
        cf = i
        cb = (CTX_BLKS - 1 - i) if in_ctx else (NBLK + CTX_BLKS - 1 - i)
        pending = []
        for h in range(RET_HEADS):
            lanes_h = slice(h * LANES, (h + 1) * LANES)
            for dr, (q_ref, k_ref, c) in enumerate(((qf_ref, kf_ref, cf), (qb_ref, kb_ref, cb))):
                r0 = c * BLOCK if in_ctx else pl.multiple_of(c * BLOCK, BLOCK)
                j = 2 * h + dr
                k = k_ref[0, pl.ds(r0, BLOCK), lanes_h]
                v = v_ref[0, pl.ds(r0, BLOCK), lanes_h]
                state = st_ref[j]
                kd = (k.astype(F32) * dec_ref[j, 2]).astype(BF16)
                st_ref[j] = dec_ref[j, 3] * state + lax.dot_general(kd, v, _TN, preferred_element_type=F32)
                if want_o:
                    q = q_ref[0, pl.ds(r0, BLOCK), lanes_h]
                    s = lax.dot_general(q, k, _NT, preferred_element_type=F32)
                    cross = jnp.dot(q, state.astype(BF16), preferred_element_type=F32)
                    pending.append((j, r0, lanes_h, v, s, cross))
        for j, r0, lanes_h, v, s, cross in pending:
            o = jnp.dot((s * dec_ref[j, 0]).astype(BF16), v, preferred_element_type=F32) + cross * dec_ref[j, 1]
            if first:
                acc_ref[pl.ds(r0, BLOCK), lanes_h] = o
            else:
                tot = acc_ref[pl.ds(r0, BLOCK), lanes_h] + o
                tot = tot * lax.rsqrt(jnp.mean(tot * tot, axis=-1, keepdims=True) + EPS)
                o_ref[0, pl.ds(r0 - first_row, BLOCK), lanes_h] = tot.astype(o_ref.dtype)

    for i in range(CTX_BLKS):
        visit(i, True, i < CTX_BLKS // 2, need_ctx)
    half = (NBLK + CTX_BLKS) // 2

    def first_pass(i, carry):
        visit(i, False, True, True)
        return carry

    def second_pass(i, carry):
        visit(i, False, False, True)
        return carry

    lax.fori_loop(CTX_BLKS, half, first_pass, 0, unroll=RET_UNROLL)
    lax.fori_loop(half, NBLK, second_pass, 0, unroll=RET_UNROLL)


def _ret_call(layer, a_f, a_b, qf, qb, kf, kb, rv, need_ctx):
    n_out = TOK if need_ctx else SEQ
    tok_spec = pl.BlockSpec((1, TOK, 512), lambda b: (b, 0, 0))
    a_spec = _layer_spec((RET_HEADS, 8, LANES), layer)
    return pl.pallas_call(
        functools.partial(_ret_kernel, need_ctx=need_ctx),
        grid=(BATCH,),
        in_specs=[a_spec, a_spec, tok_spec, tok_spec, tok_spec, tok_spec, tok_spec],
        out_specs=pl.BlockSpec((1, n_out, 512), lambda b: (b, 0, 0)),
        out_shape=jax.ShapeDtypeStruct((BATCH, n_out, 512), BF16),
        scratch_shapes=[pltpu.VMEM((TOK, 512), F32),
                        pltpu.VMEM((2 * RET_HEADS, RET_DK, LANES), F32),
                        pltpu.VMEM((2 * RET_HEADS, 4, BLOCK, BLOCK), F32)],
        compiler_params=_params("parallel"),
        name="retention",
    )(a_f, a_b, qf, qb, kf, kb, rv)


def _low_half_mask():
    return lax.broadcasted_iota(jnp.int32, (QB, LANES), 1) < HEAD_DIM


def _head_queries(q_ref, c, use_b, low_half):
    qv = q_ref[0, :, c * LANES:(c + 1) * LANES].astype(F32)
    qv = jnp.where(low_half, 0.0, qv) if use_b else jnp.where(low_half, qv, 0.0)
    return qv.astype(BF16)


def _pv(p, load_v, nkeys):
    acc = None
    for r0 in range(0, nkeys, PV_KC):
        d = jnp.dot(p[:, r0:r0 + PV_KC], load_v(r0, r0 + PV_KC), preferred_element_type=F32)
        acc = d if acc is None else acc + d
    return acc


def _store_heads(o_ref, res, low_half, extra=None):
    for c in range(4):
        ra, rb = res[2 * c], res[2 * c + 1]
        num = jnp.where(low_half, ra[:, :LANES], rb[:, LANES:])
        den = jnp.where(low_half, ra[:, LANES:], rb[:, :LANES])
        if extra is not None:
            den = den + jnp.where(low_half, extra[2 * c], extra[2 * c + 1])
        o_ref[0, :, c * LANES:(c + 1) * LANES] = (num * (1.0 / den)).astype(o_ref.dtype)


def _attn_specs(off):
    return dict(
        in_specs=[pl.BlockSpec((1, QB, 512), lambda b, i: (b, i + off, 0)),
                  pl.BlockSpec((1, QB, 512), lambda b, i: (b, jnp.minimum(i + off + 1, NQB - 1), 0)),
                  pl.BlockSpec((1, TOK, KV_W), lambda b, i: (b, 0, 0))],
        out_specs=pl.BlockSpec((1, QB, 512), lambda b, i: (b, i, 0)),
        out_shape=jax.ShapeDtypeStruct((BATCH, (NQB - off) * QB, 512), BF16),
    )


def _ax_kernel(q_ref, qn_ref, kv_ref, o_ref, s_ref, m_ref, *, blk_off):
    low_half = _low_half_mask()

    def scores(qr, head, nkeys):
        c, use_b, kb, _ = head
        qs = _head_queries(qr, c, use_b, low_half)
        k = kv_ref[0, 0:nkeys, kb * LANES:(kb + 1) * LANES]
        return lax.dot_general(qs, k, _NT, preferred_element_type=F32)

    def carry(slot, s):
        s_ref[slot] = s
        m_ref[slot] = jnp.broadcast_to(jnp.max(s, axis=-1, keepdims=True), m_ref.shape[1:])

    def finish(s, head, nkeys, mx=None):
        _, use_b, _, vb = head
        if mx is None:
            mx = jnp.max(s, axis=-1, keepdims=True)
        p = jnp.exp2(s - mx).astype(BF16)
        v0 = (vb - use_b) * LANES
        return _pv(p, lambda r0, r1: kv_ref[0, r0:r1, v0:v0 + 2 * LANES], nkeys)

    nh = len(_ATT_HEADS)

    def prologue(qr):
        for a in range(AX_AHEAD):
            carry(a, scores(qr, _ATT_HEADS[a], TOK))

    def ctx_block():
        res = [finish(scores(q_ref, head, CTX_LEN), head, CTX_LEN) for head in _ATT_HEADS]
        _store_heads(o_ref, res, low_half)
        prologue(qn_ref)

    def latent_block():
        res = []
        ahead = [(s_ref[a], m_ref[a, :, 0:1]) for a in range(AX_AHEAD)]
        for n, head in enumerate(_ATT_HEADS):
            m = n + AX_AHEAD
            ahead.append((scores(q_ref, _ATT_HEADS[m], TOK) if m < nh else scores(qn_ref, _ATT_HEADS[m - nh], TOK), None))
            s, mx = ahead.pop(0)
            res.append(finish(s, head, TOK, mx))
        for a in range(AX_AHEAD):
            carry(a, ahead[a][0])
        _store_heads(o_ref, res, low_half)

    blk = pl.program_id(1)
    if blk_off == 0:
        pl.when(blk == 0)(ctx_block)
        pl.when(blk > 0)(latent_block)
    else:
        pl.when(blk == 0)(lambda: prologue(q_ref))
        latent_block()


def _ax_call(q, kv, need_ctx):
    off = 0 if need_ctx else 1
    specs = _attn_specs(off)
    return pl.pallas_call(
        functools.partial(_ax_kernel, blk_off=off),
        grid=(BATCH, NQB - off),
        scratch_shapes=[pltpu.VMEM((AX_AHEAD, QB, TOK), F32), pltpu.VMEM((AX_AHEAD, QB, LANES), F32)],
        compiler_params=_params("parallel", "arbitrary"),
        name="axial_attn",
        **specs,
    )(q, q, kv)


def _win_kernel(sink_ref, q_ref, qn_ref, kv_ref, o_ref, s_ref, *, layer, blk_off):
    low_half = _low_half_mask()
    nh = len(_ATT_HEADS)
    nk = CTX_LEN + QB + 2 * BLOCK

    def window_loader(n):
        left = jnp.maximum(2 * n - 1, 0)
        right = jnp.minimum(2 * n + 2, LAT_BLKS - 1)
        s_left = pl.multiple_of(CTX_LEN + left * BLOCK, BLOCK)
        s_mid = pl.multiple_of(CTX_LEN + n * QB, QB)
        s_right = pl.multiple_of(CTX_LEN + right * BLOCK, BLOCK)
        cache = {}

        def load(c0, w):
            if (c0, w) not in cache:
                cache[(c0, w)] = jnp.concatenate([kv_ref[0, 0:CTX_LEN, c0:c0 + w],
                                                  kv_ref[0, pl.ds(s_left, BLOCK), c0:c0 + w],
                                                  kv_ref[0, pl.ds(s_mid, QB), c0:c0 + w],
                                                  kv_ref[0, pl.ds(s_right, BLOCK), c0:c0 + w]], axis=0)
            return cache[(c0, w)]

        return load

    def window_mask(n):
        col = lax.broadcasted_iota(jnp.int32, (QB, nk), 1)
        qi = lax.broadcasted_iota(jnp.int32, (QB, nk), 0)
        jj = col - CTX_LEN
        dlt = jj - qi
        lo = jnp.where(n >= 1, 0, BLOCK)
        hi = jnp.where(n <= SEQ // QB - 2, QB + 2 * BLOCK, QB + BLOCK)
        in_win = (dlt >= 0) & (dlt <= 2 * BLOCK) & (jj >= lo) & (jj < hi)
        return (col < CTX_LEN) | in_win

    def scores(qr, head, load):
        c, use_b, kb, _ = head
        qs = _head_queries(qr, c, use_b, low_half)
        return lax.dot_general(qs, load(kb * LANES, LANES), _NT, preferred_element_type=F32)

    def finish(s, head, load, valid):
        c, use_b, _, vb = head
        if valid is not None:
            s = jnp.where(valid, s, NEG)
        sk = sink_ref[layer, 2 * c + use_b] * LOG2E
        mx = jnp.maximum(jnp.max(s, axis=-1, keepdims=True), sk)
        p = jnp.exp2(s - mx).astype(BF16)
        v = load((vb - use_b) * LANES, 2 * LANES)
        return _pv(p, lambda r0, r1: v[r0:r1], v.shape[0]), jnp.exp2(sk - mx)

    def store(done):
        _store_heads(o_ref, [r for r, _ in done], low_half, [t for _, t in done])

    def prologue(qr, n):
        load = window_loader(n)
        for a in range(WIN_AHEAD):
            s_ref[a] = scores(qr, _ATT_HEADS[a], load)

    def ctx_block():
        def load(c0, w):
            return kv_ref[0, 0:CTX_LEN, c0:c0 + w]

        store([finish(scores(q_ref, head, load), head, load, None) for head in _ATT_HEADS])
        prologue(qn_ref, 0)

    def latent_block(n):
        load = window_loader(n)
        load_next = window_loader(jnp.minimum(n + 1, SEQ // QB - 1))
        valid = window_mask(n)
        ahead = [s_ref[a] for a in range(WIN_AHEAD)]
        done = []
        for i, head in enumerate(_ATT_HEADS):
            m = i + WIN_AHEAD
            ahead.append(scores(q_ref, _ATT_HEADS[m], load) if m < nh else scores(qn_ref, _ATT_HEADS[m - nh], load_next))
            done.append(finish(ahead.pop(0), head, load, valid))
        for a in range(WIN_AHEAD):
            s_ref[a] = ahead[a]
        store(done)

    blk = pl.program_id(1)
    if blk_off == 0:
        pl.when(blk == 0)(ctx_block)
        pl.when(blk > 0)(lambda: latent_block(blk - 1))
    else:
        pl.when(blk == 0)(lambda: prologue(q_ref, 0))
        latent_block(blk)


def _win_call(layer, sink, q, kv, need_ctx):
    off = 0 if need_ctx else 1
    specs = _attn_specs(off)
    specs["in_specs"] = [pl.BlockSpec(memory_space=pltpu.SMEM)] + specs["in_specs"]
    return pl.pallas_call(
        functools.partial(_win_kernel, layer=layer, blk_off=off),
        grid=(BATCH, NQB - off),
        scratch_shapes=[pltpu.VMEM((WIN_AHEAD, QB, CTX_LEN + QB + 2 * BLOCK), F32)],
        compiler_params=_params("parallel", "arbitrary"),
        name="window_attn",
        **specs,
    )(sink, q, q, kv)


def _merge_kernel(*refs, n_x, has_ctx, final):
    x_refs = refs[:n_x]
    (mod_ref, modc_ref, nw_ref, or_ref, ow_ref, oa_ref, win_ref, wpr_ref, wpw_ref, wpa_ref,
     wo_ref, fw_ref, out_ref) = refs[n_x:]
    nsub = out_ref.shape[1] // SUB
    branches = ((or_ref, wpr_ref, _RG), (ow_ref, wpw_ref, _WG), (oa_ref, wpa_ref, _AG))

    def from_h(sb):
        mod = _sub_mod(mod_ref, None if final else modc_ref, sb)
        x = _read_x(x_refs, sb, has_ctx)
        h = _modulated_norm(x, mod, nw_ref)
        gates = [jnp.dot(h, win_ref[:, g0:g0 + 512], preferred_element_type=F32) for _, _, g0 in branches]
        logits = [jnp.dot(h, win_ref[:, _MG + j * D_MODEL:_MG + (j + 1) * D_MODEL], preferred_element_type=F32)
                  for j in range(3)]
        return x, mod, gates, logits

    def finish(sb, x, mod, gates, logits):
        rows = slice(sb * SUB, (sb + 1) * SUB)
        mix = None
        for (o_ref, wp_ref, _), g, ml in zip(branches, gates, logits):
            u = (o_ref[0, rows, :].astype(F32) * (g * _sigmoid(g))).astype(BF16)
            t = _sigmoid(ml) * jnp.dot(u, wp_ref[...], preferred_element_type=F32)
            mix = t if mix is None else mix + t
        y = jnp.dot(mix.astype(BF16), wo_ref[...], preferred_element_type=F32)
        xn = x + mod[:, 2 * D_MODEL:3 * D_MODEL] * y
        if final:
            xn = xn * lax.rsqrt(jnp.mean(xn * xn, axis=-1, keepdims=True) + EPS) * fw_ref[...]
        out_ref[0, rows, :] = xn

    cur = from_h(0)
    for sb in range(nsub):
        nxt = from_h(sb + 1) if sb + 1 < nsub else None
        finish(sb, *cur)
        cur = nxt


def _merge_call(layer, xx, mod, nw, o_ret, o_win, o_ax, w_in_bf, wpr, wpw, wpa, wo, fw, final):
    tm = TM_FINAL if final else TM
    n_tok = SEQ if final else TOK
    o_spec = pl.BlockSpec((1, tm, 512), lambda b, t: (b, t, 0))
    x_ops, x_specs, has_ctx = _x_operands(xx, tm, latent_only=final)
    return pl.pallas_call(
        functools.partial(_merge_kernel, n_x=len(x_ops), has_ctx=has_ctx, final=final),
        grid=(BATCH, n_tok // tm),
        in_specs=[*x_specs,
                  *_mod_specs(layer),
                  _layer_spec((1, D_MODEL), layer),
                  o_spec, o_spec, o_spec,
                  _layer_spec((D_MODEL, IN_W), layer),
                  _layer_spec((512, D_MODEL), layer), _layer_spec((512, D_MODEL), layer),
                  _layer_spec((512, D_MODEL), layer),
                  _layer_spec((D_MODEL, D_MODEL), layer),
                  _const_spec((1, D_MODEL))],
        out_specs=pl.BlockSpec((1, tm, D_MODEL), lambda b, t: (b, t, 0)),
        out_shape=jax.ShapeDtypeStruct((BATCH, n_tok, D_MODEL), F32),
        compiler_params=_params("parallel", "arbitrary"),
        name="merge_final" if final else "merge",
    )(*x_ops, mod, mod, nw, o_ret, o_win, o_ax, w_in_bf, wpr, wpw, wpa, wo, fw)


def _rope_tables():
    t = jnp.arange(TOK, dtype=F32)
    theta = ROPE_BASE ** (-jnp.linspace(0.0, 1.0, RET_DK // 2, dtype=F32))
    pos_b = jnp.where(t < CTX_LEN, CTX_LEN - 1.0 - t, 2.0 * CTX_LEN + SEQ - 1.0 - t)

    def pair_tables(ang):
        c, s = jnp.cos(ang), jnp.sin(ang)
        return jnp.repeat(c, 2, axis=-1), jnp.stack([-s, s], axis=-1).reshape(ang.shape[0], -1)

    cf, sf = pair_tables(t[:, None] * theta[None])
    cb, sb = pair_tables(pos_b[:, None] * theta[None])

    s_idx = jnp.arange(SEQ)
    quarter = HEAD_DIM // 4
    freqs = ROPE_BASE ** (-jnp.arange(quarter, dtype=F32) / quarter)
    r = (s_idx // GRID_W).astype(F32)
    col = (s_idx % GRID_W).astype(F32)
    ang = jnp.concatenate([r[:, None] * freqs[None], col[:, None] * freqs[None]], axis=-1)
    c, s = pair_tables(ang)
    ca = jnp.concatenate([jnp.ones((CTX_LEN, LANES), F32), jnp.concatenate([c, c], axis=-1)], axis=0)
    sa = jnp.concatenate([jnp.zeros((CTX_LEN, LANES), F32), jnp.concatenate([s, s], axis=-1)], axis=0)
    return cf, sf, cb, sb, ca, sa


def _segment_matrix():
    head = np.arange(2 * LANES) // HEAD_DIM
    return jnp.asarray((head[:, None] == head[None, :]).astype(np.float32), dtype=BF16)


def kernel(x, c, ctx, c_ctx, norm_w, w_mod, b_mod, w_in, ret_decay_fwd, ret_decay_bwd, win_sink,
           ax_q_gain, ax_k_gain, w_proj_ret, w_proj_win, w_proj_ax, w_out, final_norm_w):
    xx = (ctx, x)
    cvec = jnp.zeros((24, D_MODEL), F32).at[:BATCH].set(c).at[BATCH].set(c_ctx)
    mod = _mod_call(cvec, w_mod, b_mod).reshape(DEPTH, 24, 1, 3 * D_MODEL)
    tabs = _rope_tables()
    seg = _segment_matrix()
    w_in_bf = w_in.astype(BF16)
    w_wkv = _kv_weights(w_in_bf, _WK, _WV)
    w_akv = _kv_weights(w_in_bf, _AK, _AV)
    w_aq = w_in_bf[:, :, _AQ:_AQ + 512]
    wpr, wpw, wpa, wo = (w.astype(BF16) for w in (w_proj_ret, w_proj_win, w_proj_ax, w_out))
    nw = norm_w.reshape(DEPTH, 1, D_MODEL)
    gq = jnp.tile(ax_q_gain, (1, 2)).reshape(DEPTH, 1, LANES)
    gk = jnp.tile(ax_k_gain, (1, 2)).reshape(DEPTH, 1, LANES)
    a_f = jnp.broadcast_to(ret_decay_fwd[:, :, None, None], (DEPTH, RET_HEADS, 8, LANES))
    a_b = jnp.broadcast_to(ret_decay_bwd[:, :, None, None], (DEPTH, RET_HEADS, 8, LANES))
    fw = final_norm_w.reshape(1, D_MODEL)

    for l in range(DEPTH):
        need_ctx = l < DEPTH - 1
        qf, qb, kf, kb, rv, wq, wkv, aq, akv = _qkv_call(l, xx, mod, nw, w_in_bf, w_wkv, w_aq, w_akv, tabs, gq, gk, seg)
        o_ret = _ret_call(l, a_f, a_b, qf, qb, kf, kb, rv, need_ctx)
        o_win = _win_call(l, win_sink, wq, wkv, need_ctx)
        o_ax = _ax_call(aq, akv, need_ctx)
        xx = _merge_call(l, xx, mod, nw, o_ret, o_win, o_ax, w_in_bf, wpr, wpw, wpa, wo, fw, final=not need_ctx)
    return xx
```

```python
import functools

import numpy as np
import jax
import jax.numpy as jnp
from jax import lax
from jax.experimental import pallas as pl
from jax.experimental.pallas import tpu as pltpu

D_MODEL = 1024
BATCH = 16
SEQ = 2048
DEPTH = 4
CTX_LEN = 256
TOK = CTX_LEN + SEQ
GRID_W = 64
BLOCK = 128
RET_HEADS = 4
RET_DK = 128
HEAD_DIM = 64
ROPE_BASE = 10000.0
EPS = 1e-6
NEG = -1e30

LANES = 128
NBLK = TOK // BLOCK
CTX_BLKS = CTX_LEN // BLOCK
LAT_BLKS = SEQ // BLOCK
LOG2E = 1.4426950408889634
KV_W = 6 * LANES
QB = CTX_LEN
NQB = TOK // QB
AX_AHEAD = 2
WIN_AHEAD = 2
PV_KC = 256

SUB = CTX_LEN
TM = 3 * SUB
TM_FINAL = 4 * SUB
VMEM_LIMIT = 52 * 1024 * 1024

F32 = jnp.float32
BF16 = jnp.bfloat16

_IN_SIZES = (512, 512, 512, 512, 512, 128, 128, 512, 512, 128, 128, 512, 3 * D_MODEL)
_OFF = np.concatenate([[0], np.cumsum(_IN_SIZES)]).astype(np.int64)
(_RQ, _RK, _RV, _RG, _WQ, _WK, _WV, _WG, _AQ, _AK, _AV, _AG, _MG) = [int(o) for o in _OFF[:-1]]

IN_W = int(_OFF[-1])


def _kv_weights(w, k0, v0):
    k, v = w[:, :, k0:k0 + LANES], w[:, :, v0:v0 + LANES]

    def swapped(x):
        return jnp.concatenate([x[..., HEAD_DIM:], x[..., :HEAD_DIM]], axis=-1)

    return jnp.concatenate([k, swapped(k), v, swapped(v)], axis=-1)

_ATT_HEADS = tuple((c, b, (c // 2 + b) % 2, 2 + 2 * (c // 2) + b) for c in range(4) for b in (0, 1))

_NT = (((1,), (1,)), ((), ()))
_TN = (((0,), (0,)), ((), ()))


def _const_spec(shape):
    nd = len(shape)
    return pl.BlockSpec(shape, lambda *_: (0,) * nd, pipeline_mode=pl.Buffered(1))


def _layer_spec(shape, layer):
    nd = len(shape)
    return pl.BlockSpec((None, *shape), lambda *_: (layer,) + (0,) * nd, pipeline_mode=pl.Buffered(1))


def _params(*sem):
    return pltpu.CompilerParams(dimension_semantics=sem, vmem_limit_bytes=VMEM_LIMIT)


def _sigmoid(z):
    return 1.0 / (1.0 + jnp.exp(-z))


def _mod_kernel(c_ref, w_ref, b_ref, o_ref):
    cv = c_ref[...]
    s = cv * _sigmoid(cv)
    o_ref[0] = jnp.dot(s, w_ref[0], preferred_element_type=F32,
                       precision=lax.Precision.HIGHEST) + b_ref[0]


def _mod_call(cvec, w_mod, b_mod):
    rows = cvec.shape[0]
    return pl.pallas_call(
        _mod_kernel,
        grid=(DEPTH, 3),
        in_specs=[pl.BlockSpec((rows, D_MODEL), lambda l, j: (0, 0)),
                  pl.BlockSpec((1, D_MODEL, D_MODEL), lambda l, j: (l, 0, j)),
                  pl.BlockSpec((1, 1, D_MODEL), lambda l, j: (l, 0, j))],
        out_specs=pl.BlockSpec((1, rows, D_MODEL), lambda l, j: (l, 0, j)),
        out_shape=jax.ShapeDtypeStruct((DEPTH, rows, 3 * D_MODEL), F32),
        compiler_params=_params("arbitrary", "arbitrary"),
        name="adaln_mod",
    )(cvec, w_mod, b_mod.reshape(DEPTH, 1, 3 * D_MODEL))


def _sub_mod(mod_ref, modc_ref, j):
    if modc_ref is None or j > 0:
        return mod_ref[0]
    return jnp.where(pl.program_id(1) == 0, modc_ref[0], mod_ref[0])


def _x_operands(xx, tm, latent_only=False):
    nsub = tm // SUB
    if not isinstance(xx, tuple):
        if not latent_only:
            return [xx], [pl.BlockSpec((1, tm, D_MODEL), lambda b, t: (b, t, 0))], False
        specs = [pl.BlockSpec((1, SUB, D_MODEL), lambda b, t, sb=sb: (b, nsub * t + sb + 1, 0)) for sb in range(nsub)]
        return [xx] * nsub, specs, False
    ctx, x = xx
    specs = [pl.BlockSpec((1, SUB, D_MODEL), lambda b, t: (b, 0, 0))]
    for sb in range(nsub):
        specs.append(pl.BlockSpec((1, SUB, D_MODEL), lambda b, t, sb=sb: (b, jnp.maximum(nsub * t + sb - 1, 0), 0)))
    return [ctx] + [x] * nsub, specs, True


def _read_x(x_refs, sb, has_ctx):
    if len(x_refs) == 1:
        return x_refs[0][0, sb * SUB:(sb + 1) * SUB, :]
    if not has_ctx:
        return x_refs[sb][0]
    if sb == 0:
        return jnp.where(pl.program_id(1) == 0, x_refs[0][0], x_refs[1][0])
    return x_refs[1 + sb][0]


def _modulated_norm(x, mod, nw_ref):
    ms = jnp.mean(x * x, axis=-1, keepdims=True)
    y = x * lax.rsqrt(ms + EPS) * nw_ref[...]
    return (y * (1.0 + mod[:, D_MODEL:2 * D_MODEL]) + mod[:, 0:D_MODEL]).astype(BF16)


def _qkv_kernel(*refs, n_x, has_ctx):
    x_refs = refs[:n_x]
    (mod_ref, modc_ref, nw_ref, w_rq, w_rk, w_rv, w_wq, w_wkv, w_aq, w_akv,
     cf_ref, sf_ref, cb_ref, sb_ref, ca_ref, sa_ref, gq_ref, gk_ref, seg_ref,
     qf_ref, qb_ref, kf_ref, kb_ref, rv_ref, wq_ref, wkv_ref, aq_ref, akv_ref) = refs[n_x:]
    k_scale = RET_DK ** -0.5
    q_scale = HEAD_DIM ** -0.5 * LOG2E
    lane = lax.broadcasted_iota(jnp.int32, (SUB, LANES), 1)
    low_half = lane < HEAD_DIM
    even_lane = (lane & 1) == 0

    def lanes(a, j):
        return a[:, j * LANES:(j + 1) * LANES]

    def qk_norm(acc, j0, g_ref):
        v2 = acc[:, j0 * LANES:(j0 + 2) * LANES]
        ss = jnp.dot((v2 * v2).astype(BF16), seg_ref[...], preferred_element_type=F32)
        r = lax.rsqrt(ss * (1.0 / HEAD_DIM) + EPS)
        return [lanes(v2, i) * lanes(r, i) * g_ref[...] for i in range(2)]

    for sb in range(qf_ref.shape[1] // SUB):
        rows = slice(sb * SUB, (sb + 1) * SUB)
        h = _modulated_norm(_read_x(x_refs, sb, has_ctx), _sub_mod(mod_ref, modc_ref, sb), nw_ref)

        def proj(w_ref):
            return jnp.dot(h, w_ref[...], preferred_element_type=F32)

        def rope(v, c_ref, s_ref):
            partner = jnp.where(even_lane, pltpu.roll(v, LANES - 1, 1), pltpu.roll(v, 1, 1))
            return v * c_ref[rows, :] + partner * s_ref[rows, :]

        def put(ref, j, val):
            ref[0, rows, j * LANES:(j + 1) * LANES] = val.astype(BF16)

        def store_kv(kv_ref, acc):
            va, vb = lanes(acc, 2), lanes(acc, 3)
            put(kv_ref, 2, jnp.where(low_half, va, 1.0))
            put(kv_ref, 3, jnp.where(low_half, 1.0, vb))
            put(kv_ref, 4, jnp.where(low_half, vb, 1.0))
            put(kv_ref, 5, jnp.where(low_half, 1.0, va))

        acc = proj(w_aq)
        for j, qn in enumerate(qk_norm(acc, 0, gq_ref) + qk_norm(acc, 2, gq_ref)):
            put(aq_ref, j, rope(qn, ca_ref, sa_ref) * q_scale)
        acc = proj(w_akv)
        for j, kn in enumerate(qk_norm(acc, 0, gk_ref)):
            put(akv_ref, j, rope(kn, ca_ref, sa_ref))
        store_kv(akv_ref, acc)

        acc = proj(w_rq)
        for j in range(4):
            put(qf_ref, j, rope(lanes(acc, j), cf_ref, sf_ref))
            put(qb_ref, j, rope(lanes(acc, j), cb_ref, sb_ref))
        acc = proj(w_rk)
        for j in range(4):
            put(kf_ref, j, rope(lanes(acc, j), cf_ref, sf_ref) * k_scale)
            put(kb_ref, j, rope(lanes(acc, j), cb_ref, sb_ref) * k_scale)
        rv_ref[0, rows, :] = proj(w_rv).astype(BF16)

        acc = proj(w_wq)
        for j in range(4):
            put(wq_ref, j, rope(lanes(acc, j), ca_ref, sa_ref) * q_scale)
        acc = proj(w_wkv)
        for j in range(2):
            put(wkv_ref, j, rope(lanes(acc, j), ca_ref, sa_ref))
        store_kv(wkv_ref, acc)


def _mod_specs(layer):
    return [pl.BlockSpec((None, 1, 1, 3 * D_MODEL), lambda b, t: (layer, b, 0, 0)),
            pl.BlockSpec((None, 1, 1, 3 * D_MODEL), lambda b, t: (layer, BATCH, 0, 0))]


def _qkv_call(layer, xx, mod, nw, w_in_bf, w_wkv, w_aq, w_akv, tabs, gq, gk, seg):
    w_spec = _layer_spec((D_MODEL, 512), layer)

    def col_spec(c0):
        return pl.BlockSpec((None, D_MODEL, 512), lambda *_: (layer, 0, c0 // 512), pipeline_mode=pl.Buffered(1))

    tok_spec = pl.BlockSpec((1, TM, 512), lambda b, t: (b, t, 0))
    tab_spec = pl.BlockSpec((TM, LANES), lambda b, t: (t, 0))
    out_sds = jax.ShapeDtypeStruct((BATCH, TOK, 512), BF16)
    kv_spec = pl.BlockSpec((1, TM, KV_W), lambda b, t: (b, t, 0))
    kv_sds = jax.ShapeDtypeStruct((BATCH, TOK, KV_W), BF16)
    x_ops, x_specs, has_ctx = _x_operands(xx, TM)
    return pl.pallas_call(
        functools.partial(_qkv_kernel, n_x=len(x_ops), has_ctx=has_ctx),
        grid=(BATCH, TOK // TM),
        in_specs=[*x_specs,
                  *_mod_specs(layer),
                  _layer_spec((1, D_MODEL), layer),
                  col_spec(_RQ), col_spec(_RK), col_spec(_RV), col_spec(_WQ), w_spec, w_spec, w_spec,
                  tab_spec, tab_spec, tab_spec, tab_spec, tab_spec, tab_spec,
                  _layer_spec((1, LANES), layer), _layer_spec((1, LANES), layer),
                  _const_spec((2 * LANES, 2 * LANES))],
        out_specs=[tok_spec] * 6 + [kv_spec, tok_spec, kv_spec],
        out_shape=[out_sds] * 6 + [kv_sds, out_sds, kv_sds],
        compiler_params=_params("parallel", "arbitrary"),
        name="qkv_proj",
    )(*x_ops, mod, mod, nw, w_in_bf, w_in_bf, w_in_bf, w_in_bf, w_wkv, w_aq, w_akv, *tabs, gq, gk, seg)


def _log_sigmoid(a):
    return jnp.minimum(a, 0.0) - jnp.log1p(jnp.exp(-jnp.abs(a)))


def _ret_kernel(af_ref, ab_ref, qf_ref, qb_ref, kf_ref, kb_ref, v_ref, o_ref, acc_ref, st_ref, dec_ref,
                *, need_ctx):
    ri = lax.broadcasted_iota(jnp.int32, (BLOCK, BLOCK), 0).astype(F32)
    ci = lax.broadcasted_iota(jnp.int32, (BLOCK, BLOCK), 1).astype(F32)
    d = ri - ci
    for h in range(RET_HEADS):
        lgf = _log_sigmoid(af_ref[h])[0:1, :]
        lgb = _log_sigmoid(ab_ref[h])[0:1, :]
        dec_ref[2 * h, 0] = jnp.where(d >= 0, jnp.exp(jnp.maximum(d, 0.0) * lgf), 0.0)
        dec_ref[2 * h, 1] = jnp.exp((ri + 1.0) * lgf)
        dec_ref[2 * h, 2] = jnp.exp((BLOCK - 1.0 - ri) * lgf)
        dec_ref[2 * h, 3] = jnp.exp(0.0 * ri + float(BLOCK) * lgf)
        dec_ref[2 * h + 1, 0] = jnp.where(d < 0, jnp.exp(jnp.maximum(-d, 0.0) * lgb), 0.0)
        dec_ref[2 * h + 1, 1] = jnp.exp((float(BLOCK) - ri) * lgb)
        dec_ref[2 * h + 1, 2] = jnp.exp(ri * lgb)
        dec_ref[2 * h + 1, 3] = jnp.exp(0.0 * ri + float(BLOCK) * lgb)
    st_ref[...] = jnp.zeros_like(st_ref)
    first_row = 0 if need_ctx else CTX_LEN

    def visit(i, in_ctx, first, want_o):
        cf = i
        cb = (CTX_BLKS - 1 - i) if in_ctx else (NBLK + CTX_BLKS - 1 - i)
        pending = []
        for h in range(RET_HEADS):
            lanes_h = slice(h * LANES, (h + 1) * LANES)
            for dr, (q_ref, k_ref, c) in enumerate(((qf_ref, kf_ref, cf), (qb_ref, kb_ref, cb))):
                r0 = c * BLOCK
                j = 2 * h + dr
                k = k_ref[0, pl.ds(r0, BLOCK), lanes_h]
                v = v_ref[0, pl.ds(r0, BLOCK), lanes_h]
                state = st_ref[j]
                kd = (k.astype(F32) * dec_ref[j, 2]).astype(BF16)
                st_ref[j] = dec_ref[j, 3] * state + lax.dot_general(kd, v, _TN, preferred_element_type=F32)
                if want_o:
                    q = q_ref[0, pl.ds(r0, BLOCK), lanes_h]
                    s = lax.dot_general(q, k, _NT, preferred_element_type=F32)
                    cross = jnp.dot(q, state.astype(BF16), preferred_element_type=F32)
                    pending.append((j, r0, lanes_h, v, s, cross))
        for j, r0, lanes_h, v, s, cross in pending:
            o = jnp.dot((s * dec_ref[j, 0]).astype(BF16), v, preferred_element_type=F32) + cross * dec_ref[j, 1]
            if first:
                acc_ref[pl.ds(r0, BLOCK), lanes_h] = o
            else:
                tot = acc_ref[pl.ds(r0, BLOCK), lanes_h] + o
                tot = tot * lax.rsqrt(jnp.mean(tot * tot, axis=-1, keepdims=True) + EPS)
                o_ref[0, pl.ds(r0 - first_row, BLOCK), lanes_h] = tot.astype(o_ref.dtype)

    half = (NBLK + CTX_BLKS) // 2
    for i in range(NBLK):
        if i < CTX_BLKS:
            visit(i, True, i < CTX_BLKS // 2, need_ctx)
        else:
            visit(i, False, i < half, True)


def _ret_call(layer, a_f, a_b, qf, qb, kf, kb, rv, need_ctx):
    n_out = TOK if need_ctx else SEQ
    tok_spec = pl.BlockSpec((1, TOK, 512), lambda b: (b, 0, 0))
    a_spec = _layer_spec((RET_HEADS, 8, LANES), layer)
    return pl.pallas_call(
        functools.partial(_ret_kernel, need_ctx=need_ctx),
        grid=(BATCH,),
        in_specs=[a_spec, a_spec, tok_spec, tok_spec, tok_spec, tok_spec, tok_spec],
        out_specs=pl.BlockSpec((1, n_out, 512), lambda b: (b, 0, 0)),
        out_shape=jax.ShapeDtypeStruct((BATCH, n_out, 512), BF16),
        scratch_shapes=[pltpu.VMEM((TOK, 512), F32),
                        pltpu.VMEM((2 * RET_HEADS, RET_DK, LANES), F32),
                        pltpu.VMEM((2 * RET_HEADS, 4, BLOCK, BLOCK), F32)],
        compiler_params=_params("parallel"),
        name="retention",
    )(a_f, a_b, qf, qb, kf, kb, rv)


def _low_half_mask():
    return lax.broadcasted_iota(jnp.int32, (QB, LANES), 1) < HEAD_DIM


def _head_queries(q_ref, c, use_b, low_half):
    qv = q_ref[0, :, c * LANES:(c + 1) * LANES].astype(F32)
    qv = jnp.where(low_half, 0.0, qv) if use_b else jnp.where(low_half, qv, 0.0)
    return qv.astype(BF16)


def _pv(p, load_v, nkeys):
    acc = None
    for r0 in range(0, nkeys, PV_KC):
        d = jnp.dot(p[:, r0:r0 + PV_KC], load_v(r0, r0 + PV_KC), preferred_element_type=F32)
        acc = d if acc is None else acc + d
    return acc


def _store_heads(o_ref, res, low_half, extra=None):
    for c in range(4):
        ra, rb = res[2 * c], res[2 * c + 1]
        num = jnp.where(low_half, ra[:, :LANES], rb[:, LANES:])
        den = jnp.where(low_half, ra[:, LANES:], rb[:, :LANES])
        if extra is not None:
            den = den + jnp.where(low_half, extra[2 * c], extra[2 * c + 1])
        o_ref[0, :, c * LANES:(c + 1) * LANES] = (num * (1.0 / den)).astype(o_ref.dtype)


def _attn_specs(off):
    return dict(
        in_specs=[pl.BlockSpec((1, QB, 512), lambda b, i: (b, i + off, 0)),
                  pl.BlockSpec((1, QB, 512), lambda b, i: (b, jnp.minimum(i + off + 1, NQB - 1), 0)),
                  pl.BlockSpec((1, TOK, KV_W), lambda b, i: (b, 0, 0))],
        out_specs=pl.BlockSpec((1, QB, 512), lambda b, i: (b, i, 0)),
        out_shape=jax.ShapeDtypeStruct((BATCH, (NQB - off) * QB, 512), BF16),
    )


def _ax_kernel(q_ref, qn_ref, kv_ref, o_ref, s_ref, m_ref, *, blk_off):
    low_half = _low_half_mask()

    def scores(qr, head, nkeys):
        c, use_b, kb, _ = head
        qs = _head_queries(qr, c, use_b, low_half)
        k = kv_ref[0, 0:nkeys, kb * LANES:(kb + 1) * LANES]
        return lax.dot_general(qs, k, _NT, preferred_element_type=F32)

    def carry(slot, s):
        s_ref[slot] = s
        m_ref[slot] = jnp.broadcast_to(jnp.max(s, axis=-1, keepdims=True), m_ref.shape[1:])

    def finish(s, head, nkeys, mx=None):
        _, use_b, _, vb = head
        if mx is None:
            mx = jnp.max(s, axis=-1, keepdims=True)
        p = jnp.exp2(s - mx).astype(BF16)
        v0 = (vb - use_b) * LANES
        return _pv(p, lambda r0, r1: kv_ref[0, r0:r1, v0:v0 + 2 * LANES], nkeys)

    nh = len(_ATT_HEADS)

    def prologue(qr):
        for a in range(AX_AHEAD):
            carry(a, scores(qr, _ATT_HEADS[a], TOK))

    def ctx_block():
        res = [finish(scores(q_ref, head, CTX_LEN), head, CTX_LEN) for head in _ATT_HEADS]
        _store_heads(o_ref, res, low_half)
        prologue(qn_ref)

    def latent_block():
        res = []
        ahead = [(s_ref[a], m_ref[a, :, 0:1]) for a in range(AX_AHEAD)]
        for n, head in enumerate(_ATT_HEADS):
            m = n + AX_AHEAD
            ahead.append((scores(q_ref, _ATT_HEADS[m], TOK) if m < nh else scores(qn_ref, _ATT_HEADS[m - nh], TOK), None))
            s, mx = ahead.pop(0)
            res.append(finish(s, head, TOK, mx))
        for a in range(AX_AHEAD):
            carry(a, ahead[a][0])
        _store_heads(o_ref, res, low_half)

    blk = pl.program_id(1)
    if blk_off == 0:
        pl.when(blk == 0)(ctx_block)
        pl.when(blk > 0)(latent_block)
    else:
        pl.when(blk == 0)(lambda: prologue(q_ref))
        latent_block()


def _ax_call(q, kv, need_ctx):
    off = 0 if need_ctx else 1
    specs = _attn_specs(off)
    return pl.pallas_call(
        functools.partial(_ax_kernel, blk_off=off),
        grid=(BATCH, NQB - off),
        scratch_shapes=[pltpu.VMEM((AX_AHEAD, QB, TOK), F32), pltpu.VMEM((AX_AHEAD, QB, LANES), F32)],
        compiler_params=_params("parallel", "arbitrary"),
        name="axial_attn",
        **specs,
    )(q, q, kv)


def _win_kernel(sink_ref, q_ref, qn_ref, kv_ref, o_ref, s_ref, *, layer, blk_off):
    low_half = _low_half_mask()
    nh = len(_ATT_HEADS)
    nk = CTX_LEN + QB + 2 * BLOCK

    def window_loader(n):
        left = jnp.maximum(2 * n - 1, 0)
        right = jnp.minimum(2 * n + 2, LAT_BLKS - 1)
        s_left = pl.multiple_of(CTX_LEN + left * BLOCK, BLOCK)
        s_mid = pl.multiple_of(CTX_LEN + n * QB, QB)
        s_right = pl.multiple_of(CTX_LEN + right * BLOCK, BLOCK)
        cache = {}

        def load(c0, w):
            if (c0, w) not in cache:
                cache[(c0, w)] = jnp.concatenate([kv_ref[0, 0:CTX_LEN, c0:c0 + w],
                                                  kv_ref[0, pl.ds(s_left, BLOCK), c0:c0 + w],
                                                  kv_ref[0, pl.ds(s_mid, QB), c0:c0 + w],
                                                  kv_ref[0, pl.ds(s_right, BLOCK), c0:c0 + w]], axis=0)
            return cache[(c0, w)]

        return load

    def window_mask(n):
        col = lax.broadcasted_iota(jnp.int32, (QB, nk), 1)
        qi = lax.broadcasted_iota(jnp.int32, (QB, nk), 0)
        jj = col - CTX_LEN
        dlt = jj - qi
        lo = jnp.where(n >= 1, 0, BLOCK)
        hi = jnp.where(n <= SEQ // QB - 2, QB + 2 * BLOCK, QB + BLOCK)
        in_win = (dlt >= 0) & (dlt <= 2 * BLOCK) & (jj >= lo) & (jj < hi)
        return (col < CTX_LEN) | in_win

    def scores(qr, head, load):
        c, use_b, kb, _ = head
        qs = _head_queries(qr, c, use_b, low_half)
        return lax.dot_general(qs, load(kb * LANES, LANES), _NT, preferred_element_type=F32)

    def finish(s, head, load, valid):
        c, use_b, _, vb = head
        if valid is not None:
            s = jnp.where(valid, s, NEG)
        sk = sink_ref[layer, 2 * c + use_b] * LOG2E
        mx = jnp.maximum(jnp.max(s, axis=-1, keepdims=True), sk)
        p = jnp.exp2(s - mx).astype(BF16)
        v = load((vb - use_b) * LANES, 2 * LANES)
        return _pv(p, lambda r0, r1: v[r0:r1], v.shape[0]), jnp.exp2(sk - mx)

    def store(done):
        _store_heads(o_ref, [r for r, _ in done], low_half, [t for _, t in done])

    def prologue(qr, n):
        load = window_loader(n)
        for a in range(WIN_AHEAD):
            s_ref[a] = scores(qr, _ATT_HEADS[a], load)

    def ctx_block():
        def load(c0, w):
            return kv_ref[0, 0:CTX_LEN, c0:c0 + w]

        store([finish(scores(q_ref, head, load), head, load, None) for head in _ATT_HEADS])
        prologue(qn_ref, 0)

    def latent_block(n):
        load = window_loader(n)
        load_next = window_loader(jnp.minimum(n + 1, SEQ // QB - 1))
        valid = window_mask(n)
        ahead = [s_ref[a] for a in range(WIN_AHEAD)]
        done = []
        for i, head in enumerate(_ATT_HEADS):
            m = i + WIN_AHEAD
            ahead.append(scores(q_ref, _ATT_HEADS[m], load) if m < nh else scores(qn_ref, _ATT_HEADS[m - nh], load_next))
            done.append(finish(ahead.pop(0), head, load, valid))
        for a in range(WIN_AHEAD):
            s_ref[a] = ahead[a]
        store(done)

    blk = pl.program_id(1)
    if blk_off == 0:
        pl.when(blk == 0)(ctx_block)
        pl.when(blk > 0)(lambda: latent_block(blk - 1))
    else:
        pl.when(blk == 0)(lambda: prologue(q_ref, 0))
        latent_block(blk)


def _win_call(layer, sink, q, kv, need_ctx):
    off = 0 if need_ctx else 1
    specs = _attn_specs(off)
    specs["in_specs"] = [pl.BlockSpec(memory_space=pltpu.SMEM)] + specs["in_specs"]
    return pl.pallas_call(
        functools.partial(_win_kernel, layer=layer, blk_off=off),
        grid=(BATCH, NQB - off),
        scratch_shapes=[pltpu.VMEM((WIN_AHEAD, QB, CTX_LEN + QB + 2 * BLOCK), F32)],
        compiler_params=_params("parallel", "arbitrary"),
        name="window_attn",
        **specs,
    )(sink, q, q, kv)


def _merge_kernel(*refs, n_x, has_ctx, final):
    x_refs = refs[:n_x]
    (mod_ref, modc_ref, nw_ref, or_ref, ow_ref, oa_ref, win_ref, wpr_ref, wpw_ref, wpa_ref,
     wo_ref, fw_ref, out_ref) = refs[n_x:]
    nsub = out_ref.shape[1] // SUB
    branches = ((or_ref, wpr_ref, _RG), (ow_ref, wpw_ref, _WG), (oa_ref, wpa_ref, _AG))

    def from_h(sb):
        mod = _sub_mod(mod_ref, None if final else modc_ref, sb)
        x = _read_x(x_refs, sb, has_ctx)
        h = _modulated_norm(x, mod, nw_ref)
        gates = [jnp.dot(h, win_ref[:, g0:g0 + 512], preferred_element_type=F32) for _, _, g0 in branches]
        logits = [jnp.dot(h, win_ref[:, _MG + j * D_MODEL:_MG + (j + 1) * D_MODEL], preferred_element_type=F32)
                  for j in range(3)]
        return x, mod, gates, logits

    def finish(sb, x, mod, gates, logits):
        rows = slice(sb * SUB, (sb + 1) * SUB)
        mix = None
        for (o_ref, wp_ref, _), g, ml in zip(branches, gates, logits):
            u = (o_ref[0, rows, :].astype(F32) * (g * _sigmoid(g))).astype(BF16)
            t = _sigmoid(ml) * jnp.dot(u, wp_ref[...], preferred_element_type=F32)
            mix = t if mix is None else mix + t
        y = jnp.dot(mix.astype(BF16), wo_ref[...], preferred_element_type=F32)
        xn = x + mod[:, 2 * D_MODEL:3 * D_MODEL] * y
        if final:
            xn = xn * lax.rsqrt(jnp.mean(xn * xn, axis=-1, keepdims=True) + EPS) * fw_ref[...]
        out_ref[0, rows, :] = xn

    cur = from_h(0)
    for sb in range(nsub):
        nxt = from_h(sb + 1) if sb + 1 < nsub else None
        finish(sb, *cur)
        cur = nxt


def _merge_call(layer, xx, mod, nw, o_ret, o_win, o_ax, w_in_bf, wpr, wpw, wpa, wo, fw, final):
    tm = TM_FINAL if final else TM
    n_tok = SEQ if final else TOK
    o_spec = pl.BlockSpec((1, tm, 512), lambda b, t: (b, t, 0))
    x_ops, x_specs, has_ctx = _x_operands(xx, tm, latent_only=final)
    return pl.pallas_call(
        functools.partial(_merge_kernel, n_x=len(x_ops), has_ctx=has_ctx, final=final),
        grid=(BATCH, n_tok // tm),
        in_specs=[*x_specs,
                  *_mod_specs(layer),
                  _layer_spec((1, D_MODEL), layer),
                  o_spec, o_spec, o_spec,
                  _layer_spec((D_MODEL, IN_W), layer),
                  _layer_spec((512, D_MODEL), layer), _layer_spec((512, D_MODEL), layer),
                  _layer_spec((512, D_MODEL), layer),
                  _layer_spec((D_MODEL, D_MODEL), layer),
                  _const_spec((1, D_MODEL))],
        out_specs=pl.BlockSpec((1, tm, D_MODEL), lambda b, t: (b, t, 0)),
        out_shape=jax.ShapeDtypeStruct((BATCH, n_tok, D_MODEL), F32),
        compiler_params=_params("parallel", "arbitrary"),
        name="merge_final" if final else "merge",
    )(*x_ops, mod, mod, nw, o_ret, o_win, o_ax, w_in_bf, wpr, wpw, wpa, wo, fw)


def _rope_tables():
    t = jnp.arange(TOK, dtype=F32)
    theta = ROPE_BASE ** (-jnp.linspace(0.0, 1.0, RET_DK // 2, dtype=F32))
    pos_b = jnp.where(t < CTX_LEN, CTX_LEN - 1.0 - t, 2.0 * CTX_LEN + SEQ - 1.0 - t)

    def pair_tables(ang):
        c, s = jnp.cos(ang), jnp.sin(ang)
        return jnp.repeat(c, 2, axis=-1), jnp.stack([-s, s], axis=-1).reshape(ang.shape[0], -1)

    cf, sf = pair_tables(t[:, None] * theta[None])
    cb, sb = pair_tables(pos_b[:, None] * theta[None])

    s_idx = jnp.arange(SEQ)
    quarter = HEAD_DIM // 4
    freqs = ROPE_BASE ** (-jnp.arange(quarter, dtype=F32) / quarter)
    r = (s_idx // GRID_W).astype(F32)
    col = (s_idx % GRID_W).astype(F32)
    ang = jnp.concatenate([r[:, None] * freqs[None], col[:, None] * freqs[None]], axis=-1)
    c, s = pair_tables(ang)
    ca = jnp.concatenate([jnp.ones((CTX_LEN, LANES), F32), jnp.concatenate([c, c], axis=-1)], axis=0)
    sa = jnp.concatenate([jnp.zeros((CTX_LEN, LANES), F32), jnp.concatenate([s, s], axis=-1)], axis=0)
    return cf, sf, cb, sb, ca, sa


def _segment_matrix():
    head = np.arange(2 * LANES) // HEAD_DIM
    return jnp.asarray((head[:, None] == head[None, :]).astype(np.float32), dtype=BF16)


def kernel(x, c, ctx, c_ctx, norm_w, w_mod, b_mod, w_in, ret_decay_fwd, ret_decay_bwd, win_sink,
           ax_q_gain, ax_k_gain, w_proj_ret, w_proj_win, w_proj_ax, w_out, final_norm_w):
    xx = (ctx, x)
    cvec = jnp.zeros((24, D_MODEL), F32).at[:BATCH].set(c).at[BATCH].set(c_ctx)
    mod = _mod_call(cvec, w_mod, b_mod).reshape(DEPTH, 24, 1, 3 * D_MODEL)
    tabs = _rope_tables()
    seg = _segment_matrix()
    w_in_bf = w_in.astype(BF16)
    w_wkv = _kv_weights(w_in_bf, _WK, _WV)
    w_akv = _kv_weights(w_in_bf, _AK, _AV)
    w_aq = w_in_bf[:, :, _AQ:_AQ + 512]
    wpr, wpw, wpa, wo = (w.astype(BF16) for w in (w_proj_ret, w_proj_win, w_proj_ax, w_out))
    nw = norm_w.reshape(DEPTH, 1, D_MODEL)
    gq = jnp.tile(ax_q_gain, (1, 2)).reshape(DEPTH, 1, LANES)
    gk = jnp.tile(ax_k_gain, (1, 2)).reshape(DEPTH, 1, LANES)
    a_f = jnp.broadcast_to(ret_decay_fwd[:, :, None, None], (DEPTH, RET_HEADS, 8, LANES))
    a_b = jnp.broadcast_to(ret_decay_bwd[:, :, None, None], (DEPTH, RET_HEADS, 8, LANES))
    fw = final_norm_w.reshape(1, D_MODEL)

    for l in range(DEPTH):
        need_ctx = l < DEPTH - 1
        qf, qb, kf, kb, rv, wq, wkv, aq, akv = _qkv_call(l, xx, mod, nw, w_in_bf, w_wkv, w_aq, w_akv, tabs, gq, gk, seg)
        o_ret = _ret_call(l, a_f, a_b, qf, qb, kf, kb, rv, need_ctx)
        o_win = _win_call(l, win_sink, wq, wkv, need_ctx)
        o_ax = _ax_call(aq, akv, need_ctx)
        xx = _merge_call(l, xx, mod, nw, o_ret, o_win, o_ax, w_in_bf, wpr, wpw, wpa, wo, fw, final=not need_ctx)
    return xx
```

```python
import functools

import numpy as np
import jax
import jax.numpy as jnp
from jax import lax
from jax.experimental import pallas as pl
from jax.experimental.pallas import tpu as pltpu

D_MODEL = 1024
BATCH = 16
SEQ = 2048
DEPTH = 4
CTX_LEN = 256
TOK = CTX_LEN + SEQ
GRID_W = 64
BLOCK = 128
RET_HEADS = 4
RET_DK = 128
HEAD_DIM = 64
ROPE_BASE = 10000.0
EPS = 1e-6
NEG = -1e30

LANES = 128
NBLK = TOK // BLOCK
CTX_BLKS = CTX_LEN // BLOCK
LAT_BLKS = SEQ // BLOCK
LOG2E = 1.4426950408889634
KV_W = 6 * LANES
QB = CTX_LEN
NQB = TOK // QB
AX_AHEAD = 2
WIN_AHEAD = 2
PV_KC = 256

SUB = CTX_LEN
TM = 3 * SUB
TM_FINAL = 4 * SUB
VMEM_LIMIT = 52 * 1024 * 1024

F32 = jnp.float32
BF16 = jnp.bfloat16

_IN_SIZES = (512, 512, 512, 512, 512, 128, 128, 512, 512, 128, 128, 512, 3 * D_MODEL)
_OFF = np.concatenate([[0], np.cumsum(_IN_SIZES)]).astype(np.int64)
(_RQ, _RK, _RV, _RG, _WQ, _WK, _WV, _WG, _AQ, _AK, _AV, _AG, _MG) = [int(o) for o in _OFF[:-1]]

IN_W = int(_OFF[-1])


def _kv_weights(w, k0, v0):
    k, v = w[:, :, k0:k0 + LANES], w[:, :, v0:v0 + LANES]

    def swapped(x):
        return jnp.concatenate([x[..., HEAD_DIM:], x[..., :HEAD_DIM]], axis=-1)

    return jnp.concatenate([k, swapped(k), v, swapped(v)], axis=-1)

_ATT_HEADS = tuple((c, b, (c // 2 + b) % 2, 2 + 2 * (c // 2) + b) for c in range(4) for b in (0, 1))

_NT = (((1,), (1,)), ((), ()))
_TN = (((0,), (0,)), ((), ()))


def _const_spec(shape):
    nd = len(shape)
    return pl.BlockSpec(shape, lambda *_: (0,) * nd, pipeline_mode=pl.Buffered(1))


def _layer_spec(shape, layer):
    nd = len(shape)
    return pl.BlockSpec((None, *shape), lambda *_: (layer,) + (0,) * nd, pipeline_mode=pl.Buffered(1))


def _params(*sem):
    return pltpu.CompilerParams(dimension_semantics=sem, vmem_limit_bytes=VMEM_LIMIT)


def _sigmoid(z):
    return 1.0 / (1.0 + jnp.exp(-z))


def _mod_kernel(c_ref, w_ref, b_ref, o_ref):
    cv = c_ref[...]
    s = cv * _sigmoid(cv)
    o_ref[0] = jnp.dot(s, w_ref[0], preferred_element_type=F32,
                       precision=lax.Precision.HIGHEST) + b_ref[0]


def _mod_call(cvec, w_mod, b_mod):
    rows = cvec.shape[0]
    return pl.pallas_call(
        _mod_kernel,
        grid=(DEPTH,),
        in_specs=[pl.BlockSpec((rows, D_MODEL), lambda l: (0, 0)),
                  pl.BlockSpec((1, D_MODEL, 3 * D_MODEL), lambda l: (l, 0, 0)),
                  pl.BlockSpec((1, 1, 3 * D_MODEL), lambda l: (l, 0, 0))],
        out_specs=pl.BlockSpec((1, rows, 3 * D_MODEL), lambda l: (l, 0, 0)),
        out_shape=jax.ShapeDtypeStruct((DEPTH, rows, 3 * D_MODEL), F32),
        compiler_params=_params("arbitrary"),
        name="adaln_mod",
    )(cvec, w_mod, b_mod.reshape(DEPTH, 1, 3 * D_MODEL))


def _sub_mod(mod_ref, modc_ref, j):
    if modc_ref is None or j > 0:
        return mod_ref[0]
    return jnp.where(pl.program_id(1) == 0, modc_ref[0], mod_ref[0])


def _x_operands(xx, tm, latent_only=False):
    nsub = tm // SUB
    if not isinstance(xx, tuple):
        if not latent_only:
            return [xx], [pl.BlockSpec((1, tm, D_MODEL), lambda b, t: (b, t, 0))], False
        specs = [pl.BlockSpec((1, SUB, D_MODEL), lambda b, t, sb=sb: (b, nsub * t + sb + 1, 0)) for sb in range(nsub)]
        return [xx] * nsub, specs, False
    ctx, x = xx
    specs = [pl.BlockSpec((1, SUB, D_MODEL), lambda b, t: (b, 0, 0))]
    for sb in range(nsub):
        specs.append(pl.BlockSpec((1, SUB, D_MODEL), lambda b, t, sb=sb: (b, jnp.maximum(nsub * t + sb - 1, 0), 0)))
    return [ctx] + [x] * nsub, specs, True


def _read_x(x_refs, sb, has_ctx):
    if len(x_refs) == 1:
        return x_refs[0][0, sb * SUB:(sb + 1) * SUB, :]
    if not has_ctx:
        return x_refs[sb][0]
    if sb == 0:
        return jnp.where(pl.program_id(1) == 0, x_refs[0][0], x_refs[1][0])
    return x_refs[1 + sb][0]


def _modulated_norm(x, mod, nw_ref):
    ms = jnp.mean(x * x, axis=-1, keepdims=True)
    y = x * lax.rsqrt(ms + EPS) * nw_ref[...]
    return (y * (1.0 + mod[:, D_MODEL:2 * D_MODEL]) + mod[:, 0:D_MODEL]).astype(BF16)


def _qkv_kernel(*refs, n_x, has_ctx):
    x_refs = refs[:n_x]
    (mod_ref, modc_ref, nw_ref, w_rq, w_rk, w_rv, w_wq, w_wkv, w_aq, w_akv,
     cf_ref, sf_ref, cb_ref, sb_ref, ca_ref, sa_ref, gq_ref, gk_ref, seg_ref,
     qf_ref, qb_ref, kf_ref, kb_ref, rv_ref, wq_ref, wkv_ref, aq_ref, akv_ref) = refs[n_x:]
    k_scale = RET_DK ** -0.5
    q_scale = HEAD_DIM ** -0.5 * LOG2E
    lane = lax.broadcasted_iota(jnp.int32, (SUB, LANES), 1)
    low_half = lane < HEAD_DIM
    even_lane = (lane & 1) == 0

    def lanes(a, j):
        return a[:, j * LANES:(j + 1) * LANES]

    def qk_norm(acc, j0, g_ref):
        v2 = acc[:, j0 * LANES:(j0 + 2) * LANES]
        ss = jnp.dot((v2 * v2).astype(BF16), seg_ref[...], preferred_element_type=F32)
        r = lax.rsqrt(ss * (1.0 / HEAD_DIM) + EPS)
        return [lanes(v2, i) * lanes(r, i) * g_ref[...] for i in range(2)]

    for sb in range(qf_ref.shape[1] // SUB):
        rows = slice(sb * SUB, (sb + 1) * SUB)
        h = _modulated_norm(_read_x(x_refs, sb, has_ctx), _sub_mod(mod_ref, modc_ref, sb), nw_ref)

        def proj(w_ref):
            return jnp.dot(h, w_ref[...], preferred_element_type=F32)

        def rope(v, c_ref, s_ref):
            partner = jnp.where(even_lane, pltpu.roll(v, LANES - 1, 1), pltpu.roll(v, 1, 1))
            return v * c_ref[rows, :] + partner * s_ref[rows, :]

        def put(ref, j, val):
            ref[0, rows, j * LANES:(j + 1) * LANES] = val.astype(BF16)

        def store_kv(kv_ref, acc):
            va, vb = lanes(acc, 2), lanes(acc, 3)
            put(kv_ref, 2, jnp.where(low_half, va, 1.0))
            put(kv_ref, 3, jnp.where(low_half, 1.0, vb))
            put(kv_ref, 4, jnp.where(low_half, vb, 1.0))
            put(kv_ref, 5, jnp.where(low_half, 1.0, va))

        acc = proj(w_aq)
        for j, qn in enumerate(qk_norm(acc, 0, gq_ref) + qk_norm(acc, 2, gq_ref)):
            put(aq_ref, j, rope(qn, ca_ref, sa_ref) * q_scale)
        acc = proj(w_akv)
        for j, kn in enumerate(qk_norm(acc, 0, gk_ref)):
            put(akv_ref, j, rope(kn, ca_ref, sa_ref))
        store_kv(akv_ref, acc)

        acc = proj(w_rq)
        for j in range(4):
            put(qf_ref, j, rope(lanes(acc, j), cf_ref, sf_ref))
            put(qb_ref, j, rope(lanes(acc, j), cb_ref, sb_ref))
        acc = proj(w_rk)
        for j in range(4):
            put(kf_ref, j, rope(lanes(acc, j), cf_ref, sf_ref) * k_scale)
            put(kb_ref, j, rope(lanes(acc, j), cb_ref, sb_ref) * k_scale)
        rv_ref[0, rows, :] = proj(w_rv).astype(BF16)

        acc = proj(w_wq)
        for j in range(4):
            put(wq_ref, j, rope(lanes(acc, j), ca_ref, sa_ref) * q_scale)
        acc = proj(w_wkv)
        for j in range(2):
            put(wkv_ref, j, rope(lanes(acc, j), ca_ref, sa_ref))
        store_kv(wkv_ref, acc)


def _mod_specs(layer):
    return [pl.BlockSpec((None, 1, 1, 3 * D_MODEL), lambda b, t: (layer, b, 0, 0)),
            pl.BlockSpec((None, 1, 1, 3 * D_MODEL), lambda b, t: (layer, BATCH, 0, 0))]


def _qkv_call(layer, xx, mod, nw, w_in_bf, w_wkv, w_aq, w_akv, tabs, gq, gk, seg):
    w_spec = _layer_spec((D_MODEL, 512), layer)

    def col_spec(c0):
        return pl.BlockSpec((None, D_MODEL, 512), lambda *_: (layer, 0, c0 // 512), pipeline_mode=pl.Buffered(1))

    tok_spec = pl.BlockSpec((1, TM, 512), lambda b, t: (b, t, 0))
    tab_spec = pl.BlockSpec((TM, LANES), lambda b, t: (t, 0))
    out_sds = jax.ShapeDtypeStruct((BATCH, TOK, 512), BF16)
    kv_spec = pl.BlockSpec((1, TM, KV_W), lambda b, t: (b, t, 0))
    kv_sds = jax.ShapeDtypeStruct((BATCH, TOK, KV_W), BF16)
    x_ops, x_specs, has_ctx = _x_operands(xx, TM)
    return pl.pallas_call(
        functools.partial(_qkv_kernel, n_x=len(x_ops), has_ctx=has_ctx),
        grid=(BATCH, TOK // TM),
        in_specs=[*x_specs,
                  *_mod_specs(layer),
                  _layer_spec((1, D_MODEL), layer),
                  col_spec(_RQ), col_spec(_RK), col_spec(_RV), col_spec(_WQ), w_spec, w_spec, w_spec,
                  tab_spec, tab_spec, tab_spec, tab_spec, tab_spec, tab_spec,
                  _layer_spec((1, LANES), layer), _layer_spec((1, LANES), layer),
                  _const_spec((2 * LANES, 2 * LANES))],
        out_specs=[tok_spec] * 6 + [kv_spec, tok_spec, kv_spec],
        out_shape=[out_sds] * 6 + [kv_sds, out_sds, kv_sds],
        compiler_params=_params("parallel", "arbitrary"),
        name="qkv_proj",
    )(*x_ops, mod, mod, nw, w_in_bf, w_in_bf, w_in_bf, w_in_bf, w_wkv, w_aq, w_akv, *tabs, gq, gk, seg)


def _log_sigmoid(a):
    return jnp.minimum(a, 0.0) - jnp.log1p(jnp.exp(-jnp.abs(a)))


def _ret_kernel(af_ref, ab_ref, qf_ref, qb_ref, kf_ref, kb_ref, v_ref, o_ref, acc_ref, st_ref, dec_ref,
                *, need_ctx):
    ri = lax.broadcasted_iota(jnp.int32, (BLOCK, BLOCK), 0).astype(F32)
    ci = lax.broadcasted_iota(jnp.int32, (BLOCK, BLOCK), 1).astype(F32)
    d = ri - ci
    for h in range(RET_HEADS):
        lgf = _log_sigmoid(af_ref[h])[0:1, :]
        lgb = _log_sigmoid(ab_ref[h])[0:1, :]
        dec_ref[2 * h, 0] = jnp.where(d >= 0, jnp.exp(jnp.maximum(d, 0.0) * lgf), 0.0)
        dec_ref[2 * h, 1] = jnp.exp((ri + 1.0) * lgf)
        dec_ref[2 * h, 2] = jnp.exp((BLOCK - 1.0 - ri) * lgf)
        dec_ref[2 * h, 3] = jnp.exp(0.0 * ri + float(BLOCK) * lgf)
        dec_ref[2 * h + 1, 0] = jnp.where(d < 0, jnp.exp(jnp.maximum(-d, 0.0) * lgb), 0.0)
        dec_ref[2 * h + 1, 1] = jnp.exp((float(BLOCK) - ri) * lgb)
        dec_ref[2 * h + 1, 2] = jnp.exp(ri * lgb)
        dec_ref[2 * h + 1, 3] = jnp.exp(0.0 * ri + float(BLOCK) * lgb)
    st_ref[...] = jnp.zeros_like(st_ref)
    first_row = 0 if need_ctx else CTX_LEN

    def visit(i, in_ctx, first, want_o):
        cf = i
        cb = (CTX_BLKS - 1 - i) if in_ctx else (NBLK + CTX_BLKS - 1 - i)
        pending = []
        for h in range(RET_HEADS):
            lanes_h = slice(h * LANES, (h + 1) * LANES)
            for dr, (q_ref, k_ref, c) in enumerate(((qf_ref, kf_ref, cf), (qb_ref, kb_ref, cb))):
                r0 = c * BLOCK
                j = 2 * h + dr
                k = k_ref[0, pl.ds(r0, BLOCK), lanes_h]
                v = v_ref[0, pl.ds(r0, BLOCK), lanes_h]
                state = st_ref[j]
                kd = (k.astype(F32) * dec_ref[j, 2]).astype(BF16)
                st_ref[j] = dec_ref[j, 3] * state + lax.dot_general(kd, v, _TN, preferred_element_type=F32)
                if want_o:
                    q = q_ref[0, pl.ds(r0, BLOCK), lanes_h]
                    s = lax.dot_general(q, k, _NT, preferred_element_type=F32)
                    cross = jnp.dot(q, state.astype(BF16), preferred_element_type=F32)
                    pending.append((j, r0, lanes_h, v, s, cross))
        for j, r0, lanes_h, v, s, cross in pending:
            o = jnp.dot((s * dec_ref[j, 0]).astype(BF16), v, preferred_element_type=F32) + cross * dec_ref[j, 1]
            if first:
                acc_ref[pl.ds(r0, BLOCK), lanes_h] = o
            else:
                tot = acc_ref[pl.ds(r0, BLOCK), lanes_h] + o
                tot = tot * lax.rsqrt(jnp.mean(tot * tot, axis=-1, keepdims=True) + EPS)
                o_ref[0, pl.ds(r0 - first_row, BLOCK), lanes_h] = tot.astype(o_ref.dtype)

    half = (NBLK + CTX_BLKS) // 2
    for i in range(NBLK):
        if i < CTX_BLKS:
            visit(i, True, i < CTX_BLKS // 2, need_ctx)
        else:
            visit(i, False, i < half, True)


def _ret_call(layer, a_f, a_b, qf, qb, kf, kb, rv, need_ctx):
    n_out = TOK if need_ctx else SEQ
    tok_spec = pl.BlockSpec((1, TOK, 512), lambda b: (b, 0, 0))
    a_spec = _layer_spec((RET_HEADS, 8, LANES), layer)
    return pl.pallas_call(
        functools.partial(_ret_kernel, need_ctx=need_ctx),
        grid=(BATCH,),
        in_specs=[a_spec, a_spec, tok_spec, tok_spec, tok_spec, tok_spec, tok_spec],
        out_specs=pl.BlockSpec((1, n_out, 512), lambda b: (b, 0, 0)),
        out_shape=jax.ShapeDtypeStruct((BATCH, n_out, 512), BF16),
        scratch_shapes=[pltpu.VMEM((TOK, 512), F32),
                        pltpu.VMEM((2 * RET_HEADS, RET_DK, LANES), F32),
                        pltpu.VMEM((2 * RET_HEADS, 4, BLOCK, BLOCK), F32)],
        compiler_params=_params("parallel"),
        name="retention",
    )(a_f, a_b, qf, qb, kf, kb, rv)


def _low_half_mask():
    return lax.broadcasted_iota(jnp.int32, (QB, LANES), 1) < HEAD_DIM


def _head_queries(q_ref, c, use_b, low_half):
    qv = q_ref[0, :, c * LANES:(c + 1) * LANES].astype(F32)
    qv = jnp.where(low_half, 0.0, qv) if use_b else jnp.where(low_half, qv, 0.0)
    return qv.astype(BF16)


def _pv(p, load_v, nkeys):
    acc = None
    for r0 in range(0, nkeys, PV_KC):
        d = jnp.dot(p[:, r0:r0 + PV_KC], load_v(r0, r0 + PV_KC), preferred_element_type=F32)
        acc = d if acc is None else acc + d
    return acc


def _store_heads(o_ref, res, low_half, extra=None):
    for c in range(4):
        ra, rb = res[2 * c], res[2 * c + 1]
        num = jnp.where(low_half, ra[:, :LANES], rb[:, LANES:])
        den = jnp.where(low_half, ra[:, LANES:], rb[:, :LANES])
        if extra is not None:
            den = den + jnp.where(low_half, extra[2 * c], extra[2 * c + 1])
        o_ref[0, :, c * LANES:(c + 1) * LANES] = (num * (1.0 / den)).astype(o_ref.dtype)


def _attn_specs(off):
    return dict(
        in_specs=[pl.BlockSpec((1, QB, 512), lambda b, i: (b, i + off, 0)),
                  pl.BlockSpec((1, QB, 512), lambda b, i: (b, jnp.minimum(i + off + 1, NQB - 1), 0)),
                  pl.BlockSpec((1, TOK, KV_W), lambda b, i: (b, 0, 0))],
        out_specs=pl.BlockSpec((1, QB, 512), lambda b, i: (b, i, 0)),
        out_shape=jax.ShapeDtypeStruct((BATCH, (NQB - off) * QB, 512), BF16),
    )


def _ax_kernel(q_ref, qn_ref, kv_ref, o_ref, s_ref, m_ref, *, blk_off):
    low_half = _low_half_mask()

    def scores(qr, head, nkeys):
        c, use_b, kb, _ = head
        qs = _head_queries(qr, c, use_b, low_half)
        k = kv_ref[0, 0:nkeys, kb * LANES:(kb + 1) * LANES]
        return lax.dot_general(qs, k, _NT, preferred_element_type=F32)

    def carry(slot, s):
        s_ref[slot] = s
        m_ref[slot] = jnp.broadcast_to(jnp.max(s, axis=-1, keepdims=True), m_ref.shape[1:])

    def finish(s, head, nkeys, mx=None):
        _, use_b, _, vb = head
        if mx is None:
            mx = jnp.max(s, axis=-1, keepdims=True)
        p = jnp.exp2(s - mx).astype(BF16)
        v0 = (vb - use_b) * LANES
        return _pv(p, lambda r0, r1: kv_ref[0, r0:r1, v0:v0 + 2 * LANES], nkeys)

    nh = len(_ATT_HEADS)

    def prologue(qr):
        for a in range(AX_AHEAD):
            carry(a, scores(qr, _ATT_HEADS[a], TOK))

    def ctx_block():
        res = [finish(scores(q_ref, head, CTX_LEN), head, CTX_LEN) for head in _ATT_HEADS]
        _store_heads(o_ref, res, low_half)
        prologue(qn_ref)

    def latent_block():
        res = []
        ahead = [(s_ref[a], m_ref[a, :, 0:1]) for a in range(AX_AHEAD)]
        for n, head in enumerate(_ATT_HEADS):
            m = n + AX_AHEAD
            ahead.append((scores(q_ref, _ATT_HEADS[m], TOK) if m < nh else scores(qn_ref, _ATT_HEADS[m - nh], TOK), None))
            s, mx = ahead.pop(0)
            res.append(finish(s, head, TOK, mx))
        for a in range(AX_AHEAD):
            carry(a, ahead[a][0])
        _store_heads(o_ref, res, low_half)

    blk = pl.program_id(1)
    if blk_off == 0:
        pl.when(blk == 0)(ctx_block)
        pl.when(blk > 0)(latent_block)
    else:
        pl.when(blk == 0)(lambda: prologue(q_ref))
        latent_block()


def _ax_call(q, kv, need_ctx):
    off = 0 if need_ctx else 1
    specs = _attn_specs(off)
    return pl.pallas_call(
        functools.partial(_ax_kernel, blk_off=off),
        grid=(BATCH, NQB - off),
        scratch_shapes=[pltpu.VMEM((AX_AHEAD, QB, TOK), F32), pltpu.VMEM((AX_AHEAD, QB, LANES), F32)],
        compiler_params=_params("parallel", "arbitrary"),
        name="axial_attn",
        **specs,
    )(q, q, kv)


def _win_kernel(sink_ref, q_ref, qn_ref, kv_ref, o_ref, s_ref, *, layer, blk_off):
    low_half = _low_half_mask()
    nh = len(_ATT_HEADS)
    nk = CTX_LEN + QB + 2 * BLOCK

    def window_loader(n):
        left = jnp.maximum(2 * n - 1, 0)
        right = jnp.minimum(2 * n + 2, LAT_BLKS - 1)
        s_left = pl.multiple_of(CTX_LEN + left * BLOCK, BLOCK)
        s_mid = pl.multiple_of(CTX_LEN + n * QB, QB)
        s_right = pl.multiple_of(CTX_LEN + right * BLOCK, BLOCK)
        cache = {}

        def load(c0, w):
            if (c0, w) not in cache:
                cache[(c0, w)] = jnp.concatenate([kv_ref[0, 0:CTX_LEN, c0:c0 + w],
                                                  kv_ref[0, pl.ds(s_left, BLOCK), c0:c0 + w],
                                                  kv_ref[0, pl.ds(s_mid, QB), c0:c0 + w],
                                                  kv_ref[0, pl.ds(s_right, BLOCK), c0:c0 + w]], axis=0)
            return cache[(c0, w)]

        return load

    def window_mask(n):
        col = lax.broadcasted_iota(jnp.int32, (QB, nk), 1)
        qi = lax.broadcasted_iota(jnp.int32, (QB, nk), 0)
        jj = col - CTX_LEN
        dlt = jj - qi
        lo = jnp.where(n >= 1, 0, BLOCK)
        hi = jnp.where(n <= SEQ // QB - 2, QB + 2 * BLOCK, QB + BLOCK)
        in_win = (dlt >= 0) & (dlt <= 2 * BLOCK) & (jj >= lo) & (jj < hi)
        return (col < CTX_LEN) | in_win

    def scores(qr, head, load):
        c, use_b, kb, _ = head
        qs = _head_queries(qr, c, use_b, low_half)
        return lax.dot_general(qs, load(kb * LANES, LANES), _NT, preferred_element_type=F32)

    def finish(s, head, load, valid):
        c, use_b, _, vb = head
        if valid is not None:
            s = jnp.where(valid, s, NEG)
        sk = sink_ref[layer, 2 * c + use_b] * LOG2E
        mx = jnp.maximum(jnp.max(s, axis=-1, keepdims=True), sk)
        p = jnp.exp2(s - mx).astype(BF16)
        v = load((vb - use_b) * LANES, 2 * LANES)
        return _pv(p, lambda r0, r1: v[r0:r1], v.shape[0]), jnp.exp2(sk - mx)

    def store(done):
        _store_heads(o_ref, [r for r, _ in done], low_half, [t for _, t in done])

    def prologue(qr, n):
        load = window_loader(n)
        for a in range(WIN_AHEAD):
            s_ref[a] = scores(qr, _ATT_HEADS[a], load)

    def ctx_block():
        def load(c0, w):
            return kv_ref[0, 0:CTX_LEN, c0:c0 + w]

        store([finish(scores(q_ref, head, load), head, load, None) for head in _ATT_HEADS])
        prologue(qn_ref, 0)

    def latent_block(n):
        load = window_loader(n)
        load_next = window_loader(jnp.minimum(n + 1, SEQ // QB - 1))
        valid = window_mask(n)
        ahead = [s_ref[a] for a in range(WIN_AHEAD)]
        done = []
        for i, head in enumerate(_ATT_HEADS):
            m = i + WIN_AHEAD
            ahead.append(scores(q_ref, _ATT_HEADS[m], load) if m < nh else scores(qn_ref, _ATT_HEADS[m - nh], load_next))
            done.append(finish(ahead.pop(0), head, load, valid))
        for a in range(WIN_AHEAD):
            s_ref[a] = ahead[a]
        store(done)

    blk = pl.program_id(1)
    if blk_off == 0:
        pl.when(blk == 0)(ctx_block)
        pl.when(blk > 0)(lambda: latent_block(blk - 1))
    else:
        pl.when(blk == 0)(lambda: prologue(q_ref, 0))
        latent_block(blk)


def _win_call(layer, sink, q, kv, need_ctx):
    off = 0 if need_ctx else 1
    specs = _attn_specs(off)
    specs["in_specs"] = [pl.BlockSpec(memory_space=pltpu.SMEM)] + specs["in_specs"]
    return pl.pallas_call(
        functools.partial(_win_kernel, layer=layer, blk_off=off),
        grid=(BATCH, NQB - off),
        scratch_shapes=[pltpu.VMEM((WIN_AHEAD, QB, CTX_LEN + QB + 2 * BLOCK), F32)],
        compiler_params=_params("parallel", "arbitrary"),
        name="window_attn",
        **specs,
    )(sink, q, q, kv)


def _merge_kernel(*refs, n_x, has_ctx, final):
    x_refs = refs[:n_x]
    (mod_ref, modc_ref, nw_ref, or_ref, ow_ref, oa_ref, win_ref, wpr_ref, wpw_ref, wpa_ref,
     wo_ref, fw_ref, out_ref) = refs[n_x:]
    nsub = out_ref.shape[1] // SUB
    branches = ((or_ref, wpr_ref, _RG), (ow_ref, wpw_ref, _WG), (oa_ref, wpa_ref, _AG))

    def from_h(sb):
        mod = _sub_mod(mod_ref, None if final else modc_ref, sb)
        x = _read_x(x_refs, sb, has_ctx)
        h = _modulated_norm(x, mod, nw_ref)
        gates = [jnp.dot(h, win_ref[:, g0:g0 + 512], preferred_element_type=F32) for _, _, g0 in branches]
        logits = [jnp.dot(h, win_ref[:, _MG + j * D_MODEL:_MG + (j + 1) * D_MODEL], preferred_element_type=F32)
                  for j in range(3)]
        return x, mod, gates, logits

    def finish(sb, x, mod, gates, logits):
        rows = slice(sb * SUB, (sb + 1) * SUB)
        mix = None
        for (o_ref, wp_ref, _), g, ml in zip(branches, gates, logits):
            u = (o_ref[0, rows, :].astype(F32) * (g * _sigmoid(g))).astype(BF16)
            t = _sigmoid(ml) * jnp.dot(u, wp_ref[...], preferred_element_type=F32)
            mix = t if mix is None else mix + t
        y = jnp.dot(mix.astype(BF16), wo_ref[...], preferred_element_type=F32)
        xn = x + mod[:, 2 * D_MODEL:3 * D_MODEL] * y
        if final:
            xn = xn * lax.rsqrt(jnp.mean(xn * xn, axis=-1, keepdims=True) + EPS) * fw_ref[...]
        out_ref[0, rows, :] = xn

    cur = from_h(0)
    for sb in range(nsub):
        nxt = from_h(sb + 1) if sb + 1 < nsub else None
        finish(sb, *cur)
        cur = nxt


def _merge_call(layer, xx, mod, nw, o_ret, o_win, o_ax, w_in_bf, wpr, wpw, wpa, wo, fw, final):
    tm = TM_FINAL if final else TM
    n_tok = SEQ if final else TOK
    o_spec = pl.BlockSpec((1, tm, 512), lambda b, t: (b, t, 0))
    x_ops, x_specs, has_ctx = _x_operands(xx, tm, latent_only=final)
    return pl.pallas_call(
        functools.partial(_merge_kernel, n_x=len(x_ops), has_ctx=has_ctx, final=final),
        grid=(BATCH, n_tok // tm),
        in_specs=[*x_specs,
                  *_mod_specs(layer),
                  _layer_spec((1, D_MODEL), layer),
                  o_spec, o_spec, o_spec,
                  _layer_spec((D_MODEL, IN_W), layer),
                  _layer_spec((512, D_MODEL), layer), _layer_spec((512, D_MODEL), layer),
                  _layer_spec((512, D_MODEL), layer),
                  _layer_spec((D_MODEL, D_MODEL), layer),
                  _const_spec((1, D_MODEL))],
        out_specs=pl.BlockSpec((1, tm, D_MODEL), lambda b, t: (b, t, 0)),
        out_shape=jax.ShapeDtypeStruct((BATCH, n_tok, D_MODEL), F32),
        compiler_params=_params("parallel", "arbitrary"),
        name="merge_final" if final else "merge",
    )(*x_ops, mod, mod, nw, o_ret, o_win, o_ax, w_in_bf, wpr, wpw, wpa, wo, fw)


def _rope_tables():
    t = jnp.arange(TOK, dtype=F32)
    theta = ROPE_BASE ** (-jnp.linspace(0.0, 1.0, RET_DK // 2, dtype=F32))
    pos_b = jnp.where(t < CTX_LEN, CTX_LEN - 1.0 - t, 2.0 * CTX_LEN + SEQ - 1.0 - t)

    def pair_tables(ang):
        c, s = jnp.cos(ang), jnp.sin(ang)
        return jnp.repeat(c, 2, axis=-1), jnp.stack([-s, s], axis=-1).reshape(ang.shape[0], -1)

    cf, sf = pair_tables(t[:, None] * theta[None])
    cb, sb = pair_tables(pos_b[:, None] * theta[None])

    s_idx = jnp.arange(SEQ)
    quarter = HEAD_DIM // 4
    freqs = ROPE_BASE ** (-jnp.arange(quarter, dtype=F32) / quarter)
    r = (s_idx // GRID_W).astype(F32)
    col = (s_idx % GRID_W).astype(F32)
    ang = jnp.concatenate([r[:, None] * freqs[None], col[:, None] * freqs[None]], axis=-1)
    c, s = pair_tables(ang)
    ca = jnp.concatenate([jnp.ones((CTX_LEN, LANES), F32), jnp.concatenate([c, c], axis=-1)], axis=0)
    sa = jnp.concatenate([jnp.zeros((CTX_LEN, LANES), F32), jnp.concatenate([s, s], axis=-1)], axis=0)
    return cf, sf, cb, sb, ca, sa


def _segment_matrix():
    head = np.arange(2 * LANES) // HEAD_DIM
    return jnp.asarray((head[:, None] == head[None, :]).astype(np.float32), dtype=BF16)


def kernel(x, c, ctx, c_ctx, norm_w, w_mod, b_mod, w_in, ret_decay_fwd, ret_decay_bwd, win_sink,
           ax_q_gain, ax_k_gain, w_proj_ret, w_proj_win, w_proj_ax, w_out, final_norm_w):
    xx = (ctx, x)
    cvec = jnp.zeros((24, D_MODEL), F32).at[:BATCH].set(c).at[BATCH].set(c_ctx)
    mod = _mod_call(cvec, w_mod, b_mod).reshape(DEPTH, 24, 1, 3 * D_MODEL)
    tabs = _rope_tables()
    seg = _segment_matrix()
    w_in_bf = w_in.astype(BF16)
    w_wkv = _kv_weights(w_in_bf, _WK, _WV)
    w_akv = _kv_weights(w_in_bf, _AK, _AV)
    w_aq = w_in_bf[:, :, _AQ:_AQ + 512]
    wpr, wpw, wpa, wo = (w.astype(BF16) for w in (w_proj_ret, w_proj_win, w_proj_ax, w_out))
    nw = norm_w.reshape(DEPTH, 1, D_MODEL)
    gq = jnp.tile(ax_q_gain, (1, 2)).reshape(DEPTH, 1, LANES)
    gk = jnp.tile(ax_k_gain, (1, 2)).reshape(DEPTH, 1, LANES)
    a_f = jnp.broadcast_to(ret_decay_fwd[:, :, None, None], (DEPTH, RET_HEADS, 8, LANES))
    a_b = jnp.broadcast_to(ret_decay_bwd[:, :, None, None], (DEPTH, RET_HEADS, 8, LANES))
    fw = final_norm_w.reshape(1, D_MODEL)

    for l in range(DEPTH):
        need_ctx = l < DEPTH - 1
        qf, qb, kf, kb, rv, wq, wkv, aq, akv = _qkv_call(l, xx, mod, nw, w_in_bf, w_wkv, w_aq, w_akv, tabs, gq, gk, seg)
        o_ret = _ret_call(l, a_f, a_b, qf, qb, kf, kb, rv, need_ctx)
        o_win = _win_call(l, win_sink, wq, wkv, need_ctx)
        o_ax = _ax_call(aq, akv, need_ctx)
        xx = _merge_call(l, xx, mod, nw, o_ret, o_win, o_ax, w_in_bf, wpr, wpw, wpa, wo, fw, final=not need_ctx)
    return xx
```
